```python
import jax
import jax.numpy as jnp
from jax import lax
import numpy as np

D_MODEL = 1024
BATCH = 2
SEQ = 8192
DEPTH = 1
DEC_BATCH = 8
DEC_SEQ = 8192
PAST_LEN = 128

GRID_W = 64
HEAD_DIM = 64
D_RWKV = D_MODEL // 2
H_RWKV = D_RWKV // HEAD_DIM
D_NA = D_MODEL // 2
H_NA = D_NA // HEAD_DIM
NA_WIN_H = 8
NA_WIN_W = 16
DECAY_LORA = 64
ICL_LORA = 64
GATE_LORA = 128
N_EXPERTS = 256
TOP_K = 8
N_GROUPS = 8
TOPK_GROUPS = 4
D_EXPERT = D_MODEL // 4
D_SHARED = D_EXPERT
ROUTED_SCALE = 2.5
MOE_BLOCK = 128
LN_EPS = 1e-5
GN_EPS = 64e-5
ALPHA = (2 * DEPTH) ** 0.25
BETA = (8 * DEPTH) ** -0.25
RWKV_SIZES = (D_RWKV, D_RWKV, D_RWKV, DECAY_LORA, DECAY_LORA, ICL_LORA, ICL_LORA, GATE_LORA)
RWKV_IN = 3 * D_RWKV + 2 * DECAY_LORA + 2 * ICL_LORA + GATE_LORA
NA_IN = 3 * D_NA
IN_COLS = RWKV_IN + NA_IN + 2 * D_MODEL

kernel_name = 'hybrid_rwkv7_natten_moe_encoder'


def layer_norm(x, g, b, eps=LN_EPS):
    xf = x.astype(jnp.float32)
    mu = jnp.mean(xf, -1, keepdims=True)
    var = jnp.mean(jnp.square(xf - mu), -1, keepdims=True)
    return ((xf - mu) * lax.rsqrt(var + eps) * g + b).astype(x.dtype)


def split_cols(t, sizes):
    return jnp.split(t, np.cumsum(sizes)[:-1].tolist(), axis=-1)


def centred_shift_lerp(p, mu):
    zero = jnp.zeros_like(p[:, :1])
    prev = jnp.concatenate([zero, p[:, :-1]], axis=1)
    nxt = jnp.concatenate([p[:, 1:], zero], axis=1)
    return p + mu * (0.5 * (prev + nxt) - p)


def rwkv7_scan(r, decay, k, v, kk, a, reverse):
    def step(S, inp):
        r_t, w_t, k_t, v_t, kk_t, a_t = inp
        sa = jnp.einsum('bhvk,bhk->bhv', S, kk_t)
        S = (S * w_t[:, :, None, :] - sa[..., None] * (kk_t * a_t)[:, :, None, :]
             + v_t[..., None] * k_t[:, :, None, :])
        return S, jnp.einsum('bhvk,bhk->bhv', S, r_t)
    B, L, H, N = r.shape
    s0 = jnp.zeros((B, H, N, N), jnp.float32)
    xs = tuple(jnp.moveaxis(t, 1, 0) for t in (r, decay, k, v, kk, a))
    _, y = lax.scan(step, s0, xs, reverse=reverse)
    return jnp.moveaxis(y, 0, 1)


def rwkv7_branch(xr, xk, xv, dw_f, dw_b, da_f, da_b, dg, w0, w2, a0, a2, g2, k_k, k_a, r_k, lnx_g, lnx_b):
    B, L, _ = xr.shape
    f32 = jnp.float32

    def heads(t):
        return t.astype(f32).reshape(B, L, H_RWKV, HEAD_DIM)

    r, k, v = heads(xr), heads(xk), heads(xv)
    kk = k * k_k.astype(f32).reshape(H_RWKV, HEAD_DIM)
    kk = kk / jnp.maximum(jnp.sqrt(jnp.sum(kk * kk, -1, keepdims=True)), 1e-12)
    k_a_h = k_a.astype(f32).reshape(H_RWKV, HEAD_DIM)
    r_k_f = r_k.astype(f32)
    wkv = 0.0
    bonus = 0.0
    for d, (dw, da) in enumerate(((dw_f, da_f), (dw_b, da_b))):
        w_log = -jax.nn.softplus(-(w0[d] + jnp.tanh(dw) @ w2[d]).astype(f32)) - 0.5
        decay = heads(jnp.exp(-jnp.exp(w_log)))
        a = heads(jax.nn.sigmoid((a0[d] + da @ a2[d]).astype(f32)))
        k_d = k * (1.0 + (a - 1.0) * k_a_h)
        wkv = wkv + rwkv7_scan(r, decay, k_d, v, kk, a, reverse=(d == 1))
        bonus = bonus + jnp.sum(r * k_d * r_k_f, -1, keepdims=True) * v
    mu = jnp.mean(wkv, -1, keepdims=True)
    var = jnp.mean(jnp.square(wkv - mu), -1, keepdims=True)
    wkv = ((wkv - mu) * lax.rsqrt(var + GN_EPS)).reshape(B, L, D_RWKV) * lnx_g + lnx_b
    g = jax.nn.sigmoid(dg) @ g2
    out = (wkv + bonus.reshape(B, L, D_RWKV)) * g
    return out.astype(xr.dtype)


def neighbourhood_attention(q, k, v, rpb):
    B, L, _ = q.shape
    rows = L // GRID_W
    wh = min(NA_WIN_H, rows)
    ww = NA_WIN_W
    scale = HEAD_DIM ** -0.5

    def grid(t):
        return t.reshape(B, rows, GRID_W, H_NA, HEAD_DIM)

    qg, kg, vg = grid(q), grid(k), grid(v)
    col = jnp.arange(GRID_W)
    col_start = jnp.clip(col - ww // 2, 0, GRID_W - ww)
    key_cols = col_start[:, None] + jnp.arange(ww)
    col_off = key_cols - col[:, None] + (NA_WIN_W - 1)

    def one_row(r):
        row_start = jnp.clip(r - wh // 2, 0, rows - wh)
        k_band = lax.dynamic_slice_in_dim(kg, row_start, wh, axis=1)
        v_band = lax.dynamic_slice_in_dim(vg, row_start, wh, axis=1)
        k_sel = k_band[:, :, key_cols]
        v_sel = v_band[:, :, key_cols]
        q_row = lax.dynamic_index_in_dim(qg, r, axis=1, keepdims=False)
        row_off = row_start + jnp.arange(wh) - r + (NA_WIN_H - 1)
        bias = rpb[:, row_off[None, :, None], col_off[:, None, :]]
        s = jnp.einsum('bchn,bicjhn->bhcij', q_row, k_sel).astype(jnp.float32) * scale
        s = s + bias.astype(jnp.float32)[None]
        p = jax.nn.softmax(s.reshape(B, H_NA, GRID_W, wh * ww), axis=-1)
        p = p.reshape(B, H_NA, GRID_W, wh, ww).astype(v.dtype)
        return jnp.einsum('bhcij,bicjhn->bchn', p, v_sel)

    out = lax.map(one_row, jnp.arange(rows))
    return jnp.moveaxis(out, 0, 1).reshape(B, L, D_NA)


def swiglu(x, wg, wu, wd):
    return (jax.nn.silu(x @ wg) * (x @ wu)) @ wd


def moe_route(x, w_router, e_bias):
    T = x.shape[0]
    scores = jax.nn.sigmoid((x @ w_router).astype(jnp.float32))
    sel = scores + e_bias.astype(jnp.float32)
    grp = sel.reshape(T, N_GROUPS, N_EXPERTS // N_GROUPS)
    grp_score = jnp.sum(lax.top_k(grp, 2)[0], -1)
    _, grp_idx = lax.top_k(grp_score, TOPK_GROUPS)
    grp_mask = jnp.any(grp_idx[..., None] == jnp.arange(N_GROUPS), axis=-2)
    sel = jnp.where(jnp.repeat(grp_mask, N_EXPERTS // N_GROUPS, axis=-1), sel, -jnp.inf)
    _, idx = lax.top_k(sel, TOP_K)
    wts = jnp.take_along_axis(scores, idx, axis=-1)
    wts = wts / jnp.sum(wts, -1, keepdims=True) * ROUTED_SCALE
    return idx, wts


def routed_experts(x, idx, wts, w_gate, w_up, w_down):
    T, D = x.shape
    n_assign = T * TOP_K
    flat_e = idx.reshape(-1)
    flat_tok = jnp.arange(n_assign, dtype=jnp.int32) // TOP_K
    flat_w = wts.reshape(-1)
    order = jnp.argsort(flat_e)
    e_sorted = flat_e[order]
    counts = jnp.bincount(flat_e, length=N_EXPERTS)
    padded = (counts + MOE_BLOCK - 1) // MOE_BLOCK * MOE_BLOCK
    start = jnp.cumsum(counts) - counts
    pend = jnp.cumsum(padded)
    pstart = pend - padded
    rank = jnp.arange(n_assign, dtype=jnp.int32) - start[e_sorted]
    dest = pstart[e_sorted] + rank
    n_blocks = (n_assign + N_EXPERTS * (MOE_BLOCK - 1) + MOE_BLOCK - 1) // MOE_BLOCK
    cap = n_blocks * MOE_BLOCK
    row_tok = jnp.full((cap,), T, jnp.int32).at[dest].set(flat_tok[order])
    row_w = jnp.zeros((cap,), jnp.float32).at[dest].set(flat_w[order])
    block_start = jnp.arange(n_blocks, dtype=jnp.int32) * MOE_BLOCK
    block_e = jnp.minimum(jnp.searchsorted(pend, block_start, side='right'), N_EXPERTS - 1)
    x_pad = jnp.concatenate([x, jnp.zeros((1, D), x.dtype)], axis=0)

    def body(y, blk):
        toks, w_rows, e = blk
        xb = x_pad[toks]
        out = swiglu(xb, w_gate[e], w_up[e], w_down[e])
        return y.at[toks].add(out * w_rows[:, None].astype(x.dtype)), None

    y0 = jnp.zeros((T + 1, D), x.dtype)
    y, _ = lax.scan(body, y0, (row_tok.reshape(n_blocks, MOE_BLOCK),
                               row_w.reshape(n_blocks, MOE_BLOCK), block_e))
    return y[:T]


def encoder_layer(x, w_in, mu_shift, w0, w2, a0, a2, g2, k_k, k_a, r_k, lnx_g, lnx_b, rpb,
                  w_up_a, w_up_n, w_out, ln1_g, ln1_b, w_router, e_bias,
                  w_gate_e, w_up_e, w_down_e, w_gate_s, w_up_s, w_down_s, ln2_g, ln2_b):
    B, L, D = x.shape
    proj = x @ w_in
    p_rwkv, p_na, p_gate = split_cols(proj, (RWKV_IN, NA_IN, 2 * D_MODEL))
    p_rwkv = centred_shift_lerp(p_rwkv, mu_shift)
    xr, xk, xv, dw_f, dw_b, da_f, da_b, dg = split_cols(p_rwkv, RWKV_SIZES)
    y_a = rwkv7_branch(xr, xk, xv, dw_f, dw_b, da_f, da_b, dg, w0, w2, a0, a2, g2,
                       k_k, k_a, r_k, lnx_g, lnx_b)
    q, k, v = split_cols(p_na, (D_NA, D_NA, D_NA))
    y_n = neighbourhood_attention(q, k, v, rpb)
    g_a, g_n = split_cols(p_gate, (D_MODEL, D_MODEL))
    merged = jax.nn.sigmoid(g_a) * (y_a @ w_up_a) + jax.nn.sigmoid(g_n) * (y_n @ w_up_n)
    x = layer_norm(ALPHA * x + merged @ w_out, ln1_g, ln1_b)
    xt = x.reshape(B * L, D)
    idx, wts = moe_route(xt, w_router, e_bias)
    y = swiglu(xt, w_gate_s, w_up_s, w_down_s) + routed_experts(xt, idx, wts, w_gate_e, w_up_e, w_down_e)
    return layer_norm(ALPHA * x + y.reshape(B, L, D), ln2_g, ln2_b)


def trunk(x, ln_in_g, ln_in_b, layer_params):
    x = layer_norm(x, ln_in_g, ln_in_b)
    for l in range(DEPTH):
        x = encoder_layer(x, *[p[l] for p in layer_params])
    return x


def setup_inputs(seed: int = 0) -> dict:
    key = jax.random.key(seed)
    ks = jax.random.split(key, 32)
    f32 = jnp.float32

    def nrm(i, shape, scale):
        return jax.random.normal(ks[i], shape, f32) * scale

    chan = jnp.arange(D_RWKV, dtype=f32) / (D_RWKV - 1)
    decay_base = -7.0 + 5.0 * chan ** 0.85 + 0.5
    return {
        'x_prompt': nrm(0, (BATCH, SEQ, D_MODEL), 1.0),
        'x_sample': nrm(1, (DEC_BATCH, DEC_SEQ, D_MODEL), 1.0),
        'ln_in_g': 1.0 + nrm(2, (D_MODEL,), 0.05),
        'ln_in_b': nrm(3, (D_MODEL,), 0.02),
        'w_in': nrm(4, (DEPTH, D_MODEL, IN_COLS), D_MODEL ** -0.5),
        'mu_shift': jax.random.uniform(ks[5], (DEPTH, RWKV_IN), f32, 0.2, 0.8),
        'w0': decay_base + nrm(6, (DEPTH, 2, D_RWKV), 0.1),
        'w2': nrm(7, (DEPTH, 2, DECAY_LORA, D_RWKV), 0.05),
        'a0': nrm(8, (DEPTH, 2, D_RWKV), 0.1),
        'a2': nrm(9, (DEPTH, 2, ICL_LORA, D_RWKV), 0.5 * ICL_LORA ** -0.5),
        'g2': nrm(10, (DEPTH, GATE_LORA, D_RWKV), GATE_LORA ** -0.5),
        'k_k': 0.85 + nrm(11, (DEPTH, D_RWKV), 0.05),
        'k_a': 1.0 + nrm(12, (DEPTH, D_RWKV), 0.05),
        'r_k': nrm(13, (DEPTH, H_RWKV, HEAD_DIM), 0.1),
        'lnx_g': 1.0 + nrm(14, (DEPTH, D_RWKV), 0.05),
        'lnx_b': nrm(15, (DEPTH, D_RWKV), 0.02),
        'rpb': nrm(16, (DEPTH, H_NA, 2 * NA_WIN_H - 1, 2 * NA_WIN_W - 1), 0.5),
        'w_up_a': nrm(17, (DEPTH, D_RWKV, D_MODEL), D_RWKV ** -0.5),
        'w_up_n': nrm(18, (DEPTH, D_NA, D_MODEL), D_NA ** -0.5),
        'w_out': nrm(19, (DEPTH, D_MODEL, D_MODEL), BETA * D_MODEL ** -0.5),
        'ln1_g': 1.0 + nrm(20, (DEPTH, D_MODEL), 0.05),
        'ln1_b': nrm(21, (DEPTH, D_MODEL), 0.02),
        'w_router': nrm(22, (DEPTH, D_MODEL, N_EXPERTS), D_MODEL ** -0.5),
        'e_bias': nrm(23, (DEPTH, N_EXPERTS), 0.01),
        'w_gate_e': nrm(24, (DEPTH, N_EXPERTS, D_MODEL, D_EXPERT), D_MODEL ** -0.5),
        'w_up_e': nrm(25, (DEPTH, N_EXPERTS, D_MODEL, D_EXPERT), D_MODEL ** -0.5),
        'w_down_e': nrm(26, (DEPTH, N_EXPERTS, D_EXPERT, D_MODEL), BETA * D_EXPERT ** -0.5),
        'w_gate_s': nrm(27, (DEPTH, D_MODEL, D_SHARED), D_MODEL ** -0.5),
        'w_up_s': nrm(28, (DEPTH, D_MODEL, D_SHARED), D_MODEL ** -0.5),
        'w_down_s': nrm(29, (DEPTH, D_SHARED, D_MODEL), BETA * D_SHARED ** -0.5),
        'ln2_g': 1.0 + nrm(30, (DEPTH, D_MODEL), 0.05),
        'ln2_b': nrm(31, (DEPTH, D_MODEL), 0.02),
    }


def reference(x_prompt, x_sample, ln_in_g, ln_in_b, w_in, mu_shift, w0, w2, a0, a2, g2, k_k, k_a,
              r_k, lnx_g, lnx_b, rpb, w_up_a, w_up_n, w_out, ln1_g, ln1_b, w_router, e_bias,
              w_gate_e, w_up_e, w_down_e, w_gate_s, w_up_s, w_down_s, ln2_g, ln2_b):
    layer_params = (w_in, mu_shift, w0, w2, a0, a2, g2, k_k, k_a, r_k, lnx_g, lnx_b, rpb,
                    w_up_a, w_up_n, w_out, ln1_g, ln1_b, w_router, e_bias,
                    w_gate_e, w_up_e, w_down_e, w_gate_s, w_up_s, w_down_s, ln2_g, ln2_b)
    y_prompt = trunk(x_prompt, ln_in_g, ln_in_b, layer_params)
    y_sample = trunk(x_sample, ln_in_g, ln_in_b, layer_params)
    return (y_prompt, y_sample)
```

```python
import functools

import jax
import jax.numpy as jnp
import numpy as np
from jax import lax
from jax.experimental import pallas as pl
from jax.experimental.pallas import tpu as pltpu

F32 = jnp.float32
BF16 = jnp.bfloat16

D_MODEL = 1024
GRID_W = 64
HEAD_DIM = 64
D_RWKV = 512
D_NA = 512
NA_WIN_H = 8
NA_WIN_W = 16
DECAY_LORA = 64
ICL_LORA = 64
GATE_LORA = 128
N_EXPERTS = 256
TOP_K = 8
N_GROUPS = 8
TOPK_GROUPS = 4
D_EXPERT = 256
ROUTED_SCALE = 2.5
LN_EPS = 1e-5
GN_EPS = 64e-5
DEPTH = 1
ALPHA = (2 * DEPTH) ** 0.25
RWKV_SIZES = (D_RWKV, D_RWKV, D_RWKV, DECAY_LORA, DECAY_LORA, ICL_LORA, ICL_LORA, GATE_LORA)
RWKV_IN = sum(RWKV_SIZES)
NA_IN = 3 * D_NA

LANES = 128
HEADS_PER_SLAB = LANES // HEAD_DIM
VMEM_LIMIT = 48 * 1024 * 1024

CHUNK = 64
MOE_ROWS = 128
NEG_BIG = -1e30

HI = lax.Precision.HIGHEST


def _mm(a, b, precision=None):
    return jnp.dot(a, b, preferred_element_type=F32, precision=precision)


def _mm_nt(a, b, precision=None):
    return lax.dot_general(a, b, (((1,), (1,)), ((), ())), preferred_element_type=F32,
                           precision=precision)


def _ln(x, g, b):
    mu = jnp.mean(x, -1, keepdims=True)
    xc = x - mu
    var = jnp.mean(xc * xc, -1, keepdims=True)
    return xc * lax.rsqrt(var + LN_EPS) * g + b


def _ln_proj_kernel(x_ref, g_ref, b_ref, wr_ref, wn_ref, wg_ref,
                    xn_ref, pr_ref, q_ref, k_ref, v_ref, ga_ref, gn_ref):
    xn = _ln(x_ref[...], g_ref[...], b_ref[...])
    xn_ref[...] = xn
    xb = xn.astype(BF16)
    pr_ref[...] = _mm(xb, wr_ref[...])
    pn = _mm(xb, wn_ref[...])
    q_ref[...] = pn[:, :D_NA].astype(BF16)
    k_ref[...] = pn[:, D_NA:2 * D_NA].astype(BF16)
    v_ref[...] = pn[:, 2 * D_NA:].astype(BF16)
    pg = _mm(xb, wg_ref[...])
    ga_ref[...] = pg[:, :D_MODEL].astype(BF16)
    gn_ref[...] = pg[:, D_MODEL:].astype(BF16)


def _ln_proj(x, g, b, w_rwkv, w_na, w_gate, tm):
    t = x.shape[0]
    row = lambda n: pl.BlockSpec((tm, n), lambda i: (i, 0))
    full = lambda a: pl.BlockSpec(a.shape, lambda i: (0, 0))
    return pl.pallas_call(
        _ln_proj_kernel,
        grid=(t // tm,),
        in_specs=[row(D_MODEL), full(g), full(b), full(w_rwkv), full(w_na), full(w_gate)],
        out_specs=[row(D_MODEL), row(RWKV_IN), row(D_NA), row(D_NA), row(D_NA),
                   row(D_MODEL), row(D_MODEL)],
        out_shape=[jax.ShapeDtypeStruct((t, D_MODEL), F32),
                   jax.ShapeDtypeStruct((t, RWKV_IN), F32),
                   jax.ShapeDtypeStruct((t, D_NA), BF16),
                   jax.ShapeDtypeStruct((t, D_NA), BF16),
                   jax.ShapeDtypeStruct((t, D_NA), BF16),
                   jax.ShapeDtypeStruct((t, D_MODEL), BF16),
                   jax.ShapeDtypeStruct((t, D_MODEL), BF16)],
        compiler_params=pltpu.CompilerParams(dimension_semantics=("parallel",),
                                             vmem_limit_bytes=VMEM_LIMIT),
        name="ln_proj",
    )(x, g, b, w_rwkv, w_na, w_gate)


def _rwkv_kernel(r_ref, v_ref, kk_ref, kd_ref, be_ref, lw_ref, y_ref, z_ref, *, n_slabs):
    d = pl.program_id(1)
    c = pl.program_id(2)

    @pl.when(c == 0)
    def _():
        z_ref[...] = jnp.zeros_like(z_ref)

    n2 = HEADS_PER_SLAB * CHUNK
    sign = 1 - 2 * d
    row = lax.broadcasted_iota(jnp.int32, (n2, n2), 0)
    col = lax.broadcasted_iota(jnp.int32, (n2, n2), 1)
    t_loc = row & (CHUNK - 1)
    s_loc = col & (CHUNK - 1)
    dts = (t_loc - s_loc) * sign
    strict = dts > 0
    incl = dts >= 0
    blk16 = (t_loc >> 4) == (s_loc >> 4)
    eye = row == col
    crow = lax.broadcasted_iota(jnp.int32, (CHUNK, CHUNK), 0)
    ccol = lax.broadcasted_iota(jnp.int32, (CHUNK, CHUNK), 1)
    tri = jnp.where((crow - ccol) * sign >= 0, 1.0, 0.0).astype(F32)
    lane = lax.broadcasted_iota(jnp.int32, (CHUNK, LANES), 1)
    head0 = lane < HEAD_DIM

    def bd(x):
        return jnp.concatenate([jnp.where(head0, x, 0.0), jnp.where(head0, 0.0, x)], axis=0)

    for p in range(n_slabs):
        sl = slice(p * LANES, (p + 1) * LANES)
        lw = lw_ref[0, 0, :, sl]
        cum = _mm(tri, lw, HI)
        tot = jnp.sum(lw, axis=0, keepdims=True)
        e_in = jnp.exp(cum)
        e_ex = jnp.exp(cum - lw)
        e_neg = jnp.exp(-cum)
        e_rem = jnp.exp(tot - cum)
        gam = jnp.exp(tot)
        kk = kk_ref[0, :, sl]
        kd = kd_ref[0, 0, :, sl]
        be = be_ref[0, 0, :, sl]
        a_t = bd(kk * e_ex)
        r_t = bd(r_ref[0, :, sl] * e_in)
        b_t = bd(be * e_neg)
        k_t = bd(kd * e_neg)
        b_h = bd(be * e_rem)
        k_h = bd(kd * e_rem)
        vv = bd(v_ref[0, :, sl])

        sc = _mm_nt(jnp.concatenate([a_t, r_t], axis=0), jnp.concatenate([b_t, k_t], axis=0), HI)
        lb = jnp.where(strict, sc[:n2, :n2], 0.0)
        lk = jnp.where(strict, sc[:n2, n2:], 0.0)
        mb = jnp.where(incl, sc[n2:, :n2], 0.0)
        mk = jnp.where(incl, sc[n2:, n2:], 0.0)

        nn = -lb
        dg = jnp.where(blk16, nn, 0.0)
        off = nn - dg
        d2 = _mm(dg, dg, HI)
        d4 = _mm(d2, d2, HI)
        d8 = _mm(d4, d4, HI)
        ident = jnp.where(eye, 1.0, 0.0).astype(F32)
        td = ident + dg
        td = td + _mm(td, d2, HI)
        td = td + _mm(td, d4, HI)
        td = td + _mm(td, d8, HI)
        e1 = _mm(td, off, HI)
        e2 = _mm(e1, e1, HI)
        e3 = _mm(e1, e2, HI)
        tt = _mm(ident + e1 + e2 + e3, td, HI)

        lkv = _mm(lk, vv, HI)
        w = _mm(tt, jnp.concatenate([a_t, lkv], axis=1), HI)
        mw = _mm(mb, w, HI)
        mkv = _mm(mk, vv, HI)
        r_hat = r_t - mw[:, :n2]
        y_loc = mkv - mw[:, n2:]
        bw = _mm(b_h.T, w, HI)
        kv = _mm(k_h.T, vv, HI)
        g = jnp.where(eye, jnp.broadcast_to(gam, (n2, n2)), 0.0) - bw[:, :n2]
        h = kv - bw[:, n2:]

        z = z_ref[p]
        y = _mm(r_hat, z, HI) + y_loc
        z_ref[p] = _mm(g, z, HI) + h
        y_ref[0, 0, :, sl] = y[:CHUNK] + y[CHUNK:]


def _rwkv_scan(r, v, kk, kd, be, lw):
    b, l, dr = r.shape
    nc = l // CHUNK
    n_slabs = dr // LANES

    def cidx(d, c):
        return c + d * (nc - 1 - 2 * c)

    shared = pl.BlockSpec((1, CHUNK, dr), lambda bi, d, c: (bi, cidx(d, c), 0))
    perdir = pl.BlockSpec((1, 1, CHUNK, dr), lambda bi, d, c: (d, bi, cidx(d, c), 0))
    return pl.pallas_call(
        functools.partial(_rwkv_kernel, n_slabs=n_slabs),
        grid=(b, 2, nc),
        in_specs=[shared, shared, shared, perdir, perdir, perdir],
        out_specs=perdir,
        out_shape=jax.ShapeDtypeStruct((2, b, l, dr), F32),
        scratch_shapes=[pltpu.VMEM((n_slabs, HEADS_PER_SLAB * CHUNK, HEADS_PER_SLAB * HEAD_DIM), F32)],
        compiler_params=pltpu.CompilerParams(
            dimension_semantics=("parallel", "parallel", "arbitrary"),
            vmem_limit_bytes=VMEM_LIMIT),
        name="rwkv_scan",
    )(r, v, kk, kd, be, lw)


def _rwkv_branch(p_rwkv, mu_shift, w0, w2, a0, a2, g2, k_k, k_a, r_k, lnx_g, lnx_b):
    b, l, _ = p_rwkv.shape
    zero = jnp.zeros_like(p_rwkv[:, :1])
    prev = jnp.concatenate([zero, p_rwkv[:, :-1]], axis=1)
    nxt = jnp.concatenate([p_rwkv[:, 1:], zero], axis=1)
    p = p_rwkv + mu_shift * (0.5 * (prev + nxt) - p_rwkv)
    xr, xk, xv, dw_f, dw_b, da_f, da_b, dg = jnp.split(p, np.cumsum(RWKV_SIZES)[:-1].tolist(), axis=-1)
    n_h = D_RWKV // HEAD_DIM

    def heads(t):
        return t.reshape(b, l, n_h, HEAD_DIM)

    kk = heads(xk * k_k)
    kk = kk / jnp.maximum(jnp.sqrt(jnp.sum(kk * kk, -1, keepdims=True)), 1e-12)
    kk = kk.reshape(b, l, D_RWKV)
    kds, bes, lws = [], [], []
    for d, (dw, da) in enumerate(((dw_f, da_f), (dw_b, da_b))):
        w_log = -jax.nn.softplus(-(w0[d] + jnp.dot(jnp.tanh(dw), w2[d], precision=HI))) - 0.5
        a = jax.nn.sigmoid(a0[d] + jnp.dot(da, a2[d], precision=HI))
        lws.append(-jnp.exp(w_log))
        kds.append(xk * (1.0 + (a - 1.0) * k_a))
        bes.append(a * kk)
    kd = jnp.stack(kds)
    be = jnp.stack(bes)
    lw = jnp.stack(lws)
    wkv2 = _rwkv_scan(xr, xv, kk, kd, be, lw)
    wkv = heads(wkv2[0] + wkv2[1])
    rk = heads(xr) * r_k
    bonus = (jnp.sum(rk * heads(kd[0] + kd[1]), -1, keepdims=True) * heads(xv)).reshape(b, l, D_RWKV)
    mu = jnp.mean(wkv, -1, keepdims=True)
    var = jnp.mean(jnp.square(wkv - mu), -1, keepdims=True)
    wkv = ((wkv - mu) * lax.rsqrt(var + GN_EPS)).reshape(b, l, D_RWKV) * lnx_g + lnx_b
    gate = jnp.dot(jax.nn.sigmoid(dg), g2, precision=HI)
    return (wkv + bonus) * gate


def _na_bias_table(rpb):
    n_h = rpb.shape[0]
    var = np.arange(NA_WIN_H)[:, None, None, None]
    c = np.arange(GRID_W)[None, :, None, None]
    i = np.arange(NA_WIN_H)[None, None, :, None]
    kc = np.arange(GRID_W)[None, None, None, :]
    cs = np.clip(c - NA_WIN_W // 2, 0, GRID_W - NA_WIN_W)
    valid = (kc >= cs) & (kc < cs + NA_WIN_W)
    row_off = np.broadcast_to(i - var + (NA_WIN_H - 1), (NA_WIN_H, GRID_W, NA_WIN_H, GRID_W))
    col_off = np.broadcast_to(np.clip(kc - c + (NA_WIN_W - 1), 0, 2 * NA_WIN_W - 2),
                              (NA_WIN_H, GRID_W, NA_WIN_H, GRID_W))
    valid = np.broadcast_to(valid, row_off.shape)
    tab = rpb.astype(F32)[:, row_off, col_off]
    tab = jnp.where(valid[None], tab, NEG_BIG)
    tab = jnp.transpose(tab, (1, 0, 2, 3, 4))
    return tab.reshape(NA_WIN_H, n_h * GRID_W, NA_WIN_H * GRID_W)


def _na_kernel(q_ref, k_ref, v_ref, bias_ref, o_ref, *, rows, n_slabs):
    r = pl.program_id(1)
    rs = jnp.clip(r - NA_WIN_H // 2, 0, rows - NA_WIN_H)
    start = pl.multiple_of(rs * GRID_W, GRID_W)
    band = NA_WIN_H * GRID_W
    scale = HEAD_DIM ** -0.5
    lane = lax.broadcasted_iota(jnp.int32, (GRID_W, LANES), 1)
    head0 = lane < HEAD_DIM
    for p in range(n_slabs):
        sl = slice(p * LANES, (p + 1) * LANES)
        q2 = q_ref[0, :, sl]
        zero = jnp.zeros_like(q2)
        lhs = jnp.concatenate([jnp.where(head0, q2, zero), jnp.where(head0, zero, q2)], axis=0)
        kb = k_ref[0, pl.ds(start, band), sl]
        vb = v_ref[0, pl.ds(start, band), sl]
        s = _mm_nt(lhs, kb) * scale + bias_ref[0, p * 2 * GRID_W:(p + 1) * 2 * GRID_W, :]
        m = jnp.max(s, axis=-1, keepdims=True)
        e = jnp.exp(s - m)
        den = jnp.sum(e, axis=-1, keepdims=True)
        o = _mm(e.astype(BF16), vb) / den
        o_ref[0, :, sl] = jnp.where(head0, o[:GRID_W], o[GRID_W:]).astype(o_ref.dtype)


def _na_attention(q, k, v, bias_tab):
    b, l, dn = q.shape
    rows = l // GRID_W
    assert rows >= NA_WIN_H
    n_slabs = dn // LANES

    def bias_idx(bi, r):
        rs = jnp.clip(r - NA_WIN_H // 2, 0, rows - NA_WIN_H)
        return (r - rs, 0, 0)

    return pl.pallas_call(
        functools.partial(_na_kernel, rows=rows, n_slabs=n_slabs),
        grid=(b, rows),
        in_specs=[pl.BlockSpec((1, GRID_W, dn), lambda bi, r: (bi, r, 0)),
                  pl.BlockSpec((1, l, dn), lambda bi, r: (bi, 0, 0)),
                  pl.BlockSpec((1, l, dn), lambda bi, r: (bi, 0, 0)),
                  pl.BlockSpec((1,) + bias_tab.shape[1:], bias_idx)],
        out_specs=pl.BlockSpec((1, GRID_W, dn), lambda bi, r: (bi, r, 0)),
        out_shape=jax.ShapeDtypeStruct((b, l, dn), BF16),
        compiler_params=pltpu.CompilerParams(dimension_semantics=("parallel", "arbitrary"),
                                             vmem_limit_bytes=VMEM_LIMIT),
        name="na_attention",
    )(q, k, v, bias_tab)


def _merge_kernel(ya_ref, yn_ref, ga_ref, gn_ref, xn_ref, wa_ref, wn_ref, wo_ref, g1_ref, b1_ref,
                  wr_ref, x1_ref, sc_ref):
    up_a = _mm(ya_ref[...].astype(BF16), wa_ref[...])
    up_n = _mm(yn_ref[...], wn_ref[...])
    merged = (jax.nn.sigmoid(ga_ref[...].astype(F32)) * up_a
              + jax.nn.sigmoid(gn_ref[...].astype(F32)) * up_n)
    mix = _mm(merged.astype(BF16), wo_ref[...])
    x1 = _ln(ALPHA * xn_ref[...] + mix, g1_ref[...], b1_ref[...])
    x1_ref[...] = x1
    sc_ref[...] = jax.nn.sigmoid(_mm(x1, wr_ref[...], HI))


def _merge(ya, yn, ga, gn, xn, w_up_a, w_up_n, w_out, ln1_g, ln1_b, w_router, tm):
    t = xn.shape[0]
    row = lambda n: pl.BlockSpec((tm, n), lambda i: (i, 0))
    full = lambda a: pl.BlockSpec(a.shape, lambda i: (0, 0))
    return pl.pallas_call(
        _merge_kernel,
        grid=(t // tm,),
        in_specs=[row(D_RWKV), row(D_NA), row(D_MODEL), row(D_MODEL), row(D_MODEL),
                  full(w_up_a), full(w_up_n), full(w_out), full(ln1_g), full(ln1_b), full(w_router)],
        out_specs=[row(D_MODEL), row(N_EXPERTS)],
        out_shape=[jax.ShapeDtypeStruct((t, D_MODEL), F32),
                   jax.ShapeDtypeStruct((t, N_EXPERTS), F32)],
        compiler_params=pltpu.CompilerParams(dimension_semantics=("parallel",),
                                             vmem_limit_bytes=VMEM_LIMIT),
        name="merge_ln1_router",
    )(ya, yn, ga, gn, xn, w_up_a, w_up_n, w_out, ln1_g, ln1_b, w_router)


def _route(scores, e_bias):
    t = scores.shape[0]
    sel = scores + e_bias.astype(F32)
    grp = sel.reshape(t, N_GROUPS, N_EXPERTS // N_GROUPS)
    grp_score = jnp.sum(lax.top_k(grp, 2)[0], -1)
    _, grp_idx = lax.top_k(grp_score, TOPK_GROUPS)
    grp_mask = jnp.any(grp_idx[..., None] == jnp.arange(N_GROUPS), axis=-2)
    sel = jnp.where(jnp.repeat(grp_mask, N_EXPERTS // N_GROUPS, axis=-1), sel, -jnp.inf)
    _, idx = lax.top_k(sel, TOP_K)
    wts = jnp.take_along_axis(scores, idx, axis=-1)
    wts = wts / jnp.sum(wts, -1, keepdims=True) * ROUTED_SCALE
    return idx, wts


def _dispatch_plan(idx, wts):
    t = idx.shape[0]
    n_assign = t * TOP_K
    n_blocks = (n_assign + N_EXPERTS * (MOE_ROWS - 1) + MOE_ROWS - 1) // MOE_ROWS
    cap = n_blocks * MOE_ROWS
    flat_e = idx.reshape(-1)
    order = jnp.argsort(flat_e).astype(jnp.int32)
    e_sorted = flat_e[order]
    counts = jnp.bincount(flat_e, length=N_EXPERTS).astype(jnp.int32)
    padded = (counts + MOE_ROWS - 1) // MOE_ROWS * MOE_ROWS
    start = jnp.cumsum(counts) - counts
    pend = jnp.cumsum(padded)
    pstart = pend - padded
    rank = jnp.arange(n_assign, dtype=jnp.int32) - start[e_sorted]
    dest = (pstart[e_sorted] + rank).astype(jnp.int32)
    tok = order // TOP_K
    slot = (order % TOP_K) * t + tok
    rows = jnp.arange(cap, dtype=jnp.int32)
    dump_slot = n_assign + (rows // MOE_ROWS % 2) * MOE_ROWS + rows % MOE_ROWS
    row_tok = jnp.zeros((cap,), jnp.int32).at[dest].set(tok)
    row_slot = dump_slot.at[dest].set(slot)
    row_w = jnp.zeros((cap,), F32).at[dest].set(wts.reshape(-1)[order])
    block_start = jnp.arange(n_blocks, dtype=jnp.int32) * MOE_ROWS
    block_e = jnp.minimum(jnp.searchsorted(pend, block_start, side='right'),
                          N_EXPERTS - 1).astype(jnp.int32)
    n_used = (pend[-1] // MOE_ROWS).astype(jnp.int32).reshape(1)
    return (row_tok.reshape(n_blocks, MOE_ROWS), row_slot.reshape(n_blocks, MOE_ROWS),
            row_w.reshape(cap, 1), block_e, n_used)


def _experts_kernel(be_ref, nu_ref, tok_hbm, slot_hbm, w_ref, x_hbm, wg_ref, wu_ref, wd_ref,
                    out_hbm, tok_s, slot_s, xbuf, obuf, isem, gsem, ssem, *, n_real_rows):
    b = pl.program_id(0)
    n_used = nu_ref[0]
    cur = b % 2
    nxt = 1 - cur

    def idx_copies(blk, buf):
        return (pltpu.make_async_copy(tok_hbm.at[blk], tok_s.at[buf], isem.at[0, buf]),
                pltpu.make_async_copy(slot_hbm.at[blk], slot_s.at[buf], isem.at[1, buf]))

    def start_gather(buf):
        def body(i, carry):
            tok = tok_s[buf, i]
            pltpu.make_async_copy(x_hbm.at[pl.ds(tok, 1)], xbuf.at[buf, pl.ds(i, 1)], gsem.at[buf]).start()
            return carry
        lax.fori_loop(0, MOE_ROWS, body, 0)

    def wait_gather(buf):
        pltpu.make_async_copy(x_hbm.at[pl.ds(0, MOE_ROWS)], xbuf.at[buf], gsem.at[buf]).wait()

    def start_scatter(buf):
        def body(i, carry):
            slot = slot_s[buf, i]
            pltpu.make_async_copy(obuf.at[buf, pl.ds(i, 1)], out_hbm.at[pl.ds(slot, 1)], ssem.at[buf]).start()
            return carry
        lax.fori_loop(0, MOE_ROWS, body, 0)

    def wait_scatter(buf):
        pltpu.make_async_copy(obuf.at[buf], out_hbm.at[pl.ds(0, MOE_ROWS)], ssem.at[buf]).wait()

    def dump_init(buf):
        return pltpu.make_async_copy(obuf.at[buf], out_hbm.at[pl.ds(n_real_rows + buf * MOE_ROWS, MOE_ROWS)],
                                     ssem.at[buf])

    @pl.when(b == 0)
    def _():
        obuf[...] = jnp.zeros_like(obuf)
        for buf in range(2):
            dump_init(buf).start()
        for buf in range(2):
            dump_init(buf).wait()
        for cp in idx_copies(0, 0):
            cp.start()
        for cp in idx_copies(0, 0):
            cp.wait()
        start_gather(0)

        @pl.when(1 < n_used)
        def _():
            for cp in idx_copies(1, 1):
                cp.start()

    @pl.when(b + 1 < n_used)
    def _():
        for cp in idx_copies(b + 1, nxt):
            cp.wait()
        start_gather(nxt)

    @pl.when(b < n_used)
    def _():
        wait_gather(cur)
        xb = xbuf[cur].astype(BF16)
        hg = _mm(xb, wg_ref[0])
        hu = _mm(xb, wu_ref[0])
        h = (hg * jax.nn.sigmoid(hg) * hu).astype(BF16)
        o = _mm(h, wd_ref[0]) * w_ref[...]

        @pl.when(b >= 2)
        def _():
            wait_scatter(cur)

        obuf[cur] = o
        start_scatter(cur)

        @pl.when(b + 2 < n_used)
        def _():
            for cp in idx_copies(b + 2, cur):
                cp.start()

    @pl.when(b == n_used - 1)
    def _():
        wait_scatter(cur)

        @pl.when(b >= 1)
        def _():
            wait_scatter(nxt)


def _experts(x1, row_tok, row_slot, row_w, block_e, n_used, w_gate_e, w_up_e, w_down_e):
    n_blocks = row_tok.shape[0]
    n_real_rows = x1.shape[0] * TOP_K
    any_spec = pl.BlockSpec(memory_space=pl.ANY)
    wspec = lambda shp: pl.BlockSpec((1,) + shp, lambda b, be, nu: (be[b], 0, 0))
    grid_spec = pltpu.PrefetchScalarGridSpec(
        num_scalar_prefetch=2,
        grid=(n_blocks,),
        in_specs=[any_spec, any_spec,
                  pl.BlockSpec((MOE_ROWS, 1), lambda b, be, nu: (b, 0)),
                  any_spec,
                  wspec((D_MODEL, D_EXPERT)), wspec((D_MODEL, D_EXPERT)), wspec((D_EXPERT, D_MODEL))],
        out_specs=any_spec,
        scratch_shapes=[pltpu.SMEM((2, MOE_ROWS), jnp.int32),
                        pltpu.SMEM((2, MOE_ROWS), jnp.int32),
                        pltpu.VMEM((2, MOE_ROWS, D_MODEL), F32),
                        pltpu.VMEM((2, MOE_ROWS, D_MODEL), F32),
                        pltpu.SemaphoreType.DMA((2, 2)),
                        pltpu.SemaphoreType.DMA((2,)),
                        pltpu.SemaphoreType.DMA((2,))])
    return pl.pallas_call(
        functools.partial(_experts_kernel, n_real_rows=n_real_rows),
        grid_spec=grid_spec,
        out_shape=jax.ShapeDtypeStruct((n_real_rows + 2 * MOE_ROWS, D_MODEL), F32),
        compiler_params=pltpu.CompilerParams(dimension_semantics=("arbitrary",),
                                             vmem_limit_bytes=VMEM_LIMIT),
        name="moe_experts",
    )(block_e, n_used, row_tok, row_slot, row_w, x1, w_gate_e, w_up_e, w_down_e)


def _final_kernel(x1_ref, *refs):
    routed = refs[:TOP_K]
    wg_ref, wu_ref, wd_ref, g2_ref, b2_ref, o_ref = refs[TOP_K:]
    x1 = x1_ref[...]
    xb = x1.astype(BF16)
    hg = _mm(xb, wg_ref[...])
    hu = _mm(xb, wu_ref[...])
    y = _mm((hg * jax.nn.sigmoid(hg) * hu).astype(BF16), wd_ref[...])
    for r_ref in routed:
        y = y + r_ref[...]
    o_ref[...] = _ln(ALPHA * x1 + y, g2_ref[...], b2_ref[...])


def _final(x1, routed, w_gate_s, w_up_s, w_down_s, ln2_g, ln2_b, tm):
    t = x1.shape[0]
    nt = t // tm
    row = pl.BlockSpec((tm, D_MODEL), lambda i: (i, 0))
    full = lambda a: pl.BlockSpec(a.shape, lambda i: (0, 0))
    routed_specs = [pl.BlockSpec((tm, D_MODEL), functools.partial(lambda i, j: (j * nt + i, 0), j=j))
                    for j in range(TOP_K)]
    return pl.pallas_call(
        _final_kernel,
        grid=(nt,),
        in_specs=[row] + routed_specs + [full(w_gate_s), full(w_up_s), full(w_down_s),
                                         full(ln2_g), full(ln2_b)],
        out_specs=row,
        out_shape=jax.ShapeDtypeStruct((t, D_MODEL), F32),
        compiler_params=pltpu.CompilerParams(dimension_semantics=("parallel",),
                                             vmem_limit_bytes=VMEM_LIMIT),
        name="shared_combine_ln2",
    )(x1, *([routed] * TOP_K), w_gate_s, w_up_s, w_down_s, ln2_g, ln2_b)


def _trunk(x, prm):
    b, l, d = x.shape
    t = b * l
    tm = 512
    row2 = lambda a: a.reshape(1, -1).astype(F32)
    xn, p_rwkv, q, k, v, ga, gn = _ln_proj(x.reshape(t, d), row2(prm['ln_in_g']), row2(prm['ln_in_b']),
                                           prm['w_in_rwkv'], prm['w_in_na'], prm['w_in_gate'], tm)
    y_a = _rwkv_branch(p_rwkv.reshape(b, l, RWKV_IN), prm['mu_shift'], prm['w0'], prm['w2'], prm['a0'],
                       prm['a2'], prm['g2'], prm['k_k'], prm['k_a'], prm['r_k'], prm['lnx_g'], prm['lnx_b'])
    y_n = _na_attention(q.reshape(b, l, D_NA), k.reshape(b, l, D_NA), v.reshape(b, l, D_NA),
                        prm['na_bias'])
    x1, scores = _merge(y_a.reshape(t, D_RWKV), y_n.reshape(t, D_NA), ga, gn, xn,
                        prm['w_up_a'], prm['w_up_n'], prm['w_out'],
                        row2(prm['ln1_g']), row2(prm['ln1_b']), prm['w_router'], tm)
    idx, wts = _route(scores, prm['e_bias'])
    routed = _experts(x1, *_dispatch_plan(idx, wts), prm['w_gate_e'], prm['w_up_e'], prm['w_down_e'])
    out = _final(x1, routed, prm['w_gate_s'], prm['w_up_s'], prm['w_down_s'],
                 row2(prm['ln2_g']), row2(prm['ln2_b']), tm)
    return out.reshape(b, l, d)


def kernel(x_prompt, x_sample, ln_in_g, ln_in_b, w_in, mu_shift, w0, w2, a0, a2, g2, k_k, k_a, r_k,
           lnx_g, lnx_b, rpb, w_up_a, w_up_n, w_out, ln1_g, ln1_b, w_router, e_bias,
           w_gate_e, w_up_e, w_down_e, w_gate_s, w_up_s, w_down_s, ln2_g, ln2_b):
    assert w_in.shape[0] == DEPTH == 1
    w_in0 = w_in[0].astype(BF16)
    prm = dict(
        ln_in_g=ln_in_g, ln_in_b=ln_in_b,
        w_in_rwkv=w_in0[:, :RWKV_IN], w_in_na=w_in0[:, RWKV_IN:RWKV_IN + NA_IN],
        w_in_gate=w_in0[:, RWKV_IN + NA_IN:],
        mu_shift=mu_shift[0], w0=w0[0], w2=w2[0], a0=a0[0], a2=a2[0], g2=g2[0], k_k=k_k[0], k_a=k_a[0],
        r_k=r_k[0], lnx_g=lnx_g[0], lnx_b=lnx_b[0], na_bias=_na_bias_table(rpb[0]),
        w_up_a=w_up_a[0].astype(BF16), w_up_n=w_up_n[0].astype(BF16), w_out=w_out[0].astype(BF16),
        ln1_g=ln1_g[0], ln1_b=ln1_b[0], w_router=w_router[0], e_bias=e_bias[0],
        w_gate_e=w_gate_e[0].astype(BF16), w_up_e=w_up_e[0].astype(BF16),
        w_down_e=w_down_e[0].astype(BF16),
        w_gate_s=w_gate_s[0].astype(BF16), w_up_s=w_up_s[0].astype(BF16),
        w_down_s=w_down_s[0].astype(BF16), ln2_g=ln2_g[0], ln2_b=ln2_b[0])
    return (_trunk(x_prompt, prm), _trunk(x_sample, prm))
```

```python
import functools

import jax
import jax.numpy as jnp
import numpy as np
from jax import lax
from jax.experimental import pallas as pl
from jax.experimental.pallas import tpu as pltpu

F32 = jnp.float32
BF16 = jnp.bfloat16

D_MODEL = 1024
GRID_W = 64
HEAD_DIM = 64
D_RWKV = 512
D_NA = 512
NA_WIN_H = 8
NA_WIN_W = 16
DECAY_LORA = 64
ICL_LORA = 64
GATE_LORA = 128
N_EXPERTS = 256
TOP_K = 8
N_GROUPS = 8
TOPK_GROUPS = 4
D_EXPERT = 256
ROUTED_SCALE = 2.5
LN_EPS = 1e-5
GN_EPS = 64e-5
DEPTH = 1
ALPHA = (2 * DEPTH) ** 0.25
RWKV_SIZES = (D_RWKV, D_RWKV, D_RWKV, DECAY_LORA, DECAY_LORA, ICL_LORA, ICL_LORA, GATE_LORA)
RWKV_IN = sum(RWKV_SIZES)
NA_IN = 3 * D_NA

LANES = 128
HEADS_PER_SLAB = LANES // HEAD_DIM
SLABS = D_MODEL // LANES
VMEM_LIMIT = 48 * 1024 * 1024

CHUNK = 64
MOE_ROWS = 128
MOE_TILE = 256
NEG_BIG = -1e30

HI = lax.Precision.HIGHEST


def _mm(a, b, precision=None):
    return jnp.dot(a, b, preferred_element_type=F32, precision=precision)


def _mm_nt(a, b, precision=None):
    return lax.dot_general(a, b, (((1,), (1,)), ((), ())), preferred_element_type=F32,
                           precision=precision)


def _ln(x, g, b):
    mu = jnp.mean(x, -1, keepdims=True)
    xc = x - mu
    var = jnp.mean(xc * xc, -1, keepdims=True)
    return xc * lax.rsqrt(var + LN_EPS) * g + b


def _ln_proj_kernel(x_ref, g_ref, b_ref, wr_ref, wn_ref, wg_ref,
                    xn_ref, pr_ref, q_ref, k_ref, v_ref, ga_ref, gn_ref):
    xn = _ln(x_ref[...], g_ref[...], b_ref[...])
    xn_ref[...] = xn
    xb = xn.astype(BF16)
    pr_ref[...] = _mm(xb, wr_ref[...])
    pn = _mm(xb, wn_ref[...])
    q_ref[...] = pn[:, :D_NA].astype(BF16)
    k_ref[...] = pn[:, D_NA:2 * D_NA].astype(BF16)
    v_ref[...] = pn[:, 2 * D_NA:].astype(BF16)
    pg = _mm(xb, wg_ref[...])
    ga_ref[...] = pg[:, :D_MODEL].astype(BF16)
    gn_ref[...] = pg[:, D_MODEL:].astype(BF16)


def _ln_proj(x, g, b, w_rwkv, w_na, w_gate, tm):
    t = x.shape[0]
    row = lambda n: pl.BlockSpec((tm, n), lambda i: (i, 0))
    full = lambda a: pl.BlockSpec(a.shape, lambda i: (0, 0))
    return pl.pallas_call(
        _ln_proj_kernel,
        grid=(t // tm,),
        in_specs=[row(D_MODEL), full(g), full(b), full(w_rwkv), full(w_na), full(w_gate)],
        out_specs=[row(D_MODEL), row(RWKV_IN), row(D_NA), row(D_NA), row(D_NA),
                   row(D_MODEL), row(D_MODEL)],
        out_shape=[jax.ShapeDtypeStruct((t, D_MODEL), F32),
                   jax.ShapeDtypeStruct((t, RWKV_IN), F32),
                   jax.ShapeDtypeStruct((t, D_NA), BF16),
                   jax.ShapeDtypeStruct((t, D_NA), BF16),
                   jax.ShapeDtypeStruct((t, D_NA), BF16),
                   jax.ShapeDtypeStruct((t, D_MODEL), BF16),
                   jax.ShapeDtypeStruct((t, D_MODEL), BF16)],
        compiler_params=pltpu.CompilerParams(dimension_semantics=("parallel",),
                                             vmem_limit_bytes=VMEM_LIMIT),
        name="ln_proj",
    )(x, g, b, w_rwkv, w_na, w_gate)


def _rwkv_kernel(rf_ref, vf_ref, kkf_ref, kdf_ref, bef_ref, lwf_ref,
                 rb_ref, vb_ref, kkb_ref, kdb_ref, beb_ref, lwb_ref, yf_ref, yb_ref, z_ref, *, n_slabs):
    @pl.when(pl.program_id(1) == 0)
    def _():
        z_ref[...] = jnp.zeros_like(z_ref)

    n2 = HEADS_PER_SLAB * CHUNK
    row = lax.broadcasted_iota(jnp.int32, (n2, n2), 0)
    col = lax.broadcasted_iota(jnp.int32, (n2, n2), 1)
    dts = (row & (CHUNK - 1)) - (col & (CHUNK - 1))
    blk16 = ((row & (CHUNK - 1)) >> 4) == ((col & (CHUNK - 1)) >> 4)
    eye = row == col
    crow = lax.broadcasted_iota(jnp.int32, (CHUNK, CHUNK), 0)
    ccol = lax.broadcasted_iota(jnp.int32, (CHUNK, CHUNK), 1)
    lane = lax.broadcasted_iota(jnp.int32, (CHUNK, LANES), 1)
    head0 = lane < HEAD_DIM

    def bd(x):
        return jnp.concatenate([jnp.where(head0, x, 0.0), jnp.where(head0, 0.0, x)], axis=0)

    def bf(x):
        return x.astype(BF16)

    chains = [(d, p) for d in range(2) for p in range(n_slabs)]
    sls = [slice(p * LANES, (p + 1) * LANES) for _, p in chains]
    strict = [dts > 0 if d == 0 else dts < 0 for d, _ in chains]
    incl = [dts >= 0 if d == 0 else dts <= 0 for d, _ in chains]
    tri_d = [jnp.where(crow >= ccol, 1.0, 0.0).astype(F32), jnp.where(crow <= ccol, 1.0, 0.0).astype(F32)]
    tri = [tri_d[d] for d, _ in chains]
    r_in = [(rf_ref, rb_ref)[d] for d, _ in chains]
    v_in = [(vf_ref, vb_ref)[d] for d, _ in chains]
    kk_in = [(kkf_ref, kkb_ref)[d] for d, _ in chains]
    kd_in = [(kdf_ref, kdb_ref)[d] for d, _ in chains]
    be_in = [(bef_ref, beb_ref)[d] for d, _ in chains]
    lw_in = [(lwf_ref, lwb_ref)[d] for d, _ in chains]
    ident = jnp.where(eye, 1.0, 0.0).astype(F32)

    def each(f, *cols):
        return [f(*args) for args in zip(*cols)]

    lw = each(lambda ref, sl: ref[0, 0, :, sl], lw_in, sls)
    cum = each(lambda t_, x: _mm(t_, x, HI), tri, lw)
    tot = each(lambda x: jnp.sum(x, axis=0, keepdims=True), lw)
    e_in = each(jnp.exp, cum)
    e_ex = each(lambda c_, l_: jnp.exp(c_ - l_), cum, lw)
    e_neg = each(lambda c_: jnp.exp(-c_), cum)
    e_rem = each(lambda t_, c_: jnp.exp(t_ - c_), tot, cum)
    gam = each(jnp.exp, tot)
    kk = each(lambda ref, sl: ref[0, :, sl], kk_in, sls)
    kd = each(lambda ref, sl: ref[0, 0, :, sl], kd_in, sls)
    be = each(lambda ref, sl: ref[0, 0, :, sl], be_in, sls)
    a_b = each(lambda x, e: bf(bd(x * e)), kk, e_ex)
    r_t = each(lambda ref, e, sl: bd(ref[0, :, sl] * e), r_in, e_in, sls)
    b_t = each(lambda x, e: bd(x * e), be, e_neg)
    k_t = each(lambda x, e: bd(x * e), kd, e_neg)
    b_h = each(lambda x, e: bd(x * e), be, e_rem)
    k_h = each(lambda x, e: bd(x * e), kd, e_rem)
    vv = each(lambda ref, sl: bf(bd(ref[0, :, sl])), v_in, sls)

    sc = each(lambda a, r, b, k: _mm_nt(jnp.concatenate([a, bf(r)], axis=0),
                                        bf(jnp.concatenate([b, k], axis=0))), a_b, r_t, b_t, k_t)
    lk = each(lambda m, s: bf(jnp.where(m, s[:n2, n2:], 0.0)), strict, sc)
    mb = each(lambda m, s: bf(jnp.where(m, s[n2:, :n2], 0.0)), incl, sc)
    mk = each(lambda m, s: bf(jnp.where(m, s[n2:, n2:], 0.0)), incl, sc)

    nn = each(lambda m, s: jnp.where(m, -s[:n2, :n2], 0.0), strict, sc)
    dg = each(lambda x: jnp.where(blk16, x, 0.0), nn)
    offb = each(lambda x, d_: bf(x - d_), nn, dg)
    d1b = each(bf, dg)
    d2b = each(lambda x: bf(_mm(x, x)), d1b)
    d4b = each(lambda x: bf(_mm(x, x)), d2b)
    d8b = each(lambda x: bf(_mm(x, x)), d4b)
    td = each(lambda d_: ident + d_, dg)
    td = each(lambda t_, d_: t_ + _mm(bf(t_), d_), td, d2b)
    td = each(lambda t_, d_: t_ + _mm(bf(t_), d_), td, d4b)
    td = each(lambda t_, d_: t_ + _mm(bf(t_), d_), td, d8b)
    tdb = each(bf, td)
    e1 = each(_mm, tdb, offb)
    e1b = each(bf, e1)
    e2 = each(lambda x: _mm(x, x), e1b)
    e3 = each(lambda x, y_: _mm(x, bf(y_)), e1b, e2)
    ttb = each(lambda x1_, x2_, x3_, t_: bf(_mm(bf(ident + x1_ + x2_ + x3_), t_)), e1, e2, e3, tdb)

    lkv = each(_mm, lk, vv)
    wb = each(lambda t_, a, l_: bf(_mm(t_, jnp.concatenate([a, bf(l_)], axis=1))), ttb, a_b, lkv)
    mw = each(_mm, mb, wb)
    mkv = each(_mm, mk, vv)
    bw = each(lambda b, w: _mm(bf(b.T), w), b_h, wb)
    kv = each(lambda k, v_: _mm(bf(k.T), v_), k_h, vv)
    r_hat = each(lambda r, m: bf(r - m[:, :n2]), r_t, mw)
    y_loc = each(lambda m, w: m - w[:, n2:], mkv, mw)
    g = each(lambda g_, b: bf(jnp.where(eye, jnp.broadcast_to(g_, (n2, n2)), 0.0) - b[:, :n2]), gam, bw)
    h = each(lambda k, b: k - b[:, n2:], kv, bw)

    n_ch = len(chains)
    z = [z_ref[i] for i in range(n_ch)]
    z_hi = each(bf, z)
    z_lo = each(lambda z_, zh: bf(z_ - zh.astype(F32)), z, z_hi)
    y = each(lambda r, zh, yl: _mm(r, zh) + yl, r_hat, z_hi, y_loc)
    z_new = each(lambda g_, zh, zl, h_: _mm(g_, zh) + _mm(g_, zl) + h_, g, z_hi, z_lo, h)
    for i, (d, _) in enumerate(chains):
        z_ref[i] = z_new[i]
        (yf_ref, yb_ref)[d][0, :, sls[i]] = y[i][:CHUNK] + y[i][CHUNK:]


def _rwkv_scan(r, v, kk, kd, be, lw):
    b, l, dr = r.shape
    nc = l // CHUNK
    n_slabs = dr // LANES
    fwd = pl.BlockSpec((1, CHUNK, dr), lambda bi, c: (bi, c, 0))
    bwd = pl.BlockSpec((1, CHUNK, dr), lambda bi, c: (bi, nc - 1 - c, 0))
    fwd_d = pl.BlockSpec((1, 1, CHUNK, dr), lambda bi, c: (0, bi, c, 0))
    bwd_d = pl.BlockSpec((1, 1, CHUNK, dr), lambda bi, c: (1, bi, nc - 1 - c, 0))
    return pl.pallas_call(
        functools.partial(_rwkv_kernel, n_slabs=n_slabs),
        grid=(b, nc),
        in_specs=[fwd, fwd, fwd, fwd_d, fwd_d, fwd_d, bwd, bwd, bwd, bwd_d, bwd_d, bwd_d],
        out_specs=[fwd, bwd],
        out_shape=[jax.ShapeDtypeStruct((b, l, dr), F32), jax.ShapeDtypeStruct((b, l, dr), F32)],
        scratch_shapes=[pltpu.VMEM((2 * n_slabs, HEADS_PER_SLAB * CHUNK, HEADS_PER_SLAB * HEAD_DIM), F32)],
        compiler_params=pltpu.CompilerParams(
            dimension_semantics=("parallel", "arbitrary"),
            vmem_limit_bytes=VMEM_LIMIT),
        name="rwkv_scan",
    )(r, v, kk, kd, be, lw, r, v, kk, kd, be, lw)


def _rwkv_branch(p_rwkv, mu_shift, w0, w2, a0, a2, g2, k_k, k_a, r_k, lnx_g, lnx_b):
    b, l, _ = p_rwkv.shape
    zero = jnp.zeros_like(p_rwkv[:, :1])
    prev = jnp.concatenate([zero, p_rwkv[:, :-1]], axis=1)
    nxt = jnp.concatenate([p_rwkv[:, 1:], zero], axis=1)
    p = p_rwkv + mu_shift * (0.5 * (prev + nxt) - p_rwkv)
    xr, xk, xv, dw_f, dw_b, da_f, da_b, dg = jnp.split(p, np.cumsum(RWKV_SIZES)[:-1].tolist(), axis=-1)
    n_h = D_RWKV // HEAD_DIM

    def heads(t):
        return t.reshape(b, l, n_h, HEAD_DIM)

    kk = heads(xk * k_k)
    kk = kk / jnp.maximum(jnp.sqrt(jnp.sum(kk * kk, -1, keepdims=True)), 1e-12)
    kk = kk.reshape(b, l, D_RWKV)
    kds, bes, lws = [], [], []
    for d, (dw, da) in enumerate(((dw_f, da_f), (dw_b, da_b))):
        w_log = -jax.nn.softplus(-(w0[d] + jnp.dot(jnp.tanh(dw), w2[d], precision=HI))) - 0.5
        a = jax.nn.sigmoid(a0[d] + jnp.dot(da, a2[d], precision=HI))
        lws.append(-jnp.exp(w_log))
        kds.append(xk * (1.0 + (a - 1.0) * k_a))
        bes.append(a * kk)
    kd = jnp.stack(kds)
    be = jnp.stack(bes)
    lw = jnp.stack(lws)
    wkv_f, wkv_b = _rwkv_scan(xr, xv, kk, kd, be, lw)
    wkv = heads(wkv_f + wkv_b)
    rk = heads(xr) * r_k
    bonus = (jnp.sum(rk * heads(kd[0] + kd[1]), -1, keepdims=True) * heads(xv)).reshape(b, l, D_RWKV)
    mu = jnp.mean(wkv, -1, keepdims=True)
    var = jnp.mean(jnp.square(wkv - mu), -1, keepdims=True)
    wkv = ((wkv - mu) * lax.rsqrt(var + GN_EPS)).reshape(b, l, D_RWKV) * lnx_g + lnx_b
    gate = jnp.dot(jax.nn.sigmoid(dg), g2, precision=HI)
    return (wkv + bonus) * gate


def _na_bias_table(rpb):
    n_h = rpb.shape[0]
    var = np.arange(NA_WIN_H)[:, None, None, None]
    c = np.arange(GRID_W)[None, :, None, None]
    i = np.arange(NA_WIN_H)[None, None, :, None]
    kc = np.arange(GRID_W)[None, None, None, :]
    cs = np.clip(c - NA_WIN_W // 2, 0, GRID_W - NA_WIN_W)
    valid = (kc >= cs) & (kc < cs + NA_WIN_W)
    row_off = np.broadcast_to(i - var + (NA_WIN_H - 1), (NA_WIN_H, GRID_W, NA_WIN_H, GRID_W))
    col_off = np.broadcast_to(np.clip(kc - c + (NA_WIN_W - 1), 0, 2 * NA_WIN_W - 2),
                              (NA_WIN_H, GRID_W, NA_WIN_H, GRID_W))
    valid = np.broadcast_to(valid, row_off.shape)
    tab = rpb.astype(F32)[:, row_off, col_off]
    tab = jnp.where(valid[None], tab, NEG_BIG)
    tab = jnp.transpose(tab, (1, 0, 2, 3, 4))
    return tab.reshape(NA_WIN_H, n_h * GRID_W, NA_WIN_H * GRID_W)


def _na_kernel(q_ref, k_ref, v_ref, bias_ref, o_ref, *, rows, n_slabs):
    r = pl.program_id(1)
    rs = jnp.clip(r - NA_WIN_H // 2, 0, rows - NA_WIN_H)
    start = pl.multiple_of(rs * GRID_W, GRID_W)
    band = NA_WIN_H * GRID_W
    scale = HEAD_DIM ** -0.5
    lane = lax.broadcasted_iota(jnp.int32, (GRID_W, LANES), 1)
    head0 = lane < HEAD_DIM
    for p in range(n_slabs):
        sl = slice(p * LANES, (p + 1) * LANES)
        q2 = q_ref[0, :, sl]
        zero = jnp.zeros_like(q2)
        lhs = jnp.concatenate([jnp.where(head0, q2, zero), jnp.where(head0, zero, q2)], axis=0)
        kb = k_ref[0, pl.ds(start, band), sl]
        vb = v_ref[0, pl.ds(start, band), sl]
        s = _mm_nt(lhs, kb) * scale + bias_ref[0, p * 2 * GRID_W:(p + 1) * 2 * GRID_W, :]
        m = jnp.max(s, axis=-1, keepdims=True)
        e = jnp.exp(s - m)
        den = jnp.sum(e, axis=-1, keepdims=True)
        o = _mm(e.astype(BF16), vb) / den
        o_ref[0, :, sl] = jnp.where(head0, o[:GRID_W], o[GRID_W:]).astype(o_ref.dtype)


def _na_attention(q, k, v, bias_tab):
    b, l, dn = q.shape
    rows = l // GRID_W
    assert rows >= NA_WIN_H
    n_slabs = dn // LANES

    def bias_idx(bi, r):
        rs = jnp.clip(r - NA_WIN_H // 2, 0, rows - NA_WIN_H)
        return (r - rs, 0, 0)

    return pl.pallas_call(
        functools.partial(_na_kernel, rows=rows, n_slabs=n_slabs),
        grid=(b, rows),
        in_specs=[pl.BlockSpec((1, GRID_W, dn), lambda bi, r: (bi, r, 0)),
                  pl.BlockSpec((1, l, dn), lambda bi, r: (bi, 0, 0)),
                  pl.BlockSpec((1, l, dn), lambda bi, r: (bi, 0, 0)),
                  pl.BlockSpec((1,) + bias_tab.shape[1:], bias_idx)],
        out_specs=pl.BlockSpec((1, GRID_W, dn), lambda bi, r: (bi, r, 0)),
        out_shape=jax.ShapeDtypeStruct((b, l, dn), BF16),
        compiler_params=pltpu.CompilerParams(dimension_semantics=("parallel", "arbitrary"),
                                             vmem_limit_bytes=VMEM_LIMIT),
        name="na_attention",
    )(q, k, v, bias_tab)


def _to_token_tiles(dst_ref, x):
    n = x.shape[0]
    for s in range(SLABS):
        dst_ref[pl.ds(s, n, stride=SLABS), :] = x[:, s * LANES:(s + 1) * LANES]


def _from_token_tiles(src_ref, n):
    return jnp.concatenate([src_ref[pl.ds(s, n, stride=SLABS), :] for s in range(SLABS)], axis=1)


def _merge_kernel(ya_ref, yn_ref, ga_ref, gn_ref, xn_ref, wa_ref, wn_ref, wo_ref, g1_ref, b1_ref,
                  wrt_ref, x1_ref, x1t_ref, sct_ref):
    up_a = _mm(ya_ref[...].astype(BF16), wa_ref[...])
    up_n = _mm(yn_ref[...], wn_ref[...])
    merged = (jax.nn.sigmoid(ga_ref[...].astype(F32)) * up_a
              + jax.nn.sigmoid(gn_ref[...].astype(F32)) * up_n)
    mix = _mm(merged.astype(BF16), wo_ref[...])
    x1 = _ln(ALPHA * xn_ref[...] + mix, g1_ref[...], b1_ref[...])
    x1_ref[...] = x1
    _to_token_tiles(x1t_ref, x1)
    sct_ref[...] = jax.nn.sigmoid(_mm_nt(wrt_ref[...], x1, HI))


def _merge(ya, yn, ga, gn, xn, w_up_a, w_up_n, w_out, ln1_g, ln1_b, w_router_t, tm):
    t = xn.shape[0]
    row = lambda n: pl.BlockSpec((tm, n), lambda i: (i, 0))
    full = lambda a: pl.BlockSpec(a.shape, lambda i: (0, 0))
    return pl.pallas_call(
        _merge_kernel,
        grid=(t // tm,),
        in_specs=[row(D_RWKV), row(D_NA), row(D_MODEL), row(D_MODEL), row(D_MODEL),
                  full(w_up_a), full(w_up_n), full(w_out), full(ln1_g), full(ln1_b), full(w_router_t)],
        out_specs=[row(D_MODEL), pl.BlockSpec((tm * SLABS, LANES), lambda i: (i, 0)),
                   pl.BlockSpec((N_EXPERTS, tm), lambda i: (0, i))],
        out_shape=[jax.ShapeDtypeStruct((t, D_MODEL), F32),
                   jax.ShapeDtypeStruct((t * SLABS, LANES), F32),
                   jax.ShapeDtypeStruct((N_EXPERTS, t), F32)],
        compiler_params=pltpu.CompilerParams(dimension_semantics=("parallel",),
                                             vmem_limit_bytes=VMEM_LIMIT),
        name="merge_ln1_router",
    )(ya, yn, ga, gn, xn, w_up_a, w_up_n, w_out, ln1_g, ln1_b, w_router_t)


def _first_argmax(vals, iota):
    m = jnp.max(vals, axis=0, keepdims=True)
    first = jnp.min(jnp.where(vals == m, iota, float(vals.shape[0])), axis=0, keepdims=True)
    return m, first


def _route_kernel(sct_ref, bias_ref, idx_ref, wts_ref, cnt_ref):
    s = sct_ref[...]
    tm = s.shape[1]
    sel = s + bias_ref[...]
    gsz = N_EXPERTS // N_GROUPS
    iota_g = lax.broadcasted_iota(jnp.int32, (gsz, tm), 0).astype(F32)
    iota_8 = lax.broadcasted_iota(jnp.int32, (N_GROUPS, tm), 0).astype(F32)
    iota_e = lax.broadcasted_iota(jnp.int32, (N_EXPERTS, tm), 0).astype(F32)

    gs = jnp.zeros((N_GROUPS, tm), F32)
    for g in range(N_GROUPS):
        blk = sel[g * gsz:(g + 1) * gsz, :]
        m1, i1 = _first_argmax(blk, iota_g)
        m2 = jnp.max(jnp.where(iota_g == i1, -jnp.inf, blk), axis=0, keepdims=True)
        gs = jnp.where(iota_8 == float(g), m1 + m2, gs)

    chosen = jnp.zeros((N_GROUPS, tm), F32)
    for _ in range(TOPK_GROUPS):
        _, gi = _first_argmax(gs, iota_8)
        hit = iota_8 == gi
        chosen = jnp.where(hit, 1.0, chosen)
        gs = jnp.where(hit, -jnp.inf, gs)
    mask_e = jnp.concatenate([jnp.broadcast_to(chosen[g:g + 1, :], (gsz, tm)) for g in range(N_GROUPS)],
                             axis=0)
    cand = jnp.where(mask_e > 0.0, sel, -jnp.inf)

    iota_k = lax.broadcasted_iota(jnp.int32, (TOP_K, tm), 0)
    idx = jnp.zeros((TOP_K, tm), F32)
    wts = jnp.zeros((TOP_K, tm), F32)
    member = jnp.zeros((N_EXPERTS, tm), F32)
    for j in range(TOP_K):
        _, ij = _first_argmax(cand, iota_e)
        hit = iota_e == ij
        wj = jnp.sum(jnp.where(hit, s, 0.0), axis=0, keepdims=True)
        cand = jnp.where(hit, -jnp.inf, cand)
        member = jnp.where(hit, 1.0, member)
        idx = jnp.where(iota_k == j, ij, idx)
        wts = jnp.where(iota_k == j, wj, wts)
    wts = wts / jnp.sum(wts, axis=0, keepdims=True) * ROUTED_SCALE
    idx_ref[...] = idx.astype(jnp.int32)
    wts_ref[...] = wts
    cnt_ref[0] = _mm_nt(jnp.ones((8, tm), BF16), member.astype(BF16))


def _route(sct, e_bias, tm):
    n_e, t = sct.shape
    nt = t // tm
    return pl.pallas_call(
        _route_kernel,
        grid=(nt,),
        in_specs=[pl.BlockSpec((n_e, tm), lambda i: (0, i)),
                  pl.BlockSpec((n_e, 1), lambda i: (0, 0))],
        out_specs=[pl.BlockSpec((TOP_K, tm), lambda i: (0, i)),
                   pl.BlockSpec((TOP_K, tm), lambda i: (0, i)),
                   pl.BlockSpec((1, 8, n_e), lambda i: (i, 0, 0))],
        out_shape=[jax.ShapeDtypeStruct((TOP_K, t), jnp.int32),
                   jax.ShapeDtypeStruct((TOP_K, t), F32),
                   jax.ShapeDtypeStruct((nt, 8, n_e), F32)],
        compiler_params=pltpu.CompilerParams(dimension_semantics=("parallel",),
                                             vmem_limit_bytes=VMEM_LIMIT),
        name="moe_route",
    )(sct, e_bias.reshape(n_e, 1).astype(F32))


def _dest_kernel(idx_ref, base_ref, dest_ref):
    idx = idx_ref[...].astype(F32)
    tm = idx.shape[1]
    iota_e = lax.broadcasted_iota(jnp.int32, (N_EXPERTS, tm), 0).astype(F32)
    hits = [iota_e == idx[j:j + 1, :] for j in range(TOP_K)]
    member = jnp.zeros((N_EXPERTS, tm), F32)
    for hit in hits:
        member = jnp.where(hit, 1.0, member)
    r = lax.broadcasted_iota(jnp.int32, (tm, tm), 0)
    c = lax.broadcasted_iota(jnp.int32, (tm, tm), 1)
    earlier = jnp.where(r < c, 1.0, 0.0).astype(BF16)
    rank = _mm(member.astype(BF16), earlier) + base_ref[0]
    iota_k = lax.broadcasted_iota(jnp.int32, (TOP_K, tm), 0)
    dest = jnp.zeros((TOP_K, tm), F32)
    for j, hit in enumerate(hits):
        dj = jnp.sum(jnp.where(hit, rank, 0.0), axis=0, keepdims=True)
        dest = jnp.where(iota_k == j, dj, dest)
    dest_ref[0] = dest.astype(jnp.int32)


def _dest_rows(idx, base, tm):
    t = idx.shape[1]
    nt = t // tm
    return pl.pallas_call(
        _dest_kernel,
        grid=(nt,),
        in_specs=[pl.BlockSpec((TOP_K, tm), lambda i: (0, i)),
                  pl.BlockSpec((1, N_EXPERTS, 1), lambda i: (i, 0, 0))],
        out_specs=pl.BlockSpec((1, TOP_K, tm), lambda i: (i, 0, 0)),
        out_shape=jax.ShapeDtypeStruct((nt, TOP_K, tm), jnp.int32),
        compiler_params=pltpu.CompilerParams(dimension_semantics=("parallel",),
                                             vmem_limit_bytes=VMEM_LIMIT),
        name="moe_dest",
    )(idx, base)


def _block_plan(tile_counts, n_tokens):
    n_blocks = (n_tokens * TOP_K + N_EXPERTS * (MOE_ROWS - 1) + MOE_ROWS - 1) // MOE_ROWS
    counts = jnp.sum(tile_counts, axis=0)
    padded = (counts + MOE_ROWS - 1) // MOE_ROWS * MOE_ROWS
    pend = jnp.cumsum(padded)
    pstart = pend - padded
    tile_base = pstart[None, :] + jnp.cumsum(tile_counts, axis=0) - tile_counts
    block_start = jnp.arange(n_blocks, dtype=jnp.int32) * MOE_ROWS
    block_e = jnp.minimum(jnp.searchsorted(pend, block_start, side='right'),
                          N_EXPERTS - 1).astype(jnp.int32)
    n_used = (pend[-1] // MOE_ROWS).astype(jnp.int32).reshape(1)
    return (tile_base.astype(F32)[:, :, None], counts.astype(jnp.int32), padded.astype(jnp.int32),
            pstart.astype(jnp.int32), block_e, n_used, n_blocks)


def _row_tile(ref, row):
    return ref.at[pl.ds(pl.multiple_of(row * SLABS, SLABS), SLABS)]


def _dispatch_kernel(cnt_ref, pad_ref, pst_ref, dest_hbm, x_ref, xs_hbm, dest_s, zrow, isem, csem, zsem,
                     *, tm):
    i = pl.program_id(0)
    n = pl.num_programs(0)
    cur = i % 2

    def idx_copy(tile, buf):
        return pltpu.make_async_copy(dest_hbm.at[tile], dest_s.at[buf], isem.at[buf])

    @pl.when(i == 0)
    def _():
        idx_copy(0, 0).start()
        zrow[...] = jnp.zeros_like(zrow)

        def per_expert(e, carry, wait):
            def per_row(r, c2):
                cp = pltpu.make_async_copy(zrow, _row_tile(xs_hbm, pst_ref[e] + r), zsem)
                if wait:
                    cp.wait()
                else:
                    cp.start()
                return c2
            return lax.fori_loop(cnt_ref[e], pad_ref[e], per_row, carry)

        lax.fori_loop(0, N_EXPERTS, functools.partial(per_expert, wait=False), 0)
        lax.fori_loop(0, N_EXPERTS, functools.partial(per_expert, wait=True), 0)

    idx_copy(i, cur).wait()

    @pl.when(i + 1 < n)
    def _():
        idx_copy(i + 1, 1 - cur).start()

    def per_token(t, carry):
        src = x_ref.at[pl.ds(pl.multiple_of(t * SLABS, SLABS), SLABS)]
        for j in range(TOP_K):
            pltpu.make_async_copy(src, _row_tile(xs_hbm, dest_s[cur, j, t]), csem).start()
        return carry

    lax.fori_loop(0, tm, per_token, 0)
    for j in range(TOP_K):
        pltpu.make_async_copy(x_ref, xs_hbm.at[pl.ds(0, tm * SLABS)], csem).wait()


def _dispatch(x1t, dest, counts, padded, pstart, n_blocks, tm):
    nt = dest.shape[0]
    any_spec = pl.BlockSpec(memory_space=pl.ANY)
    grid_spec = pltpu.PrefetchScalarGridSpec(
        num_scalar_prefetch=3,
        grid=(nt,),
        in_specs=[any_spec, pl.BlockSpec((tm * SLABS, LANES), lambda i, c, p, s: (i, 0))],
        out_specs=any_spec,
        scratch_shapes=[pltpu.SMEM((2, TOP_K, tm), jnp.int32),
                        pltpu.VMEM((SLABS, LANES), F32),
                        pltpu.SemaphoreType.DMA((2,)),
                        pltpu.SemaphoreType.DMA(()),
                        pltpu.SemaphoreType.DMA(())])
    return pl.pallas_call(
        functools.partial(_dispatch_kernel, tm=tm),
        grid_spec=grid_spec,
        out_shape=jax.ShapeDtypeStruct((n_blocks * MOE_ROWS * SLABS, LANES), F32),
        compiler_params=pltpu.CompilerParams(dimension_semantics=("arbitrary",),
                                             vmem_limit_bytes=VMEM_LIMIT),
        name="moe_dispatch",
    )(counts, padded, pstart, dest, x1t)


def _swiglu(xb, wg, wu, wd):
    hg = _mm(xb, wg)
    hu = _mm(xb, wu)
    return _mm((hg * jax.nn.sigmoid(hg) * hu).astype(BF16), wd)


def _experts_kernel(be_ref, nu_ref, xs_ref, wg_ref, wu_ref, wd_ref, ys_ref):
    b = pl.program_id(0)

    @pl.when(b < nu_ref[0])
    def _():
        xb = _from_token_tiles(xs_ref, MOE_ROWS).astype(BF16)
        _to_token_tiles(ys_ref, _swiglu(xb, wg_ref[0], wu_ref[0], wd_ref[0]))

    @pl.when(b >= nu_ref[0])
    def _():
        ys_ref[...] = jnp.zeros_like(ys_ref)


def _experts(xs, block_e, n_used, w_gate_e, w_up_e, w_down_e):
    n_blocks = block_e.shape[0]
    blk = MOE_ROWS * SLABS
    wspec = lambda shp: pl.BlockSpec((1,) + shp, lambda b, be, nu: (be[b], 0, 0))
    grid_spec = pltpu.PrefetchScalarGridSpec(
        num_scalar_prefetch=2,
        grid=(n_blocks,),
        in_specs=[pl.BlockSpec((blk, LANES), lambda b, be, nu: (jnp.minimum(b, nu[0] - 1), 0)),
                  wspec((D_MODEL, D_EXPERT)), wspec((D_MODEL, D_EXPERT)), wspec((D_EXPERT, D_MODEL))],
        out_specs=pl.BlockSpec((blk, LANES), lambda b, be, nu: (b, 0)))
    return pl.pallas_call(
        _experts_kernel,
        grid_spec=grid_spec,
        out_shape=jax.ShapeDtypeStruct(xs.shape, F32),
        compiler_params=pltpu.CompilerParams(dimension_semantics=("arbitrary",),
                                             vmem_limit_bytes=VMEM_LIMIT),
        name="moe_experts",
    )(block_e, n_used, xs, w_gate_e, w_up_e, w_down_e)


def _final_kernel(dest_hbm, ys_hbm, x1_ref, wts_ref, wg_ref, wu_ref, wd_ref, g2_ref, b2_ref, o_ref,
                  dest_s, gbuf, isem, gsem, *, tm):
    i = pl.program_id(0)
    n = pl.num_programs(0)
    cur = i % 2
    nxt = 1 - cur

    def idx_copy(tile, buf):
        return pltpu.make_async_copy(dest_hbm.at[tile], dest_s.at[buf], isem.at[buf])

    def start_gather(buf):
        def per_token(t, carry):
            for j in range(TOP_K):
                dst = gbuf.at[buf, j, pl.ds(pl.multiple_of(t * SLABS, SLABS), SLABS)]
                pltpu.make_async_copy(_row_tile(ys_hbm, dest_s[buf, j, t]), dst, gsem.at[buf]).start()
            return carry
        lax.fori_loop(0, tm, per_token, 0)

    @pl.when(i == 0)
    def _():
        idx_copy(0, 0).start()
        idx_copy(0, 0).wait()
        start_gather(0)

        @pl.when(1 < n)
        def _():
            idx_copy(1, 1).start()

    @pl.when(i + 1 < n)
    def _():
        idx_copy(i + 1, nxt).wait()
        start_gather(nxt)

        @pl.when(i + 2 < n)
        def _():
            idx_copy(i + 2, cur).start()

    x1 = x1_ref[...]
    y = _swiglu(x1.astype(BF16), wg_ref[...], wu_ref[...], wd_ref[...])
    r = lax.broadcasted_iota(jnp.int32, (tm, tm), 0)
    c = lax.broadcasted_iota(jnp.int32, (tm, tm), 1)
    w_cols = _mm_nt(jnp.where(r == c, 1.0, 0.0).astype(F32), wts_ref[...], HI)
    for j in range(TOP_K):
        pltpu.make_async_copy(ys_hbm.at[pl.ds(0, tm * SLABS)], gbuf.at[cur, j], gsem.at[cur]).wait()
    for j in range(TOP_K):
        y = y + w_cols[:, j:j + 1] * _from_token_tiles(gbuf.at[cur, j], tm)
    o_ref[...] = _ln(ALPHA * x1 + y, g2_ref[...], b2_ref[...])


def _final(dest, ys, x1, wts, w_gate_s, w_up_s, w_down_s, ln2_g, ln2_b, tm):
    t = x1.shape[0]
    any_spec = pl.BlockSpec(memory_space=pl.ANY)
    row = pl.BlockSpec((tm, D_MODEL), lambda i: (i, 0))
    full = lambda a: pl.BlockSpec(a.shape, lambda i: (0, 0))
    return pl.pallas_call(
        functools.partial(_final_kernel, tm=tm),
        grid=(t // tm,),
        in_specs=[any_spec, any_spec, row, pl.BlockSpec((TOP_K, tm), lambda i: (0, i)),
                  full(w_gate_s), full(w_up_s), full(w_down_s), full(ln2_g), full(ln2_b)],
        out_specs=row,
        out_shape=jax.ShapeDtypeStruct((t, D_MODEL), F32),
        scratch_shapes=[pltpu.SMEM((2, TOP_K, tm), jnp.int32),
                        pltpu.VMEM((2, TOP_K, tm * SLABS, LANES), F32),
                        pltpu.SemaphoreType.DMA((2,)),
                        pltpu.SemaphoreType.DMA((2,))],
        compiler_params=pltpu.CompilerParams(dimension_semantics=("arbitrary",),
                                             vmem_limit_bytes=VMEM_LIMIT),
        name="shared_combine_ln2",
    )(dest, ys, x1, wts, w_gate_s, w_up_s, w_down_s, ln2_g, ln2_b)


def _trunk(x, prm):
    b, l, d = x.shape
    t = b * l
    tm = 512
    row2 = lambda a: a.reshape(1, -1).astype(F32)
    xn, p_rwkv, q, k, v, ga, gn = _ln_proj(x.reshape(t, d), row2(prm['ln_in_g']), row2(prm['ln_in_b']),
                                           prm['w_in_rwkv'], prm['w_in_na'], prm['w_in_gate'], tm)
    y_a = _rwkv_branch(p_rwkv.reshape(b, l, RWKV_IN), prm['mu_shift'], prm['w0'], prm['w2'], prm['a0'],
                       prm['a2'], prm['g2'], prm['k_k'], prm['k_a'], prm['r_k'], prm['lnx_g'], prm['lnx_b'])
    y_n = _na_attention(q.reshape(b, l, D_NA), k.reshape(b, l, D_NA), v.reshape(b, l, D_NA),
                        prm['na_bias'])
    x1, x1t, sct = _merge(y_a.reshape(t, D_RWKV), y_n.reshape(t, D_NA), ga, gn, xn,
                          prm['w_up_a'], prm['w_up_n'], prm['w_out'],
                          row2(prm['ln1_g']), row2(prm['ln1_b']), prm['w_router_t'], tm)
    return _moe(x1, x1t, sct, prm).reshape(b, l, d)


def _moe(x1, x1t, sct, prm):
    t = x1.shape[0]
    tm = MOE_TILE
    idx, wts, tile_cnt = _route(sct, prm['e_bias'], tm)
    tile_base, counts, padded, pstart, block_e, n_used, n_blocks = _block_plan(
        tile_cnt[:, 0, :].astype(jnp.int32), t)
    dest = _dest_rows(idx, tile_base, tm)
    xs = _dispatch(x1t, dest, counts, padded, pstart, n_blocks, tm)
    ys = _experts(xs, block_e, n_used, prm['w_gate_e'], prm['w_up_e'], prm['w_down_e'])
    row2 = lambda a: a.reshape(1, -1).astype(F32)
    return _final(dest, ys, x1, wts, prm['w_gate_s'], prm['w_up_s'], prm['w_down_s'],
                  row2(prm['ln2_g']), row2(prm['ln2_b']), tm)


def kernel(x_prompt, x_sample, ln_in_g, ln_in_b, w_in, mu_shift, w0, w2, a0, a2, g2, k_k, k_a, r_k,
           lnx_g, lnx_b, rpb, w_up_a, w_up_n, w_out, ln1_g, ln1_b, w_router, e_bias,
           w_gate_e, w_up_e, w_down_e, w_gate_s, w_up_s, w_down_s, ln2_g, ln2_b):
    assert w_in.shape[0] == DEPTH == 1
    w_in0 = w_in[0].astype(BF16)
    prm = dict(
        ln_in_g=ln_in_g, ln_in_b=ln_in_b,
        w_in_rwkv=w_in0[:, :RWKV_IN], w_in_na=w_in0[:, RWKV_IN:RWKV_IN + NA_IN],
        w_in_gate=w_in0[:, RWKV_IN + NA_IN:],
        mu_shift=mu_shift[0], w0=w0[0], w2=w2[0], a0=a0[0], a2=a2[0], g2=g2[0], k_k=k_k[0], k_a=k_a[0],
        r_k=r_k[0], lnx_g=lnx_g[0], lnx_b=lnx_b[0], na_bias=_na_bias_table(rpb[0]),
        w_up_a=w_up_a[0].astype(BF16), w_up_n=w_up_n[0].astype(BF16), w_out=w_out[0].astype(BF16),
        ln1_g=ln1_g[0], ln1_b=ln1_b[0], w_router_t=w_router[0].T, e_bias=e_bias[0],
        w_gate_e=w_gate_e[0].astype(BF16), w_up_e=w_up_e[0].astype(BF16),
        w_down_e=w_down_e[0].astype(BF16),
        w_gate_s=w_gate_s[0].astype(BF16), w_up_s=w_up_s[0].astype(BF16),
        w_down_s=w_down_s[0].astype(BF16), ln2_g=ln2_g[0], ln2_b=ln2_b[0])
    return (_trunk(x_prompt, prm), _trunk(x_sample, prm))
```

```python
import functools

import jax
import jax.numpy as jnp
import numpy as np
from jax import lax
from jax.experimental import pallas as pl
from jax.experimental.pallas import tpu as pltpu

F32 = jnp.float32
BF16 = jnp.bfloat16

D_MODEL = 1024
GRID_W = 64
HEAD_DIM = 64
D_RWKV = 512
D_NA = 512
NA_WIN_H = 8
NA_WIN_W = 16
DECAY_LORA = 64
ICL_LORA = 64
GATE_LORA = 128
N_EXPERTS = 256
TOP_K = 8
N_GROUPS = 8
TOPK_GROUPS = 4
D_EXPERT = 256
ROUTED_SCALE = 2.5
LN_EPS = 1e-5
GN_EPS = 64e-5
DEPTH = 1
ALPHA = (2 * DEPTH) ** 0.25
RWKV_SIZES = (D_RWKV, D_RWKV, D_RWKV, DECAY_LORA, DECAY_LORA, ICL_LORA, ICL_LORA, GATE_LORA)
RWKV_IN = sum(RWKV_SIZES)
NA_IN = 3 * D_NA

LANES = 128
HEADS_PER_SLAB = LANES // HEAD_DIM
SLABS = D_MODEL // LANES
VMEM_LIMIT = 48 * 1024 * 1024

CHUNK = 64
MOE_ROWS = 128
MOE_TILE = 256
NEG_BIG = -1e30

HI = lax.Precision.HIGHEST


def _mm(a, b, precision=None):
    return jnp.dot(a, b, preferred_element_type=F32, precision=precision)


def _mm_nt(a, b, precision=None):
    return lax.dot_general(a, b, (((1,), (1,)), ((), ())), preferred_element_type=F32,
                           precision=precision)


def _ln(x, g, b):
    mu = jnp.mean(x, -1, keepdims=True)
    xc = x - mu
    var = jnp.mean(xc * xc, -1, keepdims=True)
    return xc * lax.rsqrt(var + LN_EPS) * g + b


def _ln_proj_kernel(x_ref, g_ref, b_ref, wr_ref, wn_ref, wg_ref,
                    xn_ref, pr_ref, q_ref, k_ref, v_ref, ga_ref, gn_ref):
    xn = _ln(x_ref[...], g_ref[...], b_ref[...])
    xn_ref[...] = xn
    xb = xn.astype(BF16)
    pr_ref[...] = _mm(xb, wr_ref[...])
    pn = _mm(xb, wn_ref[...])
    q_ref[...] = pn[:, :D_NA].astype(BF16)
    k_ref[...] = pn[:, D_NA:2 * D_NA].astype(BF16)
    v_ref[...] = pn[:, 2 * D_NA:].astype(BF16)
    pg = _mm(xb, wg_ref[...])
    ga_ref[...] = pg[:, :D_MODEL].astype(BF16)
    gn_ref[...] = pg[:, D_MODEL:].astype(BF16)


def _ln_proj(x, g, b, w_rwkv, w_na, w_gate, tm):
    t = x.shape[0]
    row = lambda n: pl.BlockSpec((tm, n), lambda i: (i, 0))
    full = lambda a: pl.BlockSpec(a.shape, lambda i: (0, 0))
    return pl.pallas_call(
        _ln_proj_kernel,
        grid=(t // tm,),
        in_specs=[row(D_MODEL), full(g), full(b), full(w_rwkv), full(w_na), full(w_gate)],
        out_specs=[row(D_MODEL), row(RWKV_IN), row(D_NA), row(D_NA), row(D_NA),
                   row(D_MODEL), row(D_MODEL)],
        out_shape=[jax.ShapeDtypeStruct((t, D_MODEL), F32),
                   jax.ShapeDtypeStruct((t, RWKV_IN), F32),
                   jax.ShapeDtypeStruct((t, D_NA), BF16),
                   jax.ShapeDtypeStruct((t, D_NA), BF16),
                   jax.ShapeDtypeStruct((t, D_NA), BF16),
                   jax.ShapeDtypeStruct((t, D_MODEL), BF16),
                   jax.ShapeDtypeStruct((t, D_MODEL), BF16)],
        compiler_params=pltpu.CompilerParams(dimension_semantics=("parallel",),
                                             vmem_limit_bytes=VMEM_LIMIT),
        name="ln_proj",
    )(x, g, b, w_rwkv, w_na, w_gate)


def _rwkv_kernel(rf_ref, vf_ref, kkf_ref, kdf_ref, bef_ref, lwf_ref,
                 rb_ref, vb_ref, kkb_ref, kdb_ref, beb_ref, lwb_ref, yf_ref, yb_ref, z_ref, *, n_slabs):
    @pl.when(pl.program_id(1) == 0)
    def _():
        z_ref[...] = jnp.zeros_like(z_ref)

    n2 = HEADS_PER_SLAB * CHUNK
    row = lax.broadcasted_iota(jnp.int32, (n2, n2), 0)
    col = lax.broadcasted_iota(jnp.int32, (n2, n2), 1)
    dts = (row & (CHUNK - 1)) - (col & (CHUNK - 1))
    blk16 = ((row & (CHUNK - 1)) >> 4) == ((col & (CHUNK - 1)) >> 4)
    eye = row == col
    crow = lax.broadcasted_iota(jnp.int32, (CHUNK, CHUNK), 0)
    ccol = lax.broadcasted_iota(jnp.int32, (CHUNK, CHUNK), 1)
    lane = lax.broadcasted_iota(jnp.int32, (CHUNK, LANES), 1)
    head0 = lane < HEAD_DIM

    def bd(x):
        return jnp.concatenate([jnp.where(head0, x, 0.0), jnp.where(head0, 0.0, x)], axis=0)

    def bf(x):
        return x.astype(BF16)

    chains = [(d, p) for d in range(2) for p in range(n_slabs)]
    sls = [slice(p * LANES, (p + 1) * LANES) for _, p in chains]
    strict = [dts > 0 if d == 0 else dts < 0 for d, _ in chains]
    incl = [dts >= 0 if d == 0 else dts <= 0 for d, _ in chains]
    tri_d = [jnp.where(crow >= ccol, 1.0, 0.0).astype(F32), jnp.where(crow <= ccol, 1.0, 0.0).astype(F32)]
    tri = [tri_d[d] for d, _ in chains]
    r_in = [(rf_ref, rb_ref)[d] for d, _ in chains]
    v_in = [(vf_ref, vb_ref)[d] for d, _ in chains]
    kk_in = [(kkf_ref, kkb_ref)[d] for d, _ in chains]
    kd_in = [(kdf_ref, kdb_ref)[d] for d, _ in chains]
    be_in = [(bef_ref, beb_ref)[d] for d, _ in chains]
    lw_in = [(lwf_ref, lwb_ref)[d] for d, _ in chains]
    ident = jnp.where(eye, 1.0, 0.0).astype(F32)

    def each(f, *cols):
        return [f(*args) for args in zip(*cols)]

    lw = each(lambda ref, sl: ref[0, 0, :, sl], lw_in, sls)
    cum = each(lambda t_, x: _mm(t_, x, HI), tri, lw)
    tot = each(lambda x: jnp.sum(x, axis=0, keepdims=True), lw)
    e_in = each(jnp.exp, cum)
    e_ex = each(lambda c_, l_: jnp.exp(c_ - l_), cum, lw)
    e_neg = each(lambda c_: jnp.exp(-c_), cum)
    e_rem = each(lambda t_, c_: jnp.exp(t_ - c_), tot, cum)
    gam = each(jnp.exp, tot)
    kk = each(lambda ref, sl: ref[0, :, sl].astype(F32), kk_in, sls)
    kd = each(lambda ref, sl: ref[0, 0, :, sl].astype(F32), kd_in, sls)
    be = each(lambda ref, sl: ref[0, 0, :, sl].astype(F32), be_in, sls)
    a_b = each(lambda x, e: bf(bd(x * e)), kk, e_ex)
    r_t = each(lambda ref, e, sl: bd(ref[0, :, sl].astype(F32) * e), r_in, e_in, sls)
    b_t = each(lambda x, e: bd(x * e), be, e_neg)
    k_t = each(lambda x, e: bd(x * e), kd, e_neg)
    b_h = each(lambda x, e: bd(x * e), be, e_rem)
    k_h = each(lambda x, e: bd(x * e), kd, e_rem)
    vv = each(lambda ref, sl: bf(bd(ref[0, :, sl].astype(F32))), v_in, sls)

    sc = each(lambda a, r, b, k: _mm_nt(jnp.concatenate([a, bf(r)], axis=0),
                                        bf(jnp.concatenate([b, k], axis=0))), a_b, r_t, b_t, k_t)
    lk = each(lambda m, s: bf(jnp.where(m, s[:n2, n2:], 0.0)), strict, sc)
    mb = each(lambda m, s: bf(jnp.where(m, s[n2:, :n2], 0.0)), incl, sc)
    mk = each(lambda m, s: bf(jnp.where(m, s[n2:, n2:], 0.0)), incl, sc)

    nn = each(lambda m, s: jnp.where(m, -s[:n2, :n2], 0.0), strict, sc)
    dg = each(lambda x: jnp.where(blk16, x, 0.0), nn)
    offb = each(lambda x, d_: bf(x - d_), nn, dg)
    d1b = each(bf, dg)
    d2b = each(lambda x: bf(_mm(x, x)), d1b)
    d4b = each(lambda x: bf(_mm(x, x)), d2b)
    d8b = each(lambda x: bf(_mm(x, x)), d4b)
    td = each(lambda d_: ident + d_, dg)
    td = each(lambda t_, d_: t_ + _mm(bf(t_), d_), td, d2b)
    td = each(lambda t_, d_: t_ + _mm(bf(t_), d_), td, d4b)
    td = each(lambda t_, d_: t_ + _mm(bf(t_), d_), td, d8b)
    tdb = each(bf, td)
    e1 = each(_mm, tdb, offb)
    e1b = each(bf, e1)
    e2 = each(lambda x: _mm(x, x), e1b)
    e3 = each(lambda x, y_: _mm(x, bf(y_)), e1b, e2)
    ttb = each(lambda x1_, x2_, x3_, t_: bf(_mm(bf(ident + x1_ + x2_ + x3_), t_)), e1, e2, e3, tdb)

    lkv = each(_mm, lk, vv)
    wb = each(lambda t_, a, l_: bf(_mm(t_, jnp.concatenate([a, bf(l_)], axis=1))), ttb, a_b, lkv)
    mw = each(_mm, mb, wb)
    mkv = each(_mm, mk, vv)
    bw = each(lambda b, w: _mm(bf(b.T), w), b_h, wb)
    kv = each(lambda k, v_: _mm(bf(k.T), v_), k_h, vv)
    r_hat = each(lambda r, m: bf(r - m[:, :n2]), r_t, mw)
    y_loc = each(lambda m, w: m - w[:, n2:], mkv, mw)
    g = each(lambda g_, b: bf(jnp.where(eye, jnp.broadcast_to(g_, (n2, n2)), 0.0) - b[:, :n2]), gam, bw)
    h = each(lambda k, b: k - b[:, n2:], kv, bw)

    n_ch = len(chains)
    z = [z_ref[i] for i in range(n_ch)]
    z_hi = each(bf, z)
    z_lo = each(lambda z_, zh: bf(z_ - zh.astype(F32)), z, z_hi)
    y = each(lambda r, zh, yl: _mm(r, zh) + yl, r_hat, z_hi, y_loc)
    z_new = each(lambda g_, zh, zl, h_: _mm(g_, zh) + _mm(g_, zl) + h_, g, z_hi, z_lo, h)
    for i, (d, _) in enumerate(chains):
        z_ref[i] = z_new[i]
        (yf_ref, yb_ref)[d][0, :, sls[i]] = y[i][:CHUNK] + y[i][CHUNK:]


def _rwkv_scan(r, v, kk, kd, be, lw):
    b, l, dr = r.shape
    nc = l // CHUNK
    n_slabs = dr // LANES
    fwd = pl.BlockSpec((1, CHUNK, dr), lambda bi, c: (bi, c, 0))
    bwd = pl.BlockSpec((1, CHUNK, dr), lambda bi, c: (bi, nc - 1 - c, 0))
    fwd_d = pl.BlockSpec((1, 1, CHUNK, dr), lambda bi, c: (0, bi, c, 0))
    bwd_d = pl.BlockSpec((1, 1, CHUNK, dr), lambda bi, c: (1, bi, nc - 1 - c, 0))
    return pl.pallas_call(
        functools.partial(_rwkv_kernel, n_slabs=n_slabs),
        grid=(b, nc),
        in_specs=[fwd, fwd, fwd, fwd_d, fwd_d, fwd_d, bwd, bwd, bwd, bwd_d, bwd_d, bwd_d],
        out_specs=[fwd, bwd],
        out_shape=[jax.ShapeDtypeStruct((b, l, dr), F32), jax.ShapeDtypeStruct((b, l, dr), F32)],
        scratch_shapes=[pltpu.VMEM((2 * n_slabs, HEADS_PER_SLAB * CHUNK, HEADS_PER_SLAB * HEAD_DIM), F32)],
        compiler_params=pltpu.CompilerParams(
            dimension_semantics=("parallel", "arbitrary"),
            vmem_limit_bytes=VMEM_LIMIT),
        name="rwkv_scan",
    )(r, v, kk, kd, be, lw, r, v, kk, kd, be, lw)


def _rwkv_prep_kernel(p_ref, prev_ref, next_ref, mu_ref, wdec_ref, w0_ref, wicl_ref, a0_ref, g2_ref,
                      kk_w_ref, ka_ref, rk_ref, hsum_ref,
                      r_ref, v_ref, kk_ref, kd_ref, be_ref, lw_ref, bonus_ref, gate_ref, *, tiles_per_seq):
    i = pl.program_id(0)
    p = p_ref[...]
    tm = p.shape[0]
    pos = i % tiles_per_seq
    prev_row = jnp.where(pos == 0, 0.0, prev_ref[7:8, :])
    next_row = jnp.where(pos == tiles_per_seq - 1, 0.0, next_ref[0:1, :])
    rowi = lax.broadcasted_iota(jnp.int32, (tm, 1), 0)
    up = jnp.where(rowi == 0, prev_row, pltpu.roll(p, 1, axis=0))
    dn = jnp.where(rowi == tm - 1, next_row, pltpu.roll(p, tm - 1, axis=0))
    ps = p + mu_ref[...] * (0.5 * (up + dn) - p)

    c = np.cumsum((0,) + RWKV_SIZES)
    xr, xk, xv = ps[:, c[0]:c[1]], ps[:, c[1]:c[2]], ps[:, c[2]:c[3]]
    dw, da, dg = ps[:, c[3]:c[5]], ps[:, c[5]:c[7]], ps[:, c[7]:c[8]]
    wl = w0_ref[...] + _mm(jnp.tanh(dw).astype(BF16), wdec_ref[...])
    neg = -wl
    softplus = jnp.maximum(neg, 0.0) + jnp.log(1.0 + jnp.exp(-jnp.abs(neg)))
    lw = -jnp.exp(-softplus - 0.5)
    a = jax.nn.sigmoid(a0_ref[...] + _mm(da.astype(BF16), wicl_ref[...]))
    kkr = xk * kk_w_ref[...]
    ss = _mm((kkr * kkr).astype(BF16), hsum_ref[...])
    kk = kkr * lax.rsqrt(jnp.maximum(ss, 1e-24))
    kd_sum = jnp.zeros_like(xk)
    for d in range(2):
        a_d = a[:, d * D_RWKV:(d + 1) * D_RWKV]
        kd = xk * (1.0 + (a_d - 1.0) * ka_ref[...])
        kd_sum = kd_sum + kd
        kd_ref[d] = kd.astype(BF16)
        be_ref[d] = (a_d * kk).astype(BF16)
        lw_ref[d] = lw[:, d * D_RWKV:(d + 1) * D_RWKV]
    r_ref[...] = xr.astype(BF16)
    v_ref[...] = xv.astype(BF16)
    kk_ref[...] = kk.astype(BF16)
    bonus_ref[...] = _mm((xr * rk_ref[...] * kd_sum).astype(BF16), hsum_ref[...]) * xv
    gate_ref[...] = _mm(jax.nn.sigmoid(dg).astype(BF16), g2_ref[...])


def _rwkv_branch(p_rwkv, b, l, prm, tm):
    t = b * l
    r, v, kk, kd, be, lw, bonus, gate = _rwkv_prep(p_rwkv, l, prm, tm)
    seq = lambda a: a.reshape(b, l, D_RWKV)
    seq2 = lambda a: a.reshape(2, b, l, D_RWKV)
    wkv_f, wkv_b = _rwkv_scan(seq(r), seq(v), seq(kk), seq2(kd), seq2(be), seq2(lw))
    return wkv_f.reshape(t, D_RWKV), wkv_b.reshape(t, D_RWKV), bonus, gate


def _block_diag2(w):
    z = jnp.zeros_like(w[0])
    return jnp.concatenate([jnp.concatenate([w[0], z], axis=1), jnp.concatenate([z, w[1]], axis=1)], axis=0)


def _head_sum_matrix(scale):
    h = np.arange(D_RWKV) // HEAD_DIM
    return jnp.asarray((h[:, None] == h[None, :]) * scale, BF16)


def _rwkv_prep(p_rwkv, seq_len, prm, tm):
    t = p_rwkv.shape[0]
    row2 = lambda a: a.reshape(1, -1).astype(F32)
    consts = [row2(prm['mu_shift']), _block_diag2(prm['w2']).astype(BF16), row2(prm['w0']),
              _block_diag2(prm['a2']).astype(BF16), row2(prm['a0']), prm['g2'].astype(BF16),
              row2(prm['k_k']), row2(prm['k_a']), row2(prm['r_k']), _head_sum_matrix(1.0)]
    full = lambda a: pl.BlockSpec(a.shape, lambda i: (0, 0))
    row = pl.BlockSpec((tm, D_RWKV), lambda i: (i, 0))
    row_d = pl.BlockSpec((2, tm, D_RWKV), lambda i: (0, i, 0))
    halo = tm // 8
    sd = lambda dt: jax.ShapeDtypeStruct((t, D_RWKV), dt)
    sd2 = lambda dt: jax.ShapeDtypeStruct((2, t, D_RWKV), dt)
    return pl.pallas_call(
        functools.partial(_rwkv_prep_kernel, tiles_per_seq=seq_len // tm),
        grid=(t // tm,),
        in_specs=[pl.BlockSpec((tm, RWKV_IN), lambda i: (i, 0)),
                  pl.BlockSpec((8, RWKV_IN), lambda i: (jnp.maximum(i * halo - 1, 0), 0)),
                  pl.BlockSpec((8, RWKV_IN), lambda i: (jnp.minimum((i + 1) * halo, t // 8 - 1), 0))]
        + [full(a) for a in consts],
        out_specs=[row, row, row, row_d, row_d, row_d, row, row],
        out_shape=[sd(BF16), sd(BF16), sd(BF16), sd2(BF16), sd2(BF16), sd2(F32), sd(F32), sd(F32)],
        compiler_params=pltpu.CompilerParams(dimension_semantics=("parallel",),
                                             vmem_limit_bytes=VMEM_LIMIT),
        name="rwkv_prep",
    )(p_rwkv, p_rwkv, p_rwkv, *consts)


def _na_bias_table(rpb):
    n_h = rpb.shape[0]
    var = np.arange(NA_WIN_H)[:, None]
    i = np.arange(NA_WIN_H)[None, :]
    c = np.arange(GRID_W)[:, None]
    kc = np.arange(GRID_W)[None, :]
    cs = np.clip(c - NA_WIN_W // 2, 0, GRID_W - NA_WIN_W)
    valid = (kc >= cs) & (kc < cs + NA_WIN_W)
    row_sel = (np.arange(2 * NA_WIN_H - 1)[None, None, :] == (i - var + NA_WIN_H - 1)[:, :, None])
    col_sel = (np.arange(2 * NA_WIN_W - 1)[None, None, :] == (kc - c + NA_WIN_W - 1)[:, :, None])
    col_sel = col_sel & valid[:, :, None]
    tab = jnp.einsum('hab,via,ckb->vhcik', rpb.astype(F32), row_sel.astype(np.float32),
                     col_sel.astype(np.float32), precision=HI)
    tab = jnp.where(valid[None, None, :, None, :], tab, NEG_BIG)
    return tab.reshape(NA_WIN_H, n_h * GRID_W, NA_WIN_H * GRID_W)


def _na_kernel(q_ref, k_ref, v_ref, bias_ref, o_ref, *, rows, n_slabs):
    r = pl.program_id(1)
    rs = jnp.clip(r - NA_WIN_H // 2, 0, rows - NA_WIN_H)
    start = pl.multiple_of(rs * GRID_W, GRID_W)
    band = NA_WIN_H * GRID_W
    scale = HEAD_DIM ** -0.5
    lane = lax.broadcasted_iota(jnp.int32, (GRID_W, LANES), 1)
    head0 = lane < HEAD_DIM
    for p in range(n_slabs):
        sl = slice(p * LANES, (p + 1) * LANES)
        q2 = q_ref[0, :, sl]
        zero = jnp.zeros_like(q2)
        lhs = jnp.concatenate([jnp.where(head0, q2, zero), jnp.where(head0, zero, q2)], axis=0)
        kb = k_ref[0, pl.ds(start, band), sl]
        vb = v_ref[0, pl.ds(start, band), sl]
        s = _mm_nt(lhs, kb) * scale + bias_ref[0, p * 2 * GRID_W:(p + 1) * 2 * GRID_W, :]
        m = jnp.max(s, axis=-1, keepdims=True)
        e = jnp.exp(s - m)
        den = jnp.sum(e, axis=-1, keepdims=True)
        o = _mm(e.astype(BF16), vb) / den
        o_ref[0, :, sl] = jnp.where(head0, o[:GRID_W], o[GRID_W:]).astype(o_ref.dtype)


def _na_attention(q, k, v, bias_tab):
    b, l, dn = q.shape
    rows = l // GRID_W
    assert rows >= NA_WIN_H
    n_slabs = dn // LANES

    def bias_idx(bi, r):
        rs = jnp.clip(r - NA_WIN_H // 2, 0, rows - NA_WIN_H)
        return (r - rs, 0, 0)

    return pl.pallas_call(
        functools.partial(_na_kernel, rows=rows, n_slabs=n_slabs),
        grid=(b, rows),
        in_specs=[pl.BlockSpec((1, GRID_W, dn), lambda bi, r: (bi, r, 0)),
                  pl.BlockSpec((1, l, dn), lambda bi, r: (bi, 0, 0)),
                  pl.BlockSpec((1, l, dn), lambda bi, r: (bi, 0, 0)),
                  pl.BlockSpec((1,) + bias_tab.shape[1:], bias_idx)],
        out_specs=pl.BlockSpec((1, GRID_W, dn), lambda bi, r: (bi, r, 0)),
        out_shape=jax.ShapeDtypeStruct((b, l, dn), BF16),
        compiler_params=pltpu.CompilerParams(dimension_semantics=("parallel", "arbitrary"),
                                             vmem_limit_bytes=VMEM_LIMIT),
        name="na_attention",
    )(q, k, v, bias_tab)


def _to_token_tiles(dst_ref, x):
    n = x.shape[0]
    for s in range(SLABS):
        dst_ref[pl.ds(s, n, stride=SLABS), :] = x[:, s * LANES:(s + 1) * LANES]


def _from_token_tiles(src_ref, n):
    return jnp.concatenate([src_ref[pl.ds(s, n, stride=SLABS), :] for s in range(SLABS)], axis=1)


def _merge_kernel(wf_ref, wb_ref, bonus_ref, gate_ref, havg_ref, lg_ref, lb_ref,
                  yn_ref, ga_ref, gn_ref, xn_ref, wa_ref, wn_ref, wo_ref, g1_ref, b1_ref,
                  wrt_ref, x1_ref, x1t_ref, sct_ref):
    wkv = wf_ref[...] + wb_ref[...]
    xc = wkv - _mm(wkv.astype(BF16), havg_ref[...])
    var = _mm((xc * xc).astype(BF16), havg_ref[...])
    ya = (xc * lax.rsqrt(var + GN_EPS) * lg_ref[...] + lb_ref[...] + bonus_ref[...]) * gate_ref[...]
    up_a = _mm(ya.astype(BF16), wa_ref[...])
    up_n = _mm(yn_ref[...], wn_ref[...])
    merged = (jax.nn.sigmoid(ga_ref[...].astype(F32)) * up_a
              + jax.nn.sigmoid(gn_ref[...].astype(F32)) * up_n)
    mix = _mm(merged.astype(BF16), wo_ref[...])
    x1 = _ln(ALPHA * xn_ref[...] + mix, g1_ref[...], b1_ref[...])
    x1_ref[...] = x1
    _to_token_tiles(x1t_ref, x1)
    sct_ref[...] = jax.nn.sigmoid(_mm_nt(wrt_ref[...], x1, HI))


def _merge(wkv_f, wkv_b, bonus, gate, lnx_g, lnx_b, yn, ga, gn, xn, w_up_a, w_up_n, w_out, ln1_g, ln1_b,
           w_router_t, tm):
    t = xn.shape[0]
    row = lambda n: pl.BlockSpec((tm, n), lambda i: (i, 0))
    full = lambda a: pl.BlockSpec(a.shape, lambda i: (0, 0))
    havg = _head_sum_matrix(1.0 / HEAD_DIM)
    return pl.pallas_call(
        _merge_kernel,
        grid=(t // tm,),
        in_specs=[row(D_RWKV), row(D_RWKV), row(D_RWKV), row(D_RWKV), full(havg), full(lnx_g), full(lnx_b),
                  row(D_NA), row(D_MODEL), row(D_MODEL), row(D_MODEL),
                  full(w_up_a), full(w_up_n), full(w_out), full(ln1_g), full(ln1_b), full(w_router_t)],
        out_specs=[row(D_MODEL), pl.BlockSpec((tm * SLABS, LANES), lambda i: (i, 0)),
                   pl.BlockSpec((N_EXPERTS, tm), lambda i: (0, i))],
        out_shape=[jax.ShapeDtypeStruct((t, D_MODEL), F32),
                   jax.ShapeDtypeStruct((t * SLABS, LANES), F32),
                   jax.ShapeDtypeStruct((N_EXPERTS, t), F32)],
        compiler_params=pltpu.CompilerParams(dimension_semantics=("parallel",),
                                             vmem_limit_bytes=VMEM_LIMIT),
        name="merge_ln1_router",
    )(wkv_f, wkv_b, bonus, gate, havg, lnx_g, lnx_b, yn, ga, gn, xn, w_up_a, w_up_n, w_out, ln1_g, ln1_b,
      w_router_t)


def _first_argmax(vals, iota):
    m = jnp.max(vals, axis=0, keepdims=True)
    first = jnp.min(jnp.where(vals == m, iota, float(vals.shape[0])), axis=0, keepdims=True)
    return m, first


def _route_kernel(sct_ref, bias_ref, idx_ref, wts_ref, cnt_ref):
    s = sct_ref[...]
    tm = s.shape[1]
    sel = s + bias_ref[...]
    gsz = N_EXPERTS // N_GROUPS
    iota_g = lax.broadcasted_iota(jnp.int32, (gsz, tm), 0).astype(F32)
    iota_8 = lax.broadcasted_iota(jnp.int32, (N_GROUPS, tm), 0).astype(F32)
    iota_e = lax.broadcasted_iota(jnp.int32, (N_EXPERTS, tm), 0).astype(F32)

    gs = jnp.zeros((N_GROUPS, tm), F32)
    for g in range(N_GROUPS):
        blk = sel[g * gsz:(g + 1) * gsz, :]
        m1, i1 = _first_argmax(blk, iota_g)
        m2 = jnp.max(jnp.where(iota_g == i1, -jnp.inf, blk), axis=0, keepdims=True)
        gs = jnp.where(iota_8 == float(g), m1 + m2, gs)

    chosen = jnp.zeros((N_GROUPS, tm), F32)
    for _ in range(TOPK_GROUPS):
        _, gi = _first_argmax(gs, iota_8)
        hit = iota_8 == gi
        chosen = jnp.where(hit, 1.0, chosen)
        gs = jnp.where(hit, -jnp.inf, gs)
    mask_e = jnp.concatenate([jnp.broadcast_to(chosen[g:g + 1, :], (gsz, tm)) for g in range(N_GROUPS)],
                             axis=0)
    cand = jnp.where(mask_e > 0.0, sel, -jnp.inf)

    iota_k = lax.broadcasted_iota(jnp.int32, (TOP_K, tm), 0)
    idx = jnp.zeros((TOP_K, tm), F32)
    wts = jnp.zeros((TOP_K, tm), F32)
    member = jnp.zeros((N_EXPERTS, tm), F32)
    for j in range(TOP_K):
        _, ij = _first_argmax(cand, iota_e)
        hit = iota_e == ij
        wj = jnp.sum(jnp.where(hit, s, 0.0), axis=0, keepdims=True)
        cand = jnp.where(hit, -jnp.inf, cand)
        member = jnp.where(hit, 1.0, member)
        idx = jnp.where(iota_k == j, ij, idx)
        wts = jnp.where(iota_k == j, wj, wts)
    wts = wts / jnp.sum(wts, axis=0, keepdims=True) * ROUTED_SCALE
    idx_ref[...] = idx.astype(jnp.int32)
    wts_ref[...] = wts
    cnt_ref[0] = _mm_nt(jnp.ones((8, tm), BF16), member.astype(BF16))


def _route(sct, e_bias, tm):
    n_e, t = sct.shape
    nt = t // tm
    return pl.pallas_call(
        _route_kernel,
        grid=(nt,),
        in_specs=[pl.BlockSpec((n_e, tm), lambda i: (0, i)),
                  pl.BlockSpec((n_e, 1), lambda i: (0, 0))],
        out_specs=[pl.BlockSpec((TOP_K, tm), lambda i: (0, i)),
                   pl.BlockSpec((TOP_K, tm), lambda i: (0, i)),
                   pl.BlockSpec((1, 8, n_e), lambda i: (i, 0, 0))],
        out_shape=[jax.ShapeDtypeStruct((TOP_K, t), jnp.int32),
                   jax.ShapeDtypeStruct((TOP_K, t), F32),
                   jax.ShapeDtypeStruct((nt, 8, n_e), F32)],
        compiler_params=pltpu.CompilerParams(dimension_semantics=("parallel",),
                                             vmem_limit_bytes=VMEM_LIMIT),
        name="moe_route",
    )(sct, e_bias.reshape(n_e, 1).astype(F32))


def _dest_kernel(idx_ref, base_ref, dest_ref):
    idx = idx_ref[...].astype(F32)
    tm = idx.shape[1]
    iota_e = lax.broadcasted_iota(jnp.int32, (N_EXPERTS, tm), 0).astype(F32)
    hits = [iota_e == idx[j:j + 1, :] for j in range(TOP_K)]
    member = jnp.zeros((N_EXPERTS, tm), F32)
    for hit in hits:
        member = jnp.where(hit, 1.0, member)
    r = lax.broadcasted_iota(jnp.int32, (tm, tm), 0)
    c = lax.broadcasted_iota(jnp.int32, (tm, tm), 1)
    earlier = jnp.where(r < c, 1.0, 0.0).astype(BF16)
    rank = _mm(member.astype(BF16), earlier) + base_ref[0]
    iota_k = lax.broadcasted_iota(jnp.int32, (TOP_K, tm), 0)
    dest = jnp.zeros((TOP_K, tm), F32)
    for j, hit in enumerate(hits):
        dj = jnp.sum(jnp.where(hit, rank, 0.0), axis=0, keepdims=True)
        dest = jnp.where(iota_k == j, dj, dest)
    dest_ref[0] = dest.astype(jnp.int32)


def _dest_rows(idx, base, tm):
    t = idx.shape[1]
    nt = t // tm
    return pl.pallas_call(
        _dest_kernel,
        grid=(nt,),
        in_specs=[pl.BlockSpec((TOP_K, tm), lambda i: (0, i)),
                  pl.BlockSpec((1, N_EXPERTS, 1), lambda i: (i, 0, 0))],
        out_specs=pl.BlockSpec((1, TOP_K, tm), lambda i: (i, 0, 0)),
        out_shape=jax.ShapeDtypeStruct((nt, TOP_K, tm), jnp.int32),
        compiler_params=pltpu.CompilerParams(dimension_semantics=("parallel",),
                                             vmem_limit_bytes=VMEM_LIMIT),
        name="moe_dest",
    )(idx, base)


def _block_plan(tile_counts, n_tokens):
    n_blocks = (n_tokens * TOP_K + N_EXPERTS * (MOE_ROWS - 1) + MOE_ROWS - 1) // MOE_ROWS
    n_blocks += n_blocks % 2
    counts = jnp.sum(tile_counts, axis=0)
    padded = (counts + MOE_ROWS - 1) // MOE_ROWS * MOE_ROWS
    odd = (jnp.sum(padded) // MOE_ROWS) % 2
    padded = padded.at[N_EXPERTS - 1].add(odd * MOE_ROWS)
    pend = jnp.cumsum(padded)
    pstart = pend - padded
    tile_base = pstart[None, :] + jnp.cumsum(tile_counts, axis=0) - tile_counts
    block_start = jnp.arange(n_blocks, dtype=jnp.int32) * MOE_ROWS
    block_e = jnp.minimum(jnp.sum(pend[None, :] <= block_start[:, None], axis=1),
                          N_EXPERTS - 1).astype(jnp.int32)
    n_used = (pend[-1] // MOE_ROWS).astype(jnp.int32).reshape(1)
    return (tile_base.astype(F32)[:, :, None], counts.astype(jnp.int32), padded.astype(jnp.int32),
            pstart.astype(jnp.int32), block_e, n_used, n_blocks)


def _row_tile(ref, row):
    return ref.at[pl.ds(pl.multiple_of(row * SLABS, SLABS), SLABS)]


def _dispatch_kernel(cnt_ref, pad_ref, pst_ref, dest_hbm, x_ref, xs_hbm, dest_s, zrow, isem, csem, zsem,
                     *, tm):
    i = pl.program_id(0)
    n = pl.num_programs(0)
    cur = i % 2

    def idx_copy(tile, buf):
        return pltpu.make_async_copy(dest_hbm.at[tile], dest_s.at[buf], isem.at[buf])

    @pl.when(i == 0)
    def _():
        idx_copy(0, 0).start()
        zrow[...] = jnp.zeros_like(zrow)

        def per_expert(e, carry, wait):
            def per_row(r, c2):
                cp = pltpu.make_async_copy(zrow, _row_tile(xs_hbm, pst_ref[e] + r), zsem)
                if wait:
                    cp.wait()
                else:
                    cp.start()
                return c2
            return lax.fori_loop(cnt_ref[e], pad_ref[e], per_row, carry)

        lax.fori_loop(0, N_EXPERTS, functools.partial(per_expert, wait=False), 0)
        lax.fori_loop(0, N_EXPERTS, functools.partial(per_expert, wait=True), 0)

    idx_copy(i, cur).wait()

    @pl.when(i + 1 < n)
    def _():
        idx_copy(i + 1, 1 - cur).start()

    def per_token(t, carry):
        src = x_ref.at[pl.ds(pl.multiple_of(t * SLABS, SLABS), SLABS)]
        for j in range(TOP_K):
            pltpu.make_async_copy(src, _row_tile(xs_hbm, dest_s[cur, j, t]), csem).start()
        return carry

    lax.fori_loop(0, tm, per_token, 0)
    for j in range(TOP_K):
        pltpu.make_async_copy(x_ref, xs_hbm.at[pl.ds(0, tm * SLABS)], csem).wait()


def _dispatch(x1t, dest, counts, padded, pstart, n_blocks, tm):
    nt = dest.shape[0]
    any_spec = pl.BlockSpec(memory_space=pl.ANY)
    grid_spec = pltpu.PrefetchScalarGridSpec(
        num_scalar_prefetch=3,
        grid=(nt,),
        in_specs=[any_spec, pl.BlockSpec((tm * SLABS, LANES), lambda i, c, p, s: (i, 0))],
        out_specs=any_spec,
        scratch_shapes=[pltpu.SMEM((2, TOP_K, tm), jnp.int32),
                        pltpu.VMEM((SLABS, LANES), F32),
                        pltpu.SemaphoreType.DMA((2,)),
                        pltpu.SemaphoreType.DMA(()),
                        pltpu.SemaphoreType.DMA(())])
    return pl.pallas_call(
        functools.partial(_dispatch_kernel, tm=tm),
        grid_spec=grid_spec,
        out_shape=jax.ShapeDtypeStruct((n_blocks * MOE_ROWS * SLABS, LANES), F32),
        compiler_params=pltpu.CompilerParams(dimension_semantics=("arbitrary",),
                                             vmem_limit_bytes=VMEM_LIMIT),
        name="moe_dispatch",
    )(counts, padded, pstart, dest, x1t)


def _swiglu(xb, wg, wu, wd):
    hg = _mm(xb, wg)
    hu = _mm(xb, wu)
    return _mm((hg * jax.nn.sigmoid(hg) * hu).astype(BF16), wd)


def _experts_kernel(be_ref, nu_ref, xs_ref, wga_ref, wua_ref, wda_ref, wgb_ref, wub_ref, wdb_ref, ys_ref):
    s = pl.program_id(0)
    blk = MOE_ROWS * SLABS

    @pl.when(2 * s < nu_ref[0])
    def _():
        halves = [xs_ref.at[pl.ds(0, blk)], xs_ref.at[pl.ds(blk, blk)]]
        wg = [wga_ref[0], wgb_ref[0]]
        wu = [wua_ref[0], wub_ref[0]]
        wd = [wda_ref[0], wdb_ref[0]]
        xb = [_from_token_tiles(h, MOE_ROWS).astype(BF16) for h in halves]
        hg = [_mm(x, w) for x, w in zip(xb, wg)]
        hu = [_mm(x, w) for x, w in zip(xb, wu)]
        hh = [(g * jax.nn.sigmoid(g) * u).astype(BF16) for g, u in zip(hg, hu)]
        out = [_mm(h, w) for h, w in zip(hh, wd)]
        _to_token_tiles(ys_ref.at[pl.ds(0, blk)], out[0])
        _to_token_tiles(ys_ref.at[pl.ds(blk, blk)], out[1])

    @pl.when(2 * s >= nu_ref[0])
    def _():
        ys_ref[...] = jnp.zeros_like(ys_ref)


def _experts(xs, block_e, n_used, w_gate_e, w_up_e, w_down_e):
    n_pairs = block_e.shape[0] // 2
    blk2 = 2 * MOE_ROWS * SLABS
    wspec = lambda shp, k: pl.BlockSpec((1,) + shp, lambda s, be, nu: (be[2 * s + k], 0, 0))
    wspecs = [wspec(shp, k) for k in range(2)
              for shp in ((D_MODEL, D_EXPERT), (D_MODEL, D_EXPERT), (D_EXPERT, D_MODEL))]
    grid_spec = pltpu.PrefetchScalarGridSpec(
        num_scalar_prefetch=2,
        grid=(n_pairs,),
        in_specs=[pl.BlockSpec((blk2, LANES), lambda s, be, nu: (jnp.minimum(s, nu[0] // 2 - 1), 0))]
        + wspecs,
        out_specs=pl.BlockSpec((blk2, LANES), lambda s, be, nu: (s, 0)))
    return pl.pallas_call(
        _experts_kernel,
        grid_spec=grid_spec,
        out_shape=jax.ShapeDtypeStruct(xs.shape, F32),
        compiler_params=pltpu.CompilerParams(dimension_semantics=("arbitrary",),
                                             vmem_limit_bytes=VMEM_LIMIT),
        name="moe_experts",
    )(block_e, n_used, xs, w_gate_e, w_up_e, w_down_e, w_gate_e, w_up_e, w_down_e)


def _final_kernel(dest_hbm, ys_hbm, x1_ref, wts_ref, wg_ref, wu_ref, wd_ref, g2_ref, b2_ref, o_ref,
                  dest_s, gbuf, isem, gsem, *, tm):
    i = pl.program_id(0)
    n = pl.num_programs(0)
    cur = i % 2
    nxt = 1 - cur

    def idx_copy(tile, buf):
        return pltpu.make_async_copy(dest_hbm.at[tile], dest_s.at[buf], isem.at[buf])

    def start_gather(buf):
        def per_token(t, carry):
            for j in range(TOP_K):
                dst = gbuf.at[buf, j, pl.ds(pl.multiple_of(t * SLABS, SLABS), SLABS)]
                pltpu.make_async_copy(_row_tile(ys_hbm, dest_s[buf, j, t]), dst, gsem.at[buf]).start()
            return carry
        lax.fori_loop(0, tm, per_token, 0)

    @pl.when(i == 0)
    def _():
        idx_copy(0, 0).start()
        idx_copy(0, 0).wait()
        start_gather(0)

        @pl.when(1 < n)
        def _():
            idx_copy(1, 1).start()

    @pl.when(i + 1 < n)
    def _():
        idx_copy(i + 1, nxt).wait()
        start_gather(nxt)

        @pl.when(i + 2 < n)
        def _():
            idx_copy(i + 2, cur).start()

    x1 = x1_ref[...]
    y = _swiglu(x1.astype(BF16), wg_ref[...], wu_ref[...], wd_ref[...])
    r = lax.broadcasted_iota(jnp.int32, (tm, tm), 0)
    c = lax.broadcasted_iota(jnp.int32, (tm, tm), 1)
    w_cols = _mm_nt(jnp.where(r == c, 1.0, 0.0).astype(F32), wts_ref[...], HI)
    for j in range(TOP_K):
        pltpu.make_async_copy(ys_hbm.at[pl.ds(0, tm * SLABS)], gbuf.at[cur, j], gsem.at[cur]).wait()
    for j in range(TOP_K):
        y = y + w_cols[:, j:j + 1] * _from_token_tiles(gbuf.at[cur, j], tm)
    o_ref[...] = _ln(ALPHA * x1 + y, g2_ref[...], b2_ref[...])


def _final(dest, ys, x1, wts, w_gate_s, w_up_s, w_down_s, ln2_g, ln2_b, tm):
    t = x1.shape[0]
    any_spec = pl.BlockSpec(memory_space=pl.ANY)
    row = pl.BlockSpec((tm, D_MODEL), lambda i: (i, 0))
    full = lambda a: pl.BlockSpec(a.shape, lambda i: (0, 0))
    return pl.pallas_call(
        functools.partial(_final_kernel, tm=tm),
        grid=(t // tm,),
        in_specs=[any_spec, any_spec, row, pl.BlockSpec((TOP_K, tm), lambda i: (0, i)),
                  full(w_gate_s), full(w_up_s), full(w_down_s), full(ln2_g), full(ln2_b)],
        out_specs=row,
        out_shape=jax.ShapeDtypeStruct((t, D_MODEL), F32),
        scratch_shapes=[pltpu.SMEM((2, TOP_K, tm), jnp.int32),
                        pltpu.VMEM((2, TOP_K, tm * SLABS, LANES), F32),
                        pltpu.SemaphoreType.DMA((2,)),
                        pltpu.SemaphoreType.DMA((2,))],
        compiler_params=pltpu.CompilerParams(dimension_semantics=("arbitrary",),
                                             vmem_limit_bytes=VMEM_LIMIT),
        name="shared_combine_ln2",
    )(dest, ys, x1, wts, w_gate_s, w_up_s, w_down_s, ln2_g, ln2_b)


def _trunk(x, prm):
    b, l, d = x.shape
    t = b * l
    tm = 512
    row2 = lambda a: a.reshape(1, -1).astype(F32)
    xn, p_rwkv, q, k, v, ga, gn = _ln_proj(x.reshape(t, d), row2(prm['ln_in_g']), row2(prm['ln_in_b']),
                                           prm['w_in_rwkv'], prm['w_in_na'], prm['w_in_gate'], tm)
    wkv_f, wkv_b, bonus, gate = _rwkv_branch(p_rwkv, b, l, prm, tm)
    y_n = _na_attention(q.reshape(b, l, D_NA), k.reshape(b, l, D_NA), v.reshape(b, l, D_NA),
                        prm['na_bias'])
    x1, x1t, sct = _merge(wkv_f, wkv_b, bonus, gate, row2(prm['lnx_g']), row2(prm['lnx_b']),
                          y_n.reshape(t, D_NA), ga, gn, xn, prm['w_up_a'], prm['w_up_n'], prm['w_out'],
                          row2(prm['ln1_g']), row2(prm['ln1_b']), prm['w_router_t'], tm)
    return _moe(x1, x1t, sct, prm).reshape(b, l, d)


def _moe(x1, x1t, sct, prm):
    t = x1.shape[0]
    tm = MOE_TILE
    idx, wts, tile_cnt = _route(sct, prm['e_bias'], tm)
    tile_base, counts, padded, pstart, block_e, n_used, n_blocks = _block_plan(
        tile_cnt[:, 0, :].astype(jnp.int32), t)
    dest = _dest_rows(idx, tile_base, tm)
    xs = _dispatch(x1t, dest, counts, padded, pstart, n_blocks, tm)
    ys = _experts(xs, block_e, n_used, prm['w_gate_e'], prm['w_up_e'], prm['w_down_e'])
    row2 = lambda a: a.reshape(1, -1).astype(F32)
    return _final(dest, ys, x1, wts, prm['w_gate_s'], prm['w_up_s'], prm['w_down_s'],
                  row2(prm['ln2_g']), row2(prm['ln2_b']), tm)


def kernel(x_prompt, x_sample, ln_in_g, ln_in_b, w_in, mu_shift, w0, w2, a0, a2, g2, k_k, k_a, r_k,
           lnx_g, lnx_b, rpb, w_up_a, w_up_n, w_out, ln1_g, ln1_b, w_router, e_bias,
           w_gate_e, w_up_e, w_down_e, w_gate_s, w_up_s, w_down_s, ln2_g, ln2_b):
    assert w_in.shape[0] == DEPTH == 1
    w_in0 = w_in[0].astype(BF16)
    prm = dict(
        ln_in_g=ln_in_g, ln_in_b=ln_in_b,
        w_in_rwkv=w_in0[:, :RWKV_IN], w_in_na=w_in0[:, RWKV_IN:RWKV_IN + NA_IN],
        w_in_gate=w_in0[:, RWKV_IN + NA_IN:],
        mu_shift=mu_shift[0], w0=w0[0], w2=w2[0], a0=a0[0], a2=a2[0], g2=g2[0], k_k=k_k[0], k_a=k_a[0],
        r_k=r_k[0], lnx_g=lnx_g[0], lnx_b=lnx_b[0], na_bias=_na_bias_table(rpb[0]),
        w_up_a=w_up_a[0].astype(BF16), w_up_n=w_up_n[0].astype(BF16), w_out=w_out[0].astype(BF16),
        ln1_g=ln1_g[0], ln1_b=ln1_b[0], w_router_t=w_router[0].T, e_bias=e_bias[0],
        w_gate_e=w_gate_e[0].astype(BF16), w_up_e=w_up_e[0].astype(BF16),
        w_down_e=w_down_e[0].astype(BF16),
        w_gate_s=w_gate_s[0].astype(BF16), w_up_s=w_up_s[0].astype(BF16),
        w_down_s=w_down_s[0].astype(BF16), ln2_g=ln2_g[0], ln2_b=ln2_b[0])
    return (_trunk(x_prompt, prm), _trunk(x_sample, prm))
```

```python
import functools

import jax
import jax.numpy as jnp
import numpy as np
from jax import lax
from jax.experimental import pallas as pl
from jax.experimental.pallas import tpu as pltpu

F32 = jnp.float32
BF16 = jnp.bfloat16

D_MODEL = 1024
GRID_W = 64
HEAD_DIM = 64
D_RWKV = 512
D_NA = 512
NA_WIN_H = 8
NA_WIN_W = 16
DECAY_LORA = 64
ICL_LORA = 64
GATE_LORA = 128
N_EXPERTS = 256
TOP_K = 8
N_GROUPS = 8
TOPK_GROUPS = 4
D_EXPERT = 256
ROUTED_SCALE = 2.5
LN_EPS = 1e-5
GN_EPS = 64e-5
DEPTH = 1
ALPHA = (2 * DEPTH) ** 0.25
RWKV_SIZES = (D_RWKV, D_RWKV, D_RWKV, DECAY_LORA, DECAY_LORA, ICL_LORA, ICL_LORA, GATE_LORA)
RWKV_IN = sum(RWKV_SIZES)
NA_IN = 3 * D_NA

LANES = 128
HEADS_PER_SLAB = LANES // HEAD_DIM
SLABS = D_MODEL // LANES
VMEM_LIMIT = 48 * 1024 * 1024

CHUNK = 64
MOE_ROWS = 128
MOE_TILE = 256
MOE_GROUP = 4
NEG_BIG = -1e30

HI = lax.Precision.HIGHEST


def _mm(a, b, precision=None):
    return jnp.dot(a, b, preferred_element_type=F32, precision=precision)


def _mm_nt(a, b, precision=None):
    return lax.dot_general(a, b, (((1,), (1,)), ((), ())), preferred_element_type=F32,
                           precision=precision)


def _ln(x, g, b):
    mu = jnp.mean(x, -1, keepdims=True)
    xc = x - mu
    var = jnp.mean(xc * xc, -1, keepdims=True)
    return xc * lax.rsqrt(var + LN_EPS) * g + b


def _ln_proj_kernel(x_ref, g_ref, b_ref, wr_ref, wn_ref, wg_ref,
                    xn_ref, pr_ref, q_ref, k_ref, v_ref, ga_ref, gn_ref):
    xn = _ln(x_ref[...], g_ref[...], b_ref[...])
    xn_ref[...] = xn
    xb = xn.astype(BF16)
    pr_ref[...] = _mm(xb, wr_ref[...])
    pn = _mm(xb, wn_ref[...])
    q_ref[...] = pn[:, :D_NA].astype(BF16)
    k_ref[...] = pn[:, D_NA:2 * D_NA].astype(BF16)
    v_ref[...] = pn[:, 2 * D_NA:].astype(BF16)
    pg = _mm(xb, wg_ref[...])
    ga_ref[...] = pg[:, :D_MODEL].astype(BF16)
    gn_ref[...] = pg[:, D_MODEL:].astype(BF16)


def _ln_proj(x, g, b, w_rwkv, w_na, w_gate, tm):
    t = x.shape[0]
    row = lambda n: pl.BlockSpec((tm, n), lambda i: (i, 0))
    full = lambda a: pl.BlockSpec(a.shape, lambda i: (0, 0))
    return pl.pallas_call(
        _ln_proj_kernel,
        grid=(t // tm,),
        in_specs=[row(D_MODEL), full(g), full(b), full(w_rwkv), full(w_na), full(w_gate)],
        out_specs=[row(D_MODEL), row(RWKV_IN), row(D_NA), row(D_NA), row(D_NA),
                   row(D_MODEL), row(D_MODEL)],
        out_shape=[jax.ShapeDtypeStruct((t, D_MODEL), F32),
                   jax.ShapeDtypeStruct((t, RWKV_IN), F32),
                   jax.ShapeDtypeStruct((t, D_NA), BF16),
                   jax.ShapeDtypeStruct((t, D_NA), BF16),
                   jax.ShapeDtypeStruct((t, D_NA), BF16),
                   jax.ShapeDtypeStruct((t, D_MODEL), BF16),
                   jax.ShapeDtypeStruct((t, D_MODEL), BF16)],
        compiler_params=pltpu.CompilerParams(dimension_semantics=("parallel",),
                                             vmem_limit_bytes=VMEM_LIMIT),
        name="ln_proj",
    )(x, g, b, w_rwkv, w_na, w_gate)


def _rwkv_kernel(rf_ref, vf_ref, kkf_ref, kdf_ref, bef_ref, lwf_ref,
                 rb_ref, vb_ref, kkb_ref, kdb_ref, beb_ref, lwb_ref, yf_ref, yb_ref, z_ref, *, n_slabs):
    @pl.when(pl.program_id(1) == 0)
    def _():
        z_ref[...] = jnp.zeros_like(z_ref)

    n2 = HEADS_PER_SLAB * CHUNK
    row = lax.broadcasted_iota(jnp.int32, (n2, n2), 0)
    col = lax.broadcasted_iota(jnp.int32, (n2, n2), 1)
    dts = (row & (CHUNK - 1)) - (col & (CHUNK - 1))
    blk16 = ((row & (CHUNK - 1)) >> 4) == ((col & (CHUNK - 1)) >> 4)
    eye = row == col
    rowi = lax.broadcasted_iota(jnp.int32, (CHUNK, LANES), 0)
    lane = lax.broadcasted_iota(jnp.int32, (CHUNK, LANES), 1)
    head0 = lane < HEAD_DIM

    def bd(x):
        return jnp.concatenate([jnp.where(head0, x, 0.0), jnp.where(head0, 0.0, x)], axis=0)

    def bf(x):
        return x.astype(BF16)

    chains = [(d, p) for d in range(2) for p in range(n_slabs)]
    sls = [slice(p * LANES, (p + 1) * LANES) for _, p in chains]
    strict = [dts > 0 if d == 0 else dts < 0 for d, _ in chains]
    incl = [dts >= 0 if d == 0 else dts <= 0 for d, _ in chains]
    r_in = [(rf_ref, rb_ref)[d] for d, _ in chains]
    v_in = [(vf_ref, vb_ref)[d] for d, _ in chains]
    kk_in = [(kkf_ref, kkb_ref)[d] for d, _ in chains]
    kd_in = [(kdf_ref, kdb_ref)[d] for d, _ in chains]
    be_in = [(bef_ref, beb_ref)[d] for d, _ in chains]
    lw_in = [(lwf_ref, lwb_ref)[d] for d, _ in chains]
    ident = jnp.where(eye, 1.0, 0.0).astype(F32)

    def each(f, *cols):
        return [f(*args) for args in zip(*cols)]

    lw = each(lambda ref, sl: ref[0, 0, :, sl], lw_in, sls)
    def scan_rows(x, reverse):
        step = 1
        while step < CHUNK:
            if reverse:
                x = x + jnp.where(rowi < CHUNK - step, pltpu.roll(x, CHUNK - step, axis=0), 0.0)
            else:
                x = x + jnp.where(rowi >= step, pltpu.roll(x, step, axis=0), 0.0)
            step *= 2
        return x

    cum = [scan_rows(x, d == 1) for x, (d, _) in zip(lw, chains)]
    tot = each(lambda x: jnp.sum(x, axis=0, keepdims=True), lw)
    e_in = each(jnp.exp, cum)
    e_ex = each(lambda c_, l_: jnp.exp(c_ - l_), cum, lw)
    e_neg = each(lambda c_: jnp.exp(-c_), cum)
    e_rem = each(lambda t_, c_: jnp.exp(t_ - c_), tot, cum)
    gam = each(jnp.exp, tot)
    kk = each(lambda ref, sl: ref[0, :, sl].astype(F32), kk_in, sls)
    kd = each(lambda ref, sl: ref[0, 0, :, sl].astype(F32), kd_in, sls)
    be = each(lambda ref, sl: ref[0, 0, :, sl].astype(F32), be_in, sls)
    a_b = each(lambda x, e: bf(bd(x * e)), kk, e_ex)
    r_t = each(lambda ref, e, sl: bd(ref[0, :, sl].astype(F32) * e), r_in, e_in, sls)
    b_t = each(lambda x, e: bd(x * e), be, e_neg)
    k_t = each(lambda x, e: bd(x * e), kd, e_neg)
    b_h = each(lambda x, e: bd(x * e), be, e_rem)
    k_h = each(lambda x, e: bd(x * e), kd, e_rem)
    vv = each(lambda ref, sl: bf(bd(ref[0, :, sl].astype(F32))), v_in, sls)

    sc = each(lambda a, r, b, k: _mm_nt(jnp.concatenate([a, bf(r)], axis=0),
                                        bf(jnp.concatenate([b, k], axis=0))), a_b, r_t, b_t, k_t)
    lk = each(lambda m, s: bf(jnp.where(m, s[:n2, n2:], 0.0)), strict, sc)
    mb = each(lambda m, s: bf(jnp.where(m, s[n2:, :n2], 0.0)), incl, sc)
    mk = each(lambda m, s: bf(jnp.where(m, s[n2:, n2:], 0.0)), incl, sc)

    nn = each(lambda m, s: jnp.where(m, -s[:n2, :n2], 0.0), strict, sc)
    dg = each(lambda x: jnp.where(blk16, x, 0.0), nn)
    offb = each(lambda x, d_: bf(x - d_), nn, dg)
    d1b = each(bf, dg)
    d2b = each(lambda x: bf(_mm(x, x)), d1b)
    d4b = each(lambda x: bf(_mm(x, x)), d2b)
    d8b = each(lambda x: bf(_mm(x, x)), d4b)
    td = each(lambda d_: ident + d_, dg)
    td = each(lambda t_, d_: t_ + _mm(bf(t_), d_), td, d2b)
    td = each(lambda t_, d_: t_ + _mm(bf(t_), d_), td, d4b)
    td = each(lambda t_, d_: t_ + _mm(bf(t_), d_), td, d8b)
    tdb = each(bf, td)
    e1 = each(_mm, tdb, offb)
    e1b = each(bf, e1)
    e2 = each(lambda x: _mm(x, x), e1b)
    e3 = each(lambda x, y_: _mm(x, bf(y_)), e1b, e2)
    ttb = each(lambda x1_, x2_, x3_, t_: bf(_mm(bf(ident + x1_ + x2_ + x3_), t_)), e1, e2, e3, tdb)

    lkv = each(_mm, lk, vv)
    wb = each(lambda t_, a, l_: bf(_mm(t_, jnp.concatenate([a, bf(l_)], axis=1))), ttb, a_b, lkv)
    mw = each(_mm, mb, wb)
    mkv = each(_mm, mk, vv)
    bw = each(lambda b, w: _mm(bf(b.T), w), b_h, wb)
    kv = each(lambda k, v_: _mm(bf(k.T), v_), k_h, vv)
    r_hat = each(lambda r, m: bf(r - m[:, :n2]), r_t, mw)
    y_loc = each(lambda m, w: m - w[:, n2:], mkv, mw)
    g = each(lambda g_, b: bf(jnp.where(eye, jnp.broadcast_to(g_, (n2, n2)), 0.0) - b[:, :n2]), gam, bw)
    h = each(lambda k, b: k - b[:, n2:], kv, bw)

    n_ch = len(chains)
    z = [z_ref[i] for i in range(n_ch)]
    z_hi = each(bf, z)
    z_lo = each(lambda z_, zh: bf(z_ - zh.astype(F32)), z, z_hi)
    y = each(lambda r, zh, yl: _mm(r, zh) + yl, r_hat, z_hi, y_loc)
    z_new = each(lambda g_, zh, zl, h_: _mm(g_, zh) + _mm(g_, zl) + h_, g, z_hi, z_lo, h)
    for i, (d, _) in enumerate(chains):
        z_ref[i] = z_new[i]
        (yf_ref, yb_ref)[d][0, :, sls[i]] = y[i][:CHUNK] + y[i][CHUNK:]


def _rwkv_scan(r, v, kk, kd, be, lw):
    b, l, dr = r.shape
    nc = l // CHUNK
    n_slabs = dr // LANES
    fwd = pl.BlockSpec((1, CHUNK, dr), lambda bi, c: (bi, c, 0))
    bwd = pl.BlockSpec((1, CHUNK, dr), lambda bi, c: (bi, nc - 1 - c, 0))
    fwd_d = pl.BlockSpec((1, 1, CHUNK, dr), lambda bi, c: (0, bi, c, 0))
    bwd_d = pl.BlockSpec((1, 1, CHUNK, dr), lambda bi, c: (1, bi, nc - 1 - c, 0))
    return pl.pallas_call(
        functools.partial(_rwkv_kernel, n_slabs=n_slabs),
        grid=(b, nc),
        in_specs=[fwd, fwd, fwd, fwd_d, fwd_d, fwd_d, bwd, bwd, bwd, bwd_d, bwd_d, bwd_d],
        out_specs=[fwd, bwd],
        out_shape=[jax.ShapeDtypeStruct((b, l, dr), F32), jax.ShapeDtypeStruct((b, l, dr), F32)],
        scratch_shapes=[pltpu.VMEM((2 * n_slabs, HEADS_PER_SLAB * CHUNK, HEADS_PER_SLAB * HEAD_DIM), F32)],
        compiler_params=pltpu.CompilerParams(
            dimension_semantics=("parallel", "arbitrary"),
            vmem_limit_bytes=VMEM_LIMIT),
        name="rwkv_scan",
    )(r, v, kk, kd, be, lw, r, v, kk, kd, be, lw)


def _rwkv_prep_kernel(p_ref, prev_ref, next_ref, mu_ref, wdec_ref, w0_ref, wicl_ref, a0_ref, g2_ref,
                      kk_w_ref, ka_ref, rk_ref, hsum_ref,
                      r_ref, v_ref, kk_ref, kd_ref, be_ref, lw_ref, bonus_ref, gate_ref, *, tiles_per_seq):
    i = pl.program_id(0)
    p = p_ref[...]
    tm = p.shape[0]
    pos = i % tiles_per_seq
    prev_row = jnp.where(pos == 0, 0.0, prev_ref[7:8, :])
    next_row = jnp.where(pos == tiles_per_seq - 1, 0.0, next_ref[0:1, :])
    rowi = lax.broadcasted_iota(jnp.int32, (tm, 1), 0)
    up = jnp.where(rowi == 0, prev_row, pltpu.roll(p, 1, axis=0))
    dn = jnp.where(rowi == tm - 1, next_row, pltpu.roll(p, tm - 1, axis=0))
    ps = p + mu_ref[...] * (0.5 * (up + dn) - p)

    c = np.cumsum((0,) + RWKV_SIZES)
    xr, xk, xv = ps[:, c[0]:c[1]], ps[:, c[1]:c[2]], ps[:, c[2]:c[3]]
    dw, da, dg = ps[:, c[3]:c[5]], ps[:, c[5]:c[7]], ps[:, c[7]:c[8]]
    wl = w0_ref[...] + _mm(jnp.tanh(dw).astype(BF16), wdec_ref[...])
    neg = -wl
    softplus = jnp.maximum(neg, 0.0) + jnp.log(1.0 + jnp.exp(-jnp.abs(neg)))
    lw = -jnp.exp(-softplus - 0.5)
    a = jax.nn.sigmoid(a0_ref[...] + _mm(da.astype(BF16), wicl_ref[...]))
    kkr = xk * kk_w_ref[...]
    ss = _mm((kkr * kkr).astype(BF16), hsum_ref[...])
    kk = kkr * lax.rsqrt(jnp.maximum(ss, 1e-24))
    kd_sum = jnp.zeros_like(xk)
    for d in range(2):
        a_d = a[:, d * D_RWKV:(d + 1) * D_RWKV]
        kd = xk * (1.0 + (a_d - 1.0) * ka_ref[...])
        kd_sum = kd_sum + kd
        kd_ref[d] = kd.astype(BF16)
        be_ref[d] = (a_d * kk).astype(BF16)
        lw_ref[d] = lw[:, d * D_RWKV:(d + 1) * D_RWKV]
    r_ref[...] = xr.astype(BF16)
    v_ref[...] = xv.astype(BF16)
    kk_ref[...] = kk.astype(BF16)
    bonus_ref[...] = _mm((xr * rk_ref[...] * kd_sum).astype(BF16), hsum_ref[...]) * xv
    gate_ref[...] = _mm(jax.nn.sigmoid(dg).astype(BF16), g2_ref[...])


def _rwkv_branch(p_rwkv, b, l, prm, tm):
    t = b * l
    r, v, kk, kd, be, lw, bonus, gate = _rwkv_prep(p_rwkv, l, prm, tm)
    seq = lambda a: a.reshape(b, l, D_RWKV)
    seq2 = lambda a: a.reshape(2, b, l, D_RWKV)
    wkv_f, wkv_b = _rwkv_scan(seq(r), seq(v), seq(kk), seq2(kd), seq2(be), seq2(lw))
    return wkv_f.reshape(t, D_RWKV), wkv_b.reshape(t, D_RWKV), bonus, gate


def _block_diag2(w):
    z = jnp.zeros_like(w[0])
    return jnp.concatenate([jnp.concatenate([w[0], z], axis=1), jnp.concatenate([z, w[1]], axis=1)], axis=0)


def _head_sum_matrix(scale):
    h = np.arange(D_RWKV) // HEAD_DIM
    return jnp.asarray((h[:, None] == h[None, :]) * scale, BF16)


def _rwkv_prep(p_rwkv, seq_len, prm, tm):
    t = p_rwkv.shape[0]
    row2 = lambda a: a.reshape(1, -1).astype(F32)
    consts = [row2(prm['mu_shift']), _block_diag2(prm['w2']).astype(BF16), row2(prm['w0']),
              _block_diag2(prm['a2']).astype(BF16), row2(prm['a0']), prm['g2'].astype(BF16),
              row2(prm['k_k']), row2(prm['k_a']), row2(prm['r_k']), _head_sum_matrix(1.0)]
    full = lambda a: pl.BlockSpec(a.shape, lambda i: (0, 0))
    row = pl.BlockSpec((tm, D_RWKV), lambda i: (i, 0))
    row_d = pl.BlockSpec((2, tm, D_RWKV), lambda i: (0, i, 0))
    halo = tm // 8
    sd = lambda dt: jax.ShapeDtypeStruct((t, D_RWKV), dt)
    sd2 = lambda dt: jax.ShapeDtypeStruct((2, t, D_RWKV), dt)
    return pl.pallas_call(
        functools.partial(_rwkv_prep_kernel, tiles_per_seq=seq_len // tm),
        grid=(t // tm,),
        in_specs=[pl.BlockSpec((tm, RWKV_IN), lambda i: (i, 0)),
                  pl.BlockSpec((8, RWKV_IN), lambda i: (jnp.maximum(i * halo - 1, 0), 0)),
                  pl.BlockSpec((8, RWKV_IN), lambda i: (jnp.minimum((i + 1) * halo, t // 8 - 1), 0))]
        + [full(a) for a in consts],
        out_specs=[row, row, row, row_d, row_d, row_d, row, row],
        out_shape=[sd(BF16), sd(BF16), sd(BF16), sd2(BF16), sd2(BF16), sd2(F32), sd(F32), sd(F32)],
        compiler_params=pltpu.CompilerParams(dimension_semantics=("parallel",),
                                             vmem_limit_bytes=VMEM_LIMIT),
        name="rwkv_prep",
    )(p_rwkv, p_rwkv, p_rwkv, *consts)


def _na_bias_table(rpb):
    n_h = rpb.shape[0]
    var = np.arange(NA_WIN_H)[:, None]
    i = np.arange(NA_WIN_H)[None, :]
    c = np.arange(GRID_W)[:, None]
    kc = np.arange(GRID_W)[None, :]
    cs = np.clip(c - NA_WIN_W // 2, 0, GRID_W - NA_WIN_W)
    valid = (kc >= cs) & (kc < cs + NA_WIN_W)
    row_sel = (np.arange(2 * NA_WIN_H - 1)[None, None, :] == (i - var + NA_WIN_H - 1)[:, :, None])
    col_sel = (np.arange(2 * NA_WIN_W - 1)[None, None, :] == (kc - c + NA_WIN_W - 1)[:, :, None])
    col_sel = col_sel & valid[:, :, None]
    tab = jnp.einsum('hab,via,ckb->vhcik', rpb.astype(F32), row_sel.astype(np.float32),
                     col_sel.astype(np.float32), precision=HI)
    tab = jnp.where(valid[None, None, :, None, :], tab, NEG_BIG)
    return tab.reshape(NA_WIN_H, n_h * GRID_W, NA_WIN_H * GRID_W)


def _na_kernel(q_ref, k_ref, v_ref, bias_ref, o_ref, *, rows, n_slabs):
    r = pl.program_id(1)
    rs = jnp.clip(r - NA_WIN_H // 2, 0, rows - NA_WIN_H)
    start = pl.multiple_of(rs * GRID_W, GRID_W)
    band = NA_WIN_H * GRID_W
    scale = HEAD_DIM ** -0.5
    lane = lax.broadcasted_iota(jnp.int32, (GRID_W, LANES), 1)
    head0 = lane < HEAD_DIM
    sls = [slice(p * LANES, (p + 1) * LANES) for p in range(n_slabs)]

    def split_heads(q2):
        zero = jnp.zeros_like(q2)
        return jnp.concatenate([jnp.where(head0, q2, zero), jnp.where(head0, zero, q2)], axis=0)

    lhs = [split_heads(q_ref[0, :, sl]) for sl in sls]
    s = [_mm_nt(x, k_ref[0, pl.ds(start, band), sl]) for x, sl in zip(lhs, sls)]
    s = [x * scale + bias_ref[0, p * 2 * GRID_W:(p + 1) * 2 * GRID_W, :] for p, x in enumerate(s)]
    m = [jnp.max(x, axis=-1, keepdims=True) for x in s]
    e = [jnp.exp(x - mx) for x, mx in zip(s, m)]
    den = [jnp.sum(x, axis=-1, keepdims=True) for x in e]
    o = [_mm(x.astype(BF16), v_ref[0, pl.ds(start, band), sl]) for x, sl in zip(e, sls)]
    for x, d, sl in zip(o, den, sls):
        x = x / d
        o_ref[0, :, sl] = jnp.where(head0, x[:GRID_W], x[GRID_W:]).astype(o_ref.dtype)


def _na_attention(q, k, v, bias_tab):
    b, l, dn = q.shape
    rows = l // GRID_W
    assert rows >= NA_WIN_H
    n_slabs = dn // LANES

    def bias_idx(bi, r):
        rs = jnp.clip(r - NA_WIN_H // 2, 0, rows - NA_WIN_H)
        return (r - rs, 0, 0)

    return pl.pallas_call(
        functools.partial(_na_kernel, rows=rows, n_slabs=n_slabs),
        grid=(b, rows),
        in_specs=[pl.BlockSpec((1, GRID_W, dn), lambda bi, r: (bi, r, 0)),
                  pl.BlockSpec((1, l, dn), lambda bi, r: (bi, 0, 0)),
                  pl.BlockSpec((1, l, dn), lambda bi, r: (bi, 0, 0)),
                  pl.BlockSpec((1,) + bias_tab.shape[1:], bias_idx)],
        out_specs=pl.BlockSpec((1, GRID_W, dn), lambda bi, r: (bi, r, 0)),
        out_shape=jax.ShapeDtypeStruct((b, l, dn), BF16),
        compiler_params=pltpu.CompilerParams(dimension_semantics=("parallel", "arbitrary"),
                                             vmem_limit_bytes=VMEM_LIMIT),
        name="na_attention",
    )(q, k, v, bias_tab)


def _to_token_tiles(dst_ref, x):
    n = x.shape[0]
    for s in range(SLABS):
        dst_ref[pl.ds(s, n, stride=SLABS), :] = x[:, s * LANES:(s + 1) * LANES]


def _from_token_tiles(src_ref, n):
    return jnp.concatenate([src_ref[pl.ds(s, n, stride=SLABS), :] for s in range(SLABS)], axis=1)


def _merge_kernel(wf_ref, wb_ref, bonus_ref, gate_ref, havg_ref, lg_ref, lb_ref,
                  yn_ref, ga_ref, gn_ref, xn_ref, wa_ref, wn_ref, wo_ref, g1_ref, b1_ref,
                  wrt_ref, x1_ref, x1t_ref, sct_ref):
    wkv = wf_ref[...] + wb_ref[...]
    xc = wkv - _mm(wkv.astype(BF16), havg_ref[...])
    var = _mm((xc * xc).astype(BF16), havg_ref[...])
    ya = (xc * lax.rsqrt(var + GN_EPS) * lg_ref[...] + lb_ref[...] + bonus_ref[...]) * gate_ref[...]
    up_a = _mm(ya.astype(BF16), wa_ref[...])
    up_n = _mm(yn_ref[...], wn_ref[...])
    merged = (jax.nn.sigmoid(ga_ref[...].astype(F32)) * up_a
              + jax.nn.sigmoid(gn_ref[...].astype(F32)) * up_n)
    mix = _mm(merged.astype(BF16), wo_ref[...])
    x1 = _ln(ALPHA * xn_ref[...] + mix, g1_ref[...], b1_ref[...])
    x1_ref[...] = x1
    _to_token_tiles(x1t_ref, x1)
    sct_ref[...] = jax.nn.sigmoid(_mm_nt(wrt_ref[...], x1, HI))


def _merge(wkv_f, wkv_b, bonus, gate, lnx_g, lnx_b, yn, ga, gn, xn, w_up_a, w_up_n, w_out, ln1_g, ln1_b,
           w_router_t, tm):
    t = xn.shape[0]
    row = lambda n: pl.BlockSpec((tm, n), lambda i: (i, 0))
    full = lambda a: pl.BlockSpec(a.shape, lambda i: (0, 0))
    havg = _head_sum_matrix(1.0 / HEAD_DIM)
    return pl.pallas_call(
        _merge_kernel,
        grid=(t // tm,),
        in_specs=[row(D_RWKV), row(D_RWKV), row(D_RWKV), row(D_RWKV), full(havg), full(lnx_g), full(lnx_b),
                  row(D_NA), row(D_MODEL), row(D_MODEL), row(D_MODEL),
                  full(w_up_a), full(w_up_n), full(w_out), full(ln1_g), full(ln1_b), full(w_router_t)],
        out_specs=[row(D_MODEL), pl.BlockSpec((tm * SLABS, LANES), lambda i: (i, 0)),
                   pl.BlockSpec((N_EXPERTS, tm), lambda i: (0, i))],
        out_shape=[jax.ShapeDtypeStruct((t, D_MODEL), F32),
                   jax.ShapeDtypeStruct((t * SLABS, LANES), F32),
                   jax.ShapeDtypeStruct((N_EXPERTS, t), F32)],
        compiler_params=pltpu.CompilerParams(dimension_semantics=("parallel",),
                                             vmem_limit_bytes=VMEM_LIMIT),
        name="merge_ln1_router",
    )(wkv_f, wkv_b, bonus, gate, havg, lnx_g, lnx_b, yn, ga, gn, xn, w_up_a, w_up_n, w_out, ln1_g, ln1_b,
      w_router_t)


def _first_argmax(vals, iota):
    m = jnp.max(vals, axis=0, keepdims=True)
    first = jnp.min(jnp.where(vals == m, iota, float(vals.shape[0])), axis=0, keepdims=True)
    return m, first


def _route_kernel(sct_ref, bias_ref, idx_ref, wts_ref, cnt_ref):
    s = sct_ref[...]
    tm = s.shape[1]
    sel = s + bias_ref[...]
    gsz = N_EXPERTS // N_GROUPS
    iota_g = lax.broadcasted_iota(jnp.int32, (gsz, tm), 0).astype(F32)
    iota_8 = lax.broadcasted_iota(jnp.int32, (N_GROUPS, tm), 0).astype(F32)
    iota_e = lax.broadcasted_iota(jnp.int32, (N_EXPERTS, tm), 0).astype(F32)

    gs = jnp.zeros((N_GROUPS, tm), F32)
    for g in range(N_GROUPS):
        blk = sel[g * gsz:(g + 1) * gsz, :]
        m1, i1 = _first_argmax(blk, iota_g)
        m2 = jnp.max(jnp.where(iota_g == i1, -jnp.inf, blk), axis=0, keepdims=True)
        gs = jnp.where(iota_8 == float(g), m1 + m2, gs)

    chosen = jnp.zeros((N_GROUPS, tm), F32)
    for _ in range(TOPK_GROUPS):
        _, gi = _first_argmax(gs, iota_8)
        hit = iota_8 == gi
        chosen = jnp.where(hit, 1.0, chosen)
        gs = jnp.where(hit, -jnp.inf, gs)
    mask_e = jnp.concatenate([jnp.broadcast_to(chosen[g:g + 1, :], (gsz, tm)) for g in range(N_GROUPS)],
                             axis=0)
    cand = jnp.where(mask_e > 0.0, sel, -jnp.inf)

    iota_k = lax.broadcasted_iota(jnp.int32, (TOP_K, tm), 0)
    idx = jnp.zeros((TOP_K, tm), F32)
    wts = jnp.zeros((TOP_K, tm), F32)
    member = jnp.zeros((N_EXPERTS, tm), F32)
    for j in range(TOP_K):
        _, ij = _first_argmax(cand, iota_e)
        hit = iota_e == ij
        wj = jnp.sum(jnp.where(hit, s, 0.0), axis=0, keepdims=True)
        cand = jnp.where(hit, -jnp.inf, cand)
        member = jnp.where(hit, 1.0, member)
        idx = jnp.where(iota_k == j, ij, idx)
        wts = jnp.where(iota_k == j, wj, wts)
    wts = wts / jnp.sum(wts, axis=0, keepdims=True) * ROUTED_SCALE
    idx_ref[...] = idx.astype(jnp.int32)
    wts_ref[...] = wts
    cnt_ref[0] = _mm_nt(jnp.ones((8, tm), BF16), member.astype(BF16))


def _route(sct, e_bias, tm):
    n_e, t = sct.shape
    nt = t // tm
    return pl.pallas_call(
        _route_kernel,
        grid=(nt,),
        in_specs=[pl.BlockSpec((n_e, tm), lambda i: (0, i)),
                  pl.BlockSpec((n_e, 1), lambda i: (0, 0))],
        out_specs=[pl.BlockSpec((TOP_K, tm), lambda i: (0, i)),
                   pl.BlockSpec((TOP_K, tm), lambda i: (0, i)),
                   pl.BlockSpec((1, 8, n_e), lambda i: (i, 0, 0))],
        out_shape=[jax.ShapeDtypeStruct((TOP_K, t), jnp.int32),
                   jax.ShapeDtypeStruct((TOP_K, t), F32),
                   jax.ShapeDtypeStruct((nt, 8, n_e), F32)],
        compiler_params=pltpu.CompilerParams(dimension_semantics=("parallel",),
                                             vmem_limit_bytes=VMEM_LIMIT),
        name="moe_route",
    )(sct, e_bias.reshape(n_e, 1).astype(F32))


def _dest_kernel(idx_ref, base_ref, dest_ref):
    idx = idx_ref[...].astype(F32)
    tm = idx.shape[1]
    iota_e = lax.broadcasted_iota(jnp.int32, (N_EXPERTS, tm), 0).astype(F32)
    hits = [iota_e == idx[j:j + 1, :] for j in range(TOP_K)]
    member = jnp.zeros((N_EXPERTS, tm), F32)
    for hit in hits:
        member = jnp.where(hit, 1.0, member)
    r = lax.broadcasted_iota(jnp.int32, (tm, tm), 0)
    c = lax.broadcasted_iota(jnp.int32, (tm, tm), 1)
    earlier = jnp.where(r < c, 1.0, 0.0).astype(BF16)
    rank = _mm(member.astype(BF16), earlier) + base_ref[0]
    iota_k = lax.broadcasted_iota(jnp.int32, (TOP_K, tm), 0)
    dest = jnp.zeros((TOP_K, tm), F32)
    for j, hit in enumerate(hits):
        dj = jnp.sum(jnp.where(hit, rank, 0.0), axis=0, keepdims=True)
        dest = jnp.where(iota_k == j, dj, dest)
    dest_ref[0] = dest.astype(jnp.int32)


def _dest_rows(idx, base, tm):
    t = idx.shape[1]
    nt = t // tm
    return pl.pallas_call(
        _dest_kernel,
        grid=(nt,),
        in_specs=[pl.BlockSpec((TOP_K, tm), lambda i: (0, i)),
                  pl.BlockSpec((1, N_EXPERTS, 1), lambda i: (i, 0, 0))],
        out_specs=pl.BlockSpec((1, TOP_K, tm), lambda i: (i, 0, 0)),
        out_shape=jax.ShapeDtypeStruct((nt, TOP_K, tm), jnp.int32),
        compiler_params=pltpu.CompilerParams(dimension_semantics=("parallel",),
                                             vmem_limit_bytes=VMEM_LIMIT),
        name="moe_dest",
    )(idx, base)


def _block_plan(tile_counts, n_tokens):
    n_blocks = (n_tokens * TOP_K + N_EXPERTS * (MOE_ROWS - 1) + MOE_ROWS - 1) // MOE_ROWS
    n_blocks = -(-n_blocks // MOE_GROUP) * MOE_GROUP
    counts = jnp.sum(tile_counts, axis=0)
    padded = (counts + MOE_ROWS - 1) // MOE_ROWS * MOE_ROWS
    extra = (-(jnp.sum(padded) // MOE_ROWS)) % MOE_GROUP
    padded = padded.at[N_EXPERTS - 1].add(extra * MOE_ROWS)
    pend = jnp.cumsum(padded)
    pstart = pend - padded
    tile_base = pstart[None, :] + jnp.cumsum(tile_counts, axis=0) - tile_counts
    block_start = jnp.arange(n_blocks, dtype=jnp.int32) * MOE_ROWS
    block_e = jnp.minimum(jnp.sum(pend[None, :] <= block_start[:, None], axis=1),
                          N_EXPERTS - 1).astype(jnp.int32)
    n_used = (pend[-1] // MOE_ROWS).astype(jnp.int32).reshape(1)
    return (tile_base.astype(F32)[:, :, None], counts.astype(jnp.int32), padded.astype(jnp.int32),
            pstart.astype(jnp.int32), block_e, n_used, n_blocks)


def _row_tile(ref, row):
    return ref.at[pl.ds(pl.multiple_of(row * SLABS, SLABS), SLABS)]


def _dispatch_kernel(cnt_ref, pad_ref, pst_ref, dest_hbm, x_ref, xs_hbm, dest_s, zrow, isem, csem, zsem,
                     *, tm):
    i = pl.program_id(0)
    n = pl.num_programs(0)
    cur = i % 2

    def idx_copy(tile, buf):
        return pltpu.make_async_copy(dest_hbm.at[tile], dest_s.at[buf], isem.at[buf])

    @pl.when(i == 0)
    def _():
        idx_copy(0, 0).start()
        zrow[...] = jnp.zeros_like(zrow)

        def per_expert(e, carry, wait):
            def per_row(r, c2):
                cp = pltpu.make_async_copy(zrow, _row_tile(xs_hbm, pst_ref[e] + r), zsem)
                if wait:
                    cp.wait()
                else:
                    cp.start()
                return c2
            return lax.fori_loop(cnt_ref[e], pad_ref[e], per_row, carry)

        lax.fori_loop(0, N_EXPERTS, functools.partial(per_expert, wait=False), 0)
        lax.fori_loop(0, N_EXPERTS, functools.partial(per_expert, wait=True), 0)

    idx_copy(i, cur).wait()

    @pl.when(i + 1 < n)
    def _():
        idx_copy(i + 1, 1 - cur).start()

    def per_token(t, carry):
        src = x_ref.at[pl.ds(pl.multiple_of(t * SLABS, SLABS), SLABS)]
        for j in range(TOP_K):
            pltpu.make_async_copy(src, _row_tile(xs_hbm, dest_s[cur, j, t]), csem).start()
        return carry

    lax.fori_loop(0, tm, per_token, 0)
    for j in range(TOP_K):
        pltpu.make_async_copy(x_ref, xs_hbm.at[pl.ds(0, tm * SLABS)], csem).wait()


def _dispatch(x1t, dest, counts, padded, pstart, n_blocks, tm):
    nt = dest.shape[0]
    any_spec = pl.BlockSpec(memory_space=pl.ANY)
    grid_spec = pltpu.PrefetchScalarGridSpec(
        num_scalar_prefetch=3,
        grid=(nt,),
        in_specs=[any_spec, pl.BlockSpec((tm * SLABS, LANES), lambda i, c, p, s: (i, 0))],
        out_specs=any_spec,
        scratch_shapes=[pltpu.SMEM((2, TOP_K, tm), jnp.int32),
                        pltpu.VMEM((SLABS, LANES), F32),
                        pltpu.SemaphoreType.DMA((2,)),
                        pltpu.SemaphoreType.DMA(()),
                        pltpu.SemaphoreType.DMA(())])
    return pl.pallas_call(
        functools.partial(_dispatch_kernel, tm=tm),
        grid_spec=grid_spec,
        out_shape=jax.ShapeDtypeStruct((n_blocks * MOE_ROWS * SLABS, LANES), F32),
        compiler_params=pltpu.CompilerParams(dimension_semantics=("arbitrary",),
                                             vmem_limit_bytes=VMEM_LIMIT),
        name="moe_dispatch",
    )(counts, padded, pstart, dest, x1t)


def _swiglu(xb, wg, wu, wd):
    hg = _mm(xb, wg)
    hu = _mm(xb, wu)
    return _mm((hg * jax.nn.sigmoid(hg) * hu).astype(BF16), wd)


def _experts_kernel(be_ref, nu_ref, xs_ref, *refs):
    w_refs, ys_ref = refs[:-1], refs[-1]
    s = pl.program_id(0)
    blk = MOE_ROWS * SLABS

    @pl.when(MOE_GROUP * s < nu_ref[0])
    def _():
        parts = [pl.ds(k * blk, blk) for k in range(MOE_GROUP)]
        wg = [w_refs[3 * k][0] for k in range(MOE_GROUP)]
        wu = [w_refs[3 * k + 1][0] for k in range(MOE_GROUP)]
        wd = [w_refs[3 * k + 2][0] for k in range(MOE_GROUP)]
        xb = [_from_token_tiles(xs_ref.at[p], MOE_ROWS).astype(BF16) for p in parts]
        hg = [_mm(x, w) for x, w in zip(xb, wg)]
        hu = [_mm(x, w) for x, w in zip(xb, wu)]
        hh = [(g * jax.nn.sigmoid(g) * u).astype(BF16) for g, u in zip(hg, hu)]
        out = [_mm(h, w) for h, w in zip(hh, wd)]
        for p, o in zip(parts, out):
            _to_token_tiles(ys_ref.at[p], o)

    @pl.when(MOE_GROUP * s >= nu_ref[0])
    def _():
        ys_ref[...] = jnp.zeros_like(ys_ref)


def _experts(xs, block_e, n_used, w_gate_e, w_up_e, w_down_e):
    n_groups = block_e.shape[0] // MOE_GROUP
    blkg = MOE_GROUP * MOE_ROWS * SLABS
    wspec = lambda shp, k: pl.BlockSpec((1,) + shp, lambda s, be, nu: (be[MOE_GROUP * s + k], 0, 0))
    wspecs = [wspec(shp, k) for k in range(MOE_GROUP)
              for shp in ((D_MODEL, D_EXPERT), (D_MODEL, D_EXPERT), (D_EXPERT, D_MODEL))]
    grid_spec = pltpu.PrefetchScalarGridSpec(
        num_scalar_prefetch=2,
        grid=(n_groups,),
        in_specs=[pl.BlockSpec((blkg, LANES),
                               lambda s, be, nu: (jnp.minimum(s, nu[0] // MOE_GROUP - 1), 0))] + wspecs,
        out_specs=pl.BlockSpec((blkg, LANES), lambda s, be, nu: (s, 0)))
    return pl.pallas_call(
        _experts_kernel,
        grid_spec=grid_spec,
        out_shape=jax.ShapeDtypeStruct(xs.shape, F32),
        compiler_params=pltpu.CompilerParams(dimension_semantics=("arbitrary",),
                                             vmem_limit_bytes=VMEM_LIMIT),
        name="moe_experts",
    )(block_e, n_used, xs, *([w_gate_e, w_up_e, w_down_e] * MOE_GROUP))


def _final_kernel(dest_hbm, ys_hbm, x1_ref, wts_ref, wg_ref, wu_ref, wd_ref, g2_ref, b2_ref, o_ref,
                  dest_s, gbuf, isem, gsem, *, tm):
    i = pl.program_id(0)
    n = pl.num_programs(0)
    cur = i % 2
    nxt = 1 - cur

    def idx_copy(tile, buf):
        return pltpu.make_async_copy(dest_hbm.at[tile], dest_s.at[buf], isem.at[buf])

    def start_gather(buf):
        def per_token(t, carry):
            for j in range(TOP_K):
                dst = gbuf.at[buf, j, pl.ds(pl.multiple_of(t * SLABS, SLABS), SLABS)]
                pltpu.make_async_copy(_row_tile(ys_hbm, dest_s[buf, j, t]), dst, gsem.at[buf]).start()
            return carry
        lax.fori_loop(0, tm, per_token, 0)

    @pl.when(i == 0)
    def _():
        idx_copy(0, 0).start()
        idx_copy(0, 0).wait()
        start_gather(0)

        @pl.when(1 < n)
        def _():
            idx_copy(1, 1).start()

    @pl.when(i + 1 < n)
    def _():
        idx_copy(i + 1, nxt).wait()
        start_gather(nxt)

        @pl.when(i + 2 < n)
        def _():
            idx_copy(i + 2, cur).start()

    x1 = x1_ref[...]
    y = _swiglu(x1.astype(BF16), wg_ref[...], wu_ref[...], wd_ref[...])
    r = lax.broadcasted_iota(jnp.int32, (tm, tm), 0)
    c = lax.broadcasted_iota(jnp.int32, (tm, tm), 1)
    w_cols = _mm_nt(jnp.where(r == c, 1.0, 0.0).astype(F32), wts_ref[...], HI)
    for j in range(TOP_K):
        pltpu.make_async_copy(ys_hbm.at[pl.ds(0, tm * SLABS)], gbuf.at[cur, j], gsem.at[cur]).wait()
    for j in range(TOP_K):
        y = y + w_cols[:, j:j + 1] * _from_token_tiles(gbuf.at[cur, j], tm)
    o_ref[...] = _ln(ALPHA * x1 + y, g2_ref[...], b2_ref[...])


def _final(dest, ys, x1, wts, w_gate_s, w_up_s, w_down_s, ln2_g, ln2_b, tm):
    t = x1.shape[0]
    any_spec = pl.BlockSpec(memory_space=pl.ANY)
    row = pl.BlockSpec((tm, D_MODEL), lambda i: (i, 0))
    full = lambda a: pl.BlockSpec(a.shape, lambda i: (0, 0))
    return pl.pallas_call(
        functools.partial(_final_kernel, tm=tm),
        grid=(t // tm,),
        in_specs=[any_spec, any_spec, row, pl.BlockSpec((TOP_K, tm), lambda i: (0, i)),
                  full(w_gate_s), full(w_up_s), full(w_down_s), full(ln2_g), full(ln2_b)],
        out_specs=row,
        out_shape=jax.ShapeDtypeStruct((t, D_MODEL), F32),
        scratch_shapes=[pltpu.SMEM((2, TOP_K, tm), jnp.int32),
                        pltpu.VMEM((2, TOP_K, tm * SLABS, LANES), F32),
                        pltpu.SemaphoreType.DMA((2,)),
                        pltpu.SemaphoreType.DMA((2,))],
        compiler_params=pltpu.CompilerParams(dimension_semantics=("arbitrary",),
                                             vmem_limit_bytes=VMEM_LIMIT),
        name="shared_combine_ln2",
    )(dest, ys, x1, wts, w_gate_s, w_up_s, w_down_s, ln2_g, ln2_b)


def _trunk(x, prm):
    b, l, d = x.shape
    t = b * l
    tm = 512
    row2 = lambda a: a.reshape(1, -1).astype(F32)
    xn, p_rwkv, q, k, v, ga, gn = _ln_proj(x.reshape(t, d), row2(prm['ln_in_g']), row2(prm['ln_in_b']),
                                           prm['w_in_rwkv'], prm['w_in_na'], prm['w_in_gate'], tm)
    wkv_f, wkv_b, bonus, gate = _rwkv_branch(p_rwkv, b, l, prm, tm)
    y_n = _na_attention(q.reshape(b, l, D_NA), k.reshape(b, l, D_NA), v.reshape(b, l, D_NA),
                        prm['na_bias'])
    x1, x1t, sct = _merge(wkv_f, wkv_b, bonus, gate, row2(prm['lnx_g']), row2(prm['lnx_b']),
                          y_n.reshape(t, D_NA), ga, gn, xn, prm['w_up_a'], prm['w_up_n'], prm['w_out'],
                          row2(prm['ln1_g']), row2(prm['ln1_b']), prm['w_router_t'], tm)
    return _moe(x1, x1t, sct, prm).reshape(b, l, d)


def _moe(x1, x1t, sct, prm):
    t = x1.shape[0]
    tm = MOE_TILE
    idx, wts, tile_cnt = _route(sct, prm['e_bias'], tm)
    tile_base, counts, padded, pstart, block_e, n_used, n_blocks = _block_plan(
        tile_cnt[:, 0, :].astype(jnp.int32), t)
    dest = _dest_rows(idx, tile_base, tm)
    xs = _dispatch(x1t, dest, counts, padded, pstart, n_blocks, tm)
    ys = _experts(xs, block_e, n_used, prm['w_gate_e'], prm['w_up_e'], prm['w_down_e'])
    row2 = lambda a: a.reshape(1, -1).astype(F32)
    return _final(dest, ys, x1, wts, prm['w_gate_s'], prm['w_up_s'], prm['w_down_s'],
                  row2(prm['ln2_g']), row2(prm['ln2_b']), tm)


def kernel(x_prompt, x_sample, ln_in_g, ln_in_b, w_in, mu_shift, w0, w2, a0, a2, g2, k_k, k_a, r_k,
           lnx_g, lnx_b, rpb, w_up_a, w_up_n, w_out, ln1_g, ln1_b, w_router, e_bias,
           w_gate_e, w_up_e, w_down_e, w_gate_s, w_up_s, w_down_s, ln2_g, ln2_b):
    assert w_in.shape[0] == DEPTH == 1
    w_in0 = w_in[0].astype(BF16)
    prm = dict(
        ln_in_g=ln_in_g, ln_in_b=ln_in_b,
        w_in_rwkv=w_in0[:, :RWKV_IN], w_in_na=w_in0[:, RWKV_IN:RWKV_IN + NA_IN],
        w_in_gate=w_in0[:, RWKV_IN + NA_IN:],
        mu_shift=mu_shift[0], w0=w0[0], w2=w2[0], a0=a0[0], a2=a2[0], g2=g2[0], k_k=k_k[0], k_a=k_a[0],
        r_k=r_k[0], lnx_g=lnx_g[0], lnx_b=lnx_b[0], na_bias=_na_bias_table(rpb[0]),
        w_up_a=w_up_a[0].astype(BF16), w_up_n=w_up_n[0].astype(BF16), w_out=w_out[0].astype(BF16),
        ln1_g=ln1_g[0], ln1_b=ln1_b[0], w_router_t=w_router[0].T, e_bias=e_bias[0],
        w_gate_e=w_gate_e[0].astype(BF16), w_up_e=w_up_e[0].astype(BF16),
        w_down_e=w_down_e[0].astype(BF16),
        w_gate_s=w_gate_s[0].astype(BF16), w_up_s=w_up_s[0].astype(BF16),
        w_down_s=w_down_s[0].astype(BF16), ln2_g=ln2_g[0], ln2_b=ln2_b[0])
    return (_trunk(x_prompt, prm), _trunk(x_sample, prm))
```

```python
import functools

import jax
import jax.numpy as jnp
import numpy as np
from jax import lax
from jax.experimental import pallas as pl
from jax.experimental.pallas import tpu as pltpu

F32 = jnp.float32
BF16 = jnp.bfloat16

D_MODEL = 1024
GRID_W = 64
HEAD_DIM = 64
D_RWKV = 512
D_NA = 512
NA_WIN_H = 8
NA_WIN_W = 16
DECAY_LORA = 64
ICL_LORA = 64
GATE_LORA = 128
N_EXPERTS = 256
TOP_K = 8
N_GROUPS = 8
TOPK_GROUPS = 4
D_EXPERT = 256
ROUTED_SCALE = 2.5
LN_EPS = 1e-5
GN_EPS = 64e-5
DEPTH = 1
ALPHA = (2 * DEPTH) ** 0.25
RWKV_SIZES = (D_RWKV, D_RWKV, D_RWKV, DECAY_LORA, DECAY_LORA, ICL_LORA, ICL_LORA, GATE_LORA)
RWKV_IN = sum(RWKV_SIZES)
NA_IN = 3 * D_NA

LANES = 128
HEADS_PER_SLAB = LANES // HEAD_DIM
SLABS = D_MODEL // LANES
VMEM_LIMIT = 48 * 1024 * 1024

CHUNK = 64
RWKV_SEQS_PER_STEP = 2
MOE_ROWS = 128
MOE_TILE = 256
MOE_GROUP = 4
NEG_BIG = -1e30

HI = lax.Precision.HIGHEST


def _mm(a, b, precision=None):
    return jnp.dot(a, b, preferred_element_type=F32, precision=precision)


def _mm_nt(a, b, precision=None):
    return lax.dot_general(a, b, (((1,), (1,)), ((), ())), preferred_element_type=F32,
                           precision=precision)


def _ln(x, g, b):
    mu = jnp.mean(x, -1, keepdims=True)
    xc = x - mu
    var = jnp.mean(xc * xc, -1, keepdims=True)
    return xc * lax.rsqrt(var + LN_EPS) * g + b


def _ln_proj_kernel(x_ref, g_ref, b_ref, wr_ref, wn_ref, wg_ref,
                    xn_ref, pr_ref, q_ref, k_ref, v_ref, ga_ref, gn_ref):
    xn = _ln(x_ref[...], g_ref[...], b_ref[...])
    xn_ref[...] = xn
    xb = xn.astype(BF16)
    pr_ref[...] = _mm(xb, wr_ref[...])
    pn = _mm(xb, wn_ref[...])
    q_ref[...] = pn[:, :D_NA].astype(BF16)
    k_ref[...] = pn[:, D_NA:2 * D_NA].astype(BF16)
    v_ref[...] = pn[:, 2 * D_NA:].astype(BF16)
    pg = _mm(xb, wg_ref[...])
    ga_ref[...] = pg[:, :D_MODEL].astype(BF16)
    gn_ref[...] = pg[:, D_MODEL:].astype(BF16)


def _ln_proj(x, g, b, w_rwkv, w_na, w_gate, tm):
    t = x.shape[0]
    row = lambda n: pl.BlockSpec((tm, n), lambda i: (i, 0))
    full = lambda a: pl.BlockSpec(a.shape, lambda i: (0, 0))
    return pl.pallas_call(
        _ln_proj_kernel,
        grid=(t // tm,),
        in_specs=[row(D_MODEL), full(g), full(b), full(w_rwkv), full(w_na), full(w_gate)],
        out_specs=[row(D_MODEL), row(RWKV_IN), row(D_NA), row(D_NA), row(D_NA),
                   row(D_MODEL), row(D_MODEL)],
        out_shape=[jax.ShapeDtypeStruct((t, D_MODEL), F32),
                   jax.ShapeDtypeStruct((t, RWKV_IN), F32),
                   jax.ShapeDtypeStruct((t, D_NA), BF16),
                   jax.ShapeDtypeStruct((t, D_NA), BF16),
                   jax.ShapeDtypeStruct((t, D_NA), BF16),
                   jax.ShapeDtypeStruct((t, D_MODEL), BF16),
                   jax.ShapeDtypeStruct((t, D_MODEL), BF16)],
        compiler_params=pltpu.CompilerParams(dimension_semantics=("parallel",),
                                             vmem_limit_bytes=VMEM_LIMIT),
        name="ln_proj",
    )(x, g, b, w_rwkv, w_na, w_gate)


def _rwkv_kernel(rf_ref, vf_ref, kkf_ref, kdf_ref, bef_ref, lwf_ref,
                 rb_ref, vb_ref, kkb_ref, kdb_ref, beb_ref, lwb_ref, yf_ref, yb_ref, z_ref, *, n_slabs, n_seq):
    @pl.when(pl.program_id(1) == 0)
    def _():
        z_ref[...] = jnp.zeros_like(z_ref)

    n2 = HEADS_PER_SLAB * CHUNK
    row = lax.broadcasted_iota(jnp.int32, (n2, n2), 0)
    col = lax.broadcasted_iota(jnp.int32, (n2, n2), 1)
    dts = (row & (CHUNK - 1)) - (col & (CHUNK - 1))
    blk16 = ((row & (CHUNK - 1)) >> 4) == ((col & (CHUNK - 1)) >> 4)
    eye = row == col
    rowi = lax.broadcasted_iota(jnp.int32, (CHUNK, LANES), 0)
    lane = lax.broadcasted_iota(jnp.int32, (CHUNK, LANES), 1)
    head0 = lane < HEAD_DIM

    def bd(x):
        return jnp.concatenate([jnp.where(head0, x, 0.0), jnp.where(head0, 0.0, x)], axis=0)

    def bf(x):
        return x.astype(BF16)

    chains = [(d, (q, p)) for q in range(n_seq) for d in range(2) for p in range(n_slabs)]
    sls = [(q, slice(p * LANES, (p + 1) * LANES)) for _, (q, p) in chains]
    strict = [dts > 0 if d == 0 else dts < 0 for d, _ in chains]
    incl = [dts >= 0 if d == 0 else dts <= 0 for d, _ in chains]
    r_in = [(rf_ref, rb_ref)[d] for d, _ in chains]
    v_in = [(vf_ref, vb_ref)[d] for d, _ in chains]
    kk_in = [(kkf_ref, kkb_ref)[d] for d, _ in chains]
    kd_in = [(kdf_ref, kdb_ref)[d] for d, _ in chains]
    be_in = [(bef_ref, beb_ref)[d] for d, _ in chains]
    lw_in = [(lwf_ref, lwb_ref)[d] for d, _ in chains]
    ident = jnp.where(eye, 1.0, 0.0).astype(F32)

    def each(f, *cols):
        return [f(*args) for args in zip(*cols)]

    lw = each(lambda ref, sl: ref[0, sl[0], :, sl[1]], lw_in, sls)
    def scan_rows(x, reverse):
        step = 1
        while step < CHUNK:
            if reverse:
                x = x + jnp.where(rowi < CHUNK - step, pltpu.roll(x, CHUNK - step, axis=0), 0.0)
            else:
                x = x + jnp.where(rowi >= step, pltpu.roll(x, step, axis=0), 0.0)
            step *= 2
        return x

    cum = [scan_rows(x, d == 1) for x, (d, _) in zip(lw, chains)]
    tot = each(lambda x: jnp.sum(x, axis=0, keepdims=True), lw)
    e_in = each(jnp.exp, cum)
    e_ex = each(lambda c_, l_: jnp.exp(c_ - l_), cum, lw)
    e_neg = each(lambda c_: jnp.exp(-c_), cum)
    e_rem = each(lambda t_, c_: jnp.exp(t_ - c_), tot, cum)
    gam = each(jnp.exp, tot)
    kk = each(lambda ref, sl: ref[sl[0], :, sl[1]].astype(F32), kk_in, sls)
    kd = each(lambda ref, sl: ref[0, sl[0], :, sl[1]].astype(F32), kd_in, sls)
    be = each(lambda ref, sl: ref[0, sl[0], :, sl[1]].astype(F32), be_in, sls)
    a_b = each(lambda x, e: bf(bd(x * e)), kk, e_ex)
    r_t = each(lambda ref, e, sl: bd(ref[sl[0], :, sl[1]].astype(F32) * e), r_in, e_in, sls)
    b_t = each(lambda x, e: bd(x * e), be, e_neg)
    k_t = each(lambda x, e: bd(x * e), kd, e_neg)
    b_h = each(lambda x, e: bd(x * e), be, e_rem)
    k_h = each(lambda x, e: bd(x * e), kd, e_rem)
    vv = each(lambda ref, sl: bf(bd(ref[sl[0], :, sl[1]].astype(F32))), v_in, sls)

    sc = each(lambda a, r, b, k: _mm_nt(jnp.concatenate([a, bf(r)], axis=0),
                                        bf(jnp.concatenate([b, k], axis=0))), a_b, r_t, b_t, k_t)
    lk = each(lambda m, s: bf(jnp.where(m, s[:n2, n2:], 0.0)), strict, sc)
    mb = each(lambda m, s: bf(jnp.where(m, s[n2:, :n2], 0.0)), incl, sc)
    mk = each(lambda m, s: bf(jnp.where(m, s[n2:, n2:], 0.0)), incl, sc)

    nn = each(lambda m, s: jnp.where(m, -s[:n2, :n2], 0.0), strict, sc)
    dg = each(lambda x: jnp.where(blk16, x, 0.0), nn)
    offb = each(lambda x, d_: bf(x - d_), nn, dg)
    def mm2(xs, ys):
        return each(_mm, xs, ys)

    d1b = each(bf, dg)
    d2b = each(bf, mm2(d1b, d1b))
    d4b = each(bf, mm2(d2b, d2b))
    d8b = each(bf, mm2(d4b, d4b))
    td = each(lambda d_: ident + d_, dg)
    for dpow in (d2b, d4b, d8b):
        td = each(lambda t_, m_: t_ + m_, td, mm2(each(bf, td), dpow))
    tdb = each(bf, td)
    e1 = mm2(tdb, offb)
    e1b = each(bf, e1)
    e2 = mm2(e1b, e1b)
    e3 = mm2(e1b, each(bf, e2))
    ttb = each(bf, mm2(each(lambda x1_, x2_, x3_: bf(ident + x1_ + x2_ + x3_), e1, e2, e3), tdb))

    lkv = mm2(lk, vv)
    wb = each(lambda t_, a, l_: bf(_mm(t_, jnp.concatenate([a, bf(l_)], axis=1))), ttb, a_b, lkv)
    mw = each(_mm, mb, wb)
    mkv = mm2(mk, vv)
    bw = each(lambda b, w: _mm(bf(b.T), w), b_h, wb)
    kv = mm2(each(lambda k: bf(k.T), k_h), vv)
    r_hat = each(lambda r, m: bf(r - m[:, :n2]), r_t, mw)
    y_loc = each(lambda m, w: m - w[:, n2:], mkv, mw)
    g = each(lambda g_, b: bf(jnp.where(eye, jnp.broadcast_to(g_, (n2, n2)), 0.0) - b[:, :n2]), gam, bw)
    h = each(lambda k, b: k - b[:, n2:], kv, bw)

    n_ch = len(chains)
    z = [z_ref[i] for i in range(n_ch)]
    z_hi = each(bf, z)
    z_lo = each(lambda z_, zh: bf(z_ - zh.astype(F32)), z, z_hi)
    y = each(lambda m_, yl: m_ + yl, mm2(r_hat, z_hi), y_loc)
    gz = each(lambda g_, zh, zl: _mm(g_, jnp.concatenate([zh, zl], axis=1)), g, z_hi, z_lo)
    z_new = each(lambda m_, h_: m_[:, :n2] + m_[:, n2:] + h_, gz, h)
    for i, (d, _) in enumerate(chains):
        z_ref[i] = z_new[i]
        (yf_ref, yb_ref)[d][sls[i][0], :, sls[i][1]] = y[i][:CHUNK] + y[i][CHUNK:]


def _rwkv_scan(r, v, kk, kd, be, lw):
    b, l, dr = r.shape
    nc = l // CHUNK
    n_slabs = dr // LANES
    n_seq = RWKV_SEQS_PER_STEP if b % RWKV_SEQS_PER_STEP == 0 else 1
    fwd = pl.BlockSpec((n_seq, CHUNK, dr), lambda bi, c: (bi, c, 0))
    bwd = pl.BlockSpec((n_seq, CHUNK, dr), lambda bi, c: (bi, nc - 1 - c, 0))
    fwd_d = pl.BlockSpec((1, n_seq, CHUNK, dr), lambda bi, c: (0, bi, c, 0))
    bwd_d = pl.BlockSpec((1, n_seq, CHUNK, dr), lambda bi, c: (1, bi, nc - 1 - c, 0))
    return pl.pallas_call(
        functools.partial(_rwkv_kernel, n_slabs=n_slabs, n_seq=n_seq),
        grid=(b // n_seq, nc),
        in_specs=[fwd, fwd, fwd, fwd_d, fwd_d, fwd_d, bwd, bwd, bwd, bwd_d, bwd_d, bwd_d],
        out_specs=[fwd, bwd],
        out_shape=[jax.ShapeDtypeStruct((b, l, dr), F32), jax.ShapeDtypeStruct((b, l, dr), F32)],
        scratch_shapes=[pltpu.VMEM((2 * n_slabs * n_seq, HEADS_PER_SLAB * CHUNK, HEADS_PER_SLAB * HEAD_DIM),
                                   F32)],
        compiler_params=pltpu.CompilerParams(
            dimension_semantics=("parallel", "arbitrary"),
            vmem_limit_bytes=VMEM_LIMIT),
        name="rwkv_scan",
    )(r, v, kk, kd, be, lw, r, v, kk, kd, be, lw)


def _rwkv_prep_kernel(p_ref, prev_ref, next_ref, mu_ref, wdec_ref, w0_ref, wicl_ref, a0_ref, g2_ref,
                      kk_w_ref, ka_ref, rk_ref, hsum_ref,
                      r_ref, v_ref, kk_ref, kd_ref, be_ref, lw_ref, bonus_ref, gate_ref, *, tiles_per_seq):
    i = pl.program_id(0)
    p = p_ref[...]
    tm = p.shape[0]
    pos = i % tiles_per_seq
    prev_row = jnp.where(pos == 0, 0.0, prev_ref[7:8, :])
    next_row = jnp.where(pos == tiles_per_seq - 1, 0.0, next_ref[0:1, :])
    rowi = lax.broadcasted_iota(jnp.int32, (tm, 1), 0)
    up = jnp.where(rowi == 0, prev_row, pltpu.roll(p, 1, axis=0))
    dn = jnp.where(rowi == tm - 1, next_row, pltpu.roll(p, tm - 1, axis=0))
    ps = p + mu_ref[...] * (0.5 * (up + dn) - p)

    c = np.cumsum((0,) + RWKV_SIZES)
    xr, xk, xv = ps[:, c[0]:c[1]], ps[:, c[1]:c[2]], ps[:, c[2]:c[3]]
    dw, da, dg = ps[:, c[3]:c[5]], ps[:, c[5]:c[7]], ps[:, c[7]:c[8]]
    wl = w0_ref[...] + _mm(jnp.tanh(dw).astype(BF16), wdec_ref[...])
    neg = -wl
    softplus = jnp.maximum(neg, 0.0) + jnp.log(1.0 + jnp.exp(-jnp.abs(neg)))
    lw = -jnp.exp(-softplus - 0.5)
    a = jax.nn.sigmoid(a0_ref[...] + _mm(da.astype(BF16), wicl_ref[...]))
    kkr = xk * kk_w_ref[...]
    ss = _mm((kkr * kkr).astype(BF16), hsum_ref[...])
    kk = kkr * lax.rsqrt(jnp.maximum(ss, 1e-24))
    kd_sum = jnp.zeros_like(xk)
    for d in range(2):
        a_d = a[:, d * D_RWKV:(d + 1) * D_RWKV]
        kd = xk * (1.0 + (a_d - 1.0) * ka_ref[...])
        kd_sum = kd_sum + kd
        kd_ref[d] = kd.astype(BF16)
        be_ref[d] = (a_d * kk).astype(BF16)
        lw_ref[d] = lw[:, d * D_RWKV:(d + 1) * D_RWKV]
    r_ref[...] = xr.astype(BF16)
    v_ref[...] = xv.astype(BF16)
    kk_ref[...] = kk.astype(BF16)
    bonus_ref[...] = _mm((xr * rk_ref[...] * kd_sum).astype(BF16), hsum_ref[...]) * xv
    gate_ref[...] = _mm(jax.nn.sigmoid(dg).astype(BF16), g2_ref[...])


def _rwkv_branch(p_rwkv, b, l, prm, tm):
    t = b * l
    r, v, kk, kd, be, lw, bonus, gate = _rwkv_prep(p_rwkv, l, prm, tm)
    seq = lambda a: a.reshape(b, l, D_RWKV)
    seq2 = lambda a: a.reshape(2, b, l, D_RWKV)
    wkv_f, wkv_b = _rwkv_scan(seq(r), seq(v), seq(kk), seq2(kd), seq2(be), seq2(lw))
    return wkv_f.reshape(t, D_RWKV), wkv_b.reshape(t, D_RWKV), bonus, gate


def _block_diag2(w):
    z = jnp.zeros_like(w[0])
    return jnp.concatenate([jnp.concatenate([w[0], z], axis=1), jnp.concatenate([z, w[1]], axis=1)], axis=0)


def _head_sum_matrix(scale):
    h = np.arange(D_RWKV) // HEAD_DIM
    return jnp.asarray((h[:, None] == h[None, :]) * scale, BF16)


def _rwkv_prep(p_rwkv, seq_len, prm, tm):
    t = p_rwkv.shape[0]
    row2 = lambda a: a.reshape(1, -1).astype(F32)
    consts = [row2(prm['mu_shift']), _block_diag2(prm['w2']).astype(BF16), row2(prm['w0']),
              _block_diag2(prm['a2']).astype(BF16), row2(prm['a0']), prm['g2'].astype(BF16),
              row2(prm['k_k']), row2(prm['k_a']), row2(prm['r_k']), _head_sum_matrix(1.0)]
    full = lambda a: pl.BlockSpec(a.shape, lambda i: (0, 0))
    row = pl.BlockSpec((tm, D_RWKV), lambda i: (i, 0))
    row_d = pl.BlockSpec((2, tm, D_RWKV), lambda i: (0, i, 0))
    halo = tm // 8
    sd = lambda dt: jax.ShapeDtypeStruct((t, D_RWKV), dt)
    sd2 = lambda dt: jax.ShapeDtypeStruct((2, t, D_RWKV), dt)
    return pl.pallas_call(
        functools.partial(_rwkv_prep_kernel, tiles_per_seq=seq_len // tm),
        grid=(t // tm,),
        in_specs=[pl.BlockSpec((tm, RWKV_IN), lambda i: (i, 0)),
                  pl.BlockSpec((8, RWKV_IN), lambda i: (jnp.maximum(i * halo - 1, 0), 0)),
                  pl.BlockSpec((8, RWKV_IN), lambda i: (jnp.minimum((i + 1) * halo, t // 8 - 1), 0))]
        + [full(a) for a in consts],
        out_specs=[row, row, row, row_d, row_d, row_d, row, row],
        out_shape=[sd(BF16), sd(BF16), sd(BF16), sd2(BF16), sd2(BF16), sd2(F32), sd(F32), sd(F32)],
        compiler_params=pltpu.CompilerParams(dimension_semantics=("parallel",),
                                             vmem_limit_bytes=VMEM_LIMIT),
        name="rwkv_prep",
    )(p_rwkv, p_rwkv, p_rwkv, *consts)


def _na_bias_table(rpb):
    n_h = rpb.shape[0]
    var = np.arange(NA_WIN_H)[:, None]
    i = np.arange(NA_WIN_H)[None, :]
    c = np.arange(GRID_W)[:, None]
    kc = np.arange(GRID_W)[None, :]
    cs = np.clip(c - NA_WIN_W // 2, 0, GRID_W - NA_WIN_W)
    valid = (kc >= cs) & (kc < cs + NA_WIN_W)
    row_sel = (np.arange(2 * NA_WIN_H - 1)[None, None, :] == (i - var + NA_WIN_H - 1)[:, :, None])
    col_sel = (np.arange(2 * NA_WIN_W - 1)[None, None, :] == (kc - c + NA_WIN_W - 1)[:, :, None])
    col_sel = col_sel & valid[:, :, None]
    tab = jnp.einsum('hab,via,ckb->vhcik', rpb.astype(F32), row_sel.astype(np.float32),
                     col_sel.astype(np.float32), precision=HI)
    tab = jnp.where(valid[None, None, :, None, :], tab, NEG_BIG)
    return tab.reshape(NA_WIN_H, n_h * GRID_W, NA_WIN_H * GRID_W)


def _na_kernel(q_ref, k_ref, v_ref, bias_ref, o_ref, *, rows, n_slabs):
    r = pl.program_id(1)
    rs = jnp.clip(r - NA_WIN_H // 2, 0, rows - NA_WIN_H)
    start = pl.multiple_of(rs * GRID_W, GRID_W)
    band = NA_WIN_H * GRID_W
    scale = HEAD_DIM ** -0.5
    lane = lax.broadcasted_iota(jnp.int32, (GRID_W, LANES), 1)
    head0 = lane < HEAD_DIM
    sls = [slice(p * LANES, (p + 1) * LANES) for p in range(n_slabs)]

    def split_heads(q2):
        zero = jnp.zeros_like(q2)
        return jnp.concatenate([jnp.where(head0, q2, zero), jnp.where(head0, zero, q2)], axis=0)

    lhs = [split_heads(q_ref[0, :, sl]) for sl in sls]
    s = [_mm_nt(x, k_ref[0, pl.ds(start, band), sl]) for x, sl in zip(lhs, sls)]
    s = [x * scale + bias_ref[0, p * 2 * GRID_W:(p + 1) * 2 * GRID_W, :] for p, x in enumerate(s)]
    m = [jnp.max(x, axis=-1, keepdims=True) for x in s]
    e = [jnp.exp(x - mx) for x, mx in zip(s, m)]
    den = [jnp.sum(x, axis=-1, keepdims=True) for x in e]
    o = [_mm(x.astype(BF16), v_ref[0, pl.ds(start, band), sl]) for x, sl in zip(e, sls)]
    for x, d, sl in zip(o, den, sls):
        x = x / d
        o_ref[0, :, sl] = jnp.where(head0, x[:GRID_W], x[GRID_W:]).astype(o_ref.dtype)


def _na_attention(q, k, v, bias_tab):
    b, l, dn = q.shape
    rows = l // GRID_W
    assert rows >= NA_WIN_H
    n_slabs = dn // LANES

    def bias_idx(bi, r):
        rs = jnp.clip(r - NA_WIN_H // 2, 0, rows - NA_WIN_H)
        return (r - rs, 0, 0)

    return pl.pallas_call(
        functools.partial(_na_kernel, rows=rows, n_slabs=n_slabs),
        grid=(b, rows),
        in_specs=[pl.BlockSpec((1, GRID_W, dn), lambda bi, r: (bi, r, 0)),
                  pl.BlockSpec((1, l, dn), lambda bi, r: (bi, 0, 0)),
                  pl.BlockSpec((1, l, dn), lambda bi, r: (bi, 0, 0)),
                  pl.BlockSpec((1,) + bias_tab.shape[1:], bias_idx)],
        out_specs=pl.BlockSpec((1, GRID_W, dn), lambda bi, r: (bi, r, 0)),
        out_shape=jax.ShapeDtypeStruct((b, l, dn), BF16),
        compiler_params=pltpu.CompilerParams(dimension_semantics=("parallel", "arbitrary"),
                                             vmem_limit_bytes=VMEM_LIMIT),
        name="na_attention",
    )(q, k, v, bias_tab)


def _to_token_tiles(dst_ref, x):
    n = x.shape[0]
    for s in range(SLABS):
        dst_ref[pl.ds(s, n, stride=SLABS), :] = x[:, s * LANES:(s + 1) * LANES]


def _from_token_tiles(src_ref, n):
    return jnp.concatenate([src_ref[pl.ds(s, n, stride=SLABS), :] for s in range(SLABS)], axis=1)


def _merge_kernel(wf_ref, wb_ref, bonus_ref, gate_ref, havg_ref, lg_ref, lb_ref,
                  yn_ref, ga_ref, gn_ref, xn_ref, wa_ref, wn_ref, wo_ref, g1_ref, b1_ref,
                  wrt_ref, x1_ref, x1t_ref, sct_ref):
    wkv = wf_ref[...] + wb_ref[...]
    xc = wkv - _mm(wkv.astype(BF16), havg_ref[...])
    var = _mm((xc * xc).astype(BF16), havg_ref[...])
    ya = (xc * lax.rsqrt(var + GN_EPS) * lg_ref[...] + lb_ref[...] + bonus_ref[...]) * gate_ref[...]
    up_a = _mm(ya.astype(BF16), wa_ref[...])
    up_n = _mm(yn_ref[...], wn_ref[...])
    merged = (jax.nn.sigmoid(ga_ref[...].astype(F32)) * up_a
              + jax.nn.sigmoid(gn_ref[...].astype(F32)) * up_n)
    mix = _mm(merged.astype(BF16), wo_ref[...])
    x1 = _ln(ALPHA * xn_ref[...] + mix, g1_ref[...], b1_ref[...])
    x1_ref[...] = x1
    _to_token_tiles(x1t_ref, x1)
    sct_ref[...] = jax.nn.sigmoid(_mm_nt(wrt_ref[...], x1, HI))


def _merge(wkv_f, wkv_b, bonus, gate, lnx_g, lnx_b, yn, ga, gn, xn, w_up_a, w_up_n, w_out, ln1_g, ln1_b,
           w_router_t, tm):
    t = xn.shape[0]
    row = lambda n: pl.BlockSpec((tm, n), lambda i: (i, 0))
    full = lambda a: pl.BlockSpec(a.shape, lambda i: (0, 0))
    havg = _head_sum_matrix(1.0 / HEAD_DIM)
    return pl.pallas_call(
        _merge_kernel,
        grid=(t // tm,),
        in_specs=[row(D_RWKV), row(D_RWKV), row(D_RWKV), row(D_RWKV), full(havg), full(lnx_g), full(lnx_b),
                  row(D_NA), row(D_MODEL), row(D_MODEL), row(D_MODEL),
                  full(w_up_a), full(w_up_n), full(w_out), full(ln1_g), full(ln1_b), full(w_router_t)],
        out_specs=[row(D_MODEL), pl.BlockSpec((tm * SLABS, LANES), lambda i: (i, 0)),
                   pl.BlockSpec((N_EXPERTS, tm), lambda i: (0, i))],
        out_shape=[jax.ShapeDtypeStruct((t, D_MODEL), F32),
                   jax.ShapeDtypeStruct((t * SLABS, LANES), F32),
                   jax.ShapeDtypeStruct((N_EXPERTS, t), F32)],
        compiler_params=pltpu.CompilerParams(dimension_semantics=("parallel",),
                                             vmem_limit_bytes=VMEM_LIMIT),
        name="merge_ln1_router",
    )(wkv_f, wkv_b, bonus, gate, havg, lnx_g, lnx_b, yn, ga, gn, xn, w_up_a, w_up_n, w_out, ln1_g, ln1_b,
      w_router_t)


def _first_argmax(vals, iota):
    m = jnp.max(vals, axis=0, keepdims=True)
    first = jnp.min(jnp.where(vals == m, iota, float(vals.shape[0])), axis=0, keepdims=True)
    return m, first


def _route_kernel(sct_ref, bias_ref, idx_ref, wts_ref, cnt_ref):
    s = sct_ref[...]
    tm = s.shape[1]
    sel = s + bias_ref[...]
    gsz = N_EXPERTS // N_GROUPS
    iota_g = lax.broadcasted_iota(jnp.int32, (gsz, tm), 0).astype(F32)
    iota_8 = lax.broadcasted_iota(jnp.int32, (N_GROUPS, tm), 0).astype(F32)
    iota_e = lax.broadcasted_iota(jnp.int32, (N_EXPERTS, tm), 0).astype(F32)

    gs = jnp.zeros((N_GROUPS, tm), F32)
    for g in range(N_GROUPS):
        blk = sel[g * gsz:(g + 1) * gsz, :]
        m1, i1 = _first_argmax(blk, iota_g)
        m2 = jnp.max(jnp.where(iota_g == i1, -jnp.inf, blk), axis=0, keepdims=True)
        gs = jnp.where(iota_8 == float(g), m1 + m2, gs)

    chosen = jnp.zeros((N_GROUPS, tm), F32)
    for _ in range(TOPK_GROUPS):
        _, gi = _first_argmax(gs, iota_8)
        hit = iota_8 == gi
        chosen = jnp.where(hit, 1.0, chosen)
        gs = jnp.where(hit, -jnp.inf, gs)
    mask_e = jnp.concatenate([jnp.broadcast_to(chosen[g:g + 1, :], (gsz, tm)) for g in range(N_GROUPS)],
                             axis=0)
    cand = jnp.where(mask_e > 0.0, sel, -jnp.inf)

    iota_k = lax.broadcasted_iota(jnp.int32, (TOP_K, tm), 0)
    idx = jnp.zeros((TOP_K, tm), F32)
    wts = jnp.zeros((TOP_K, tm), F32)
    member = jnp.zeros((N_EXPERTS, tm), F32)
    for j in range(TOP_K):
        _, ij = _first_argmax(cand, iota_e)
        hit = iota_e == ij
        wj = jnp.sum(jnp.where(hit, s, 0.0), axis=0, keepdims=True)
        cand = jnp.where(hit, -jnp.inf, cand)
        member = jnp.where(hit, 1.0, member)
        idx = jnp.where(iota_k == j, ij, idx)
        wts = jnp.where(iota_k == j, wj, wts)
    wts = wts / jnp.sum(wts, axis=0, keepdims=True) * ROUTED_SCALE
    idx_ref[...] = idx.astype(jnp.int32)
    wts_ref[...] = wts
    cnt_ref[0] = _mm_nt(jnp.ones((8, tm), BF16), member.astype(BF16))


def _route(sct, e_bias, tm):
    n_e, t = sct.shape
    nt = t // tm
    return pl.pallas_call(
        _route_kernel,
        grid=(nt,),
        in_specs=[pl.BlockSpec((n_e, tm), lambda i: (0, i)),
                  pl.BlockSpec((n_e, 1), lambda i: (0, 0))],
        out_specs=[pl.BlockSpec((TOP_K, tm), lambda i: (0, i)),
                   pl.BlockSpec((TOP_K, tm), lambda i: (0, i)),
                   pl.BlockSpec((1, 8, n_e), lambda i: (i, 0, 0))],
        out_shape=[jax.ShapeDtypeStruct((TOP_K, t), jnp.int32),
                   jax.ShapeDtypeStruct((TOP_K, t), F32),
                   jax.ShapeDtypeStruct((nt, 8, n_e), F32)],
        compiler_params=pltpu.CompilerParams(dimension_semantics=("parallel",),
                                             vmem_limit_bytes=VMEM_LIMIT),
        name="moe_route",
    )(sct, e_bias.reshape(n_e, 1).astype(F32))


def _dest_kernel(idx_ref, base_ref, dest_ref):
    idx = idx_ref[...].astype(F32)
    tm = idx.shape[1]
    iota_e = lax.broadcasted_iota(jnp.int32, (N_EXPERTS, tm), 0).astype(F32)
    hits = [iota_e == idx[j:j + 1, :] for j in range(TOP_K)]
    member = jnp.zeros((N_EXPERTS, tm), F32)
    for hit in hits:
        member = jnp.where(hit, 1.0, member)
    r = lax.broadcasted_iota(jnp.int32, (tm, tm), 0)
    c = lax.broadcasted_iota(jnp.int32, (tm, tm), 1)
    earlier = jnp.where(r < c, 1.0, 0.0).astype(BF16)
    rank = _mm(member.astype(BF16), earlier) + base_ref[0]
    iota_k = lax.broadcasted_iota(jnp.int32, (TOP_K, tm), 0)
    dest = jnp.zeros((TOP_K, tm), F32)
    for j, hit in enumerate(hits):
        dj = jnp.sum(jnp.where(hit, rank, 0.0), axis=0, keepdims=True)
        dest = jnp.where(iota_k == j, dj, dest)
    dest_ref[0] = dest.astype(jnp.int32)


def _dest_rows(idx, base, tm):
    t = idx.shape[1]
    nt = t // tm
    return pl.pallas_call(
        _dest_kernel,
        grid=(nt,),
        in_specs=[pl.BlockSpec((TOP_K, tm), lambda i: (0, i)),
                  pl.BlockSpec((1, N_EXPERTS, 1), lambda i: (i, 0, 0))],
        out_specs=pl.BlockSpec((1, TOP_K, tm), lambda i: (i, 0, 0)),
        out_shape=jax.ShapeDtypeStruct((nt, TOP_K, tm), jnp.int32),
        compiler_params=pltpu.CompilerParams(dimension_semantics=("parallel",),
                                             vmem_limit_bytes=VMEM_LIMIT),
        name="moe_dest",
    )(idx, base)


def _block_plan(tile_counts, n_tokens):
    n_blocks = (n_tokens * TOP_K + N_EXPERTS * (MOE_ROWS - 1) + MOE_ROWS - 1) // MOE_ROWS
    n_blocks = -(-n_blocks // MOE_GROUP) * MOE_GROUP
    counts = jnp.sum(tile_counts, axis=0)
    padded = (counts + MOE_ROWS - 1) // MOE_ROWS * MOE_ROWS
    extra = (-(jnp.sum(padded) // MOE_ROWS)) % MOE_GROUP
    padded = padded.at[N_EXPERTS - 1].add(extra * MOE_ROWS)
    pend = jnp.cumsum(padded)
    pstart = pend - padded
    tile_base = pstart[None, :] + jnp.cumsum(tile_counts, axis=0) - tile_counts
    block_start = jnp.arange(n_blocks, dtype=jnp.int32) * MOE_ROWS
    block_e = jnp.minimum(jnp.sum(pend[None, :] <= block_start[:, None], axis=1),
                          N_EXPERTS - 1).astype(jnp.int32)
    n_used = (pend[-1] // MOE_ROWS).astype(jnp.int32).reshape(1)
    return (tile_base.astype(F32)[:, :, None], counts.astype(jnp.int32), padded.astype(jnp.int32),
            pstart.astype(jnp.int32), block_e, n_used, n_blocks)


def _row_tile(ref, row):
    return ref.at[pl.ds(pl.multiple_of(row * SLABS, SLABS), SLABS)]


def _dispatch_kernel(cnt_ref, pad_ref, pst_ref, dest_hbm, x_ref, xs_hbm, dest_s, zrow, isem, csem, zsem,
                     *, tm):
    i = pl.program_id(0)
    n = pl.num_programs(0)
    cur = i % 2

    def idx_copy(tile, buf):
        return pltpu.make_async_copy(dest_hbm.at[tile], dest_s.at[buf], isem.at[buf])

    @pl.when(i == 0)
    def _():
        idx_copy(0, 0).start()
        zrow[...] = jnp.zeros_like(zrow)

        def per_expert(e, carry, wait):
            def per_row(r, c2):
                cp = pltpu.make_async_copy(zrow, _row_tile(xs_hbm, pst_ref[e] + r), zsem)
                if wait:
                    cp.wait()
                else:
                    cp.start()
                return c2
            return lax.fori_loop(cnt_ref[e], pad_ref[e], per_row, carry)

        lax.fori_loop(0, N_EXPERTS, functools.partial(per_expert, wait=False), 0)
        lax.fori_loop(0, N_EXPERTS, functools.partial(per_expert, wait=True), 0)

    idx_copy(i, cur).wait()

    @pl.when(i + 1 < n)
    def _():
        idx_copy(i + 1, 1 - cur).start()

    def per_token(t, carry):
        src = x_ref.at[pl.ds(pl.multiple_of(t * SLABS, SLABS), SLABS)]
        for j in range(TOP_K):
            pltpu.make_async_copy(src, _row_tile(xs_hbm, dest_s[cur, j, t]), csem).start()
        return carry

    lax.fori_loop(0, tm, per_token, 0)
    for j in range(TOP_K):
        pltpu.make_async_copy(x_ref, xs_hbm.at[pl.ds(0, tm * SLABS)], csem).wait()


def _dispatch(x1t, dest, counts, padded, pstart, n_blocks, tm):
    nt = dest.shape[0]
    any_spec = pl.BlockSpec(memory_space=pl.ANY)
    grid_spec = pltpu.PrefetchScalarGridSpec(
        num_scalar_prefetch=3,
        grid=(nt,),
        in_specs=[any_spec, pl.BlockSpec((tm * SLABS, LANES), lambda i, c, p, s: (i, 0))],
        out_specs=any_spec,
        scratch_shapes=[pltpu.SMEM((2, TOP_K, tm), jnp.int32),
                        pltpu.VMEM((SLABS, LANES), F32),
                        pltpu.SemaphoreType.DMA((2,)),
                        pltpu.SemaphoreType.DMA(()),
                        pltpu.SemaphoreType.DMA(())])
    return pl.pallas_call(
        functools.partial(_dispatch_kernel, tm=tm),
        grid_spec=grid_spec,
        out_shape=jax.ShapeDtypeStruct((n_blocks * MOE_ROWS * SLABS, LANES), F32),
        compiler_params=pltpu.CompilerParams(dimension_semantics=("arbitrary",),
                                             vmem_limit_bytes=VMEM_LIMIT),
        name="moe_dispatch",
    )(counts, padded, pstart, dest, x1t)


def _swiglu(xb, wg, wu, wd):
    hg = _mm(xb, wg)
    hu = _mm(xb, wu)
    return _mm((hg * jax.nn.sigmoid(hg) * hu).astype(BF16), wd)


def _experts_kernel(be_ref, nu_ref, xs_ref, *refs):
    w_refs, ys_ref = refs[:-1], refs[-1]
    s = pl.program_id(0)
    blk = MOE_ROWS * SLABS

    @pl.when(MOE_GROUP * s < nu_ref[0])
    def _():
        parts = [pl.ds(k * blk, blk) for k in range(MOE_GROUP)]
        wgu = [w_refs[2 * k] for k in range(MOE_GROUP)]
        wd = [w_refs[2 * k + 1] for k in range(MOE_GROUP)]
        half = D_MODEL // 2
        xb, hh = {}, {}
        for step in range(MOE_GROUP + 2):
            k = step
            if k < MOE_GROUP:
                xb[k] = _from_token_tiles(xs_ref.at[parts[k]], MOE_ROWS).astype(BF16)
            k = step - 1
            if 0 <= k < MOE_GROUP:
                hgu = _mm(xb[k], wgu[k][0])
                hg, hu = hgu[:, :D_EXPERT], hgu[:, D_EXPERT:]
                hh[k] = (hg * jax.nn.sigmoid(hg) * hu).astype(BF16)
            k = step - 2
            if 0 <= k < MOE_GROUP:
                for c in range(2):
                    o = _mm(hh[k], wd[k][0, :, c * half:(c + 1) * half])
                    for s_ in range(SLABS // 2):
                        ys_ref.at[parts[k]][pl.ds(c * (SLABS // 2) + s_, MOE_ROWS, stride=SLABS), :] = (
                            o[:, s_ * LANES:(s_ + 1) * LANES])

    @pl.when(MOE_GROUP * s >= nu_ref[0])
    def _():
        ys_ref[...] = jnp.zeros_like(ys_ref)


def _experts(xs, block_e, n_used, w_gate_up_e, w_down_e):
    n_groups = block_e.shape[0] // MOE_GROUP
    blkg = MOE_GROUP * MOE_ROWS * SLABS
    wspec = lambda shp, k: pl.BlockSpec((1,) + shp, lambda s, be, nu: (be[MOE_GROUP * s + k], 0, 0))
    wspecs = [wspec(shp, k) for k in range(MOE_GROUP)
              for shp in ((D_MODEL, 2 * D_EXPERT), (D_EXPERT, D_MODEL))]
    grid_spec = pltpu.PrefetchScalarGridSpec(
        num_scalar_prefetch=2,
        grid=(n_groups,),
        in_specs=[pl.BlockSpec((blkg, LANES),
                               lambda s, be, nu: (jnp.minimum(s, nu[0] // MOE_GROUP - 1), 0))] + wspecs,
        out_specs=pl.BlockSpec((blkg, LANES), lambda s, be, nu: (s, 0)))
    return pl.pallas_call(
        _experts_kernel,
        grid_spec=grid_spec,
        out_shape=jax.ShapeDtypeStruct(xs.shape, F32),
        compiler_params=pltpu.CompilerParams(dimension_semantics=("arbitrary",),
                                             vmem_limit_bytes=VMEM_LIMIT),
        name="moe_experts",
    )(block_e, n_used, xs, *([w_gate_up_e, w_down_e] * MOE_GROUP))


def _final_kernel(dest_hbm, ys_hbm, x1_ref, wts_ref, wg_ref, wu_ref, wd_ref, g2_ref, b2_ref, o_ref,
                  dest_s, gbuf, isem, gsem, *, tm):
    i = pl.program_id(0)
    n = pl.num_programs(0)
    cur = i % 2
    nxt = 1 - cur

    def idx_copy(tile, buf):
        return pltpu.make_async_copy(dest_hbm.at[tile], dest_s.at[buf], isem.at[buf])

    def start_gather(buf):
        def per_token(t, carry):
            for j in range(TOP_K):
                dst = gbuf.at[buf, j, pl.ds(pl.multiple_of(t * SLABS, SLABS), SLABS)]
                pltpu.make_async_copy(_row_tile(ys_hbm, dest_s[buf, j, t]), dst, gsem.at[buf]).start()
            return carry
        lax.fori_loop(0, tm, per_token, 0)

    @pl.when(i == 0)
    def _():
        idx_copy(0, 0).start()
        idx_copy(0, 0).wait()
        start_gather(0)

        @pl.when(1 < n)
        def _():
            idx_copy(1, 1).start()

    @pl.when(i + 1 < n)
    def _():
        idx_copy(i + 1, nxt).wait()
        start_gather(nxt)

        @pl.when(i + 2 < n)
        def _():
            idx_copy(i + 2, cur).start()

    x1 = x1_ref[...]
    y = _swiglu(x1.astype(BF16), wg_ref[...], wu_ref[...], wd_ref[...])
    r = lax.broadcasted_iota(jnp.int32, (tm, tm), 0)
    c = lax.broadcasted_iota(jnp.int32, (tm, tm), 1)
    w_cols = _mm_nt(jnp.where(r == c, 1.0, 0.0).astype(F32), wts_ref[...], HI)
    for j in range(TOP_K):
        pltpu.make_async_copy(ys_hbm.at[pl.ds(0, tm * SLABS)], gbuf.at[cur, j], gsem.at[cur]).wait()
    for j in range(TOP_K):
        y = y + w_cols[:, j:j + 1] * _from_token_tiles(gbuf.at[cur, j], tm)
    o_ref[...] = _ln(ALPHA * x1 + y, g2_ref[...], b2_ref[...])


def _final(dest, ys, x1, wts, w_gate_s, w_up_s, w_down_s, ln2_g, ln2_b, tm):
    t = x1.shape[0]
    any_spec = pl.BlockSpec(memory_space=pl.ANY)
    row = pl.BlockSpec((tm, D_MODEL), lambda i: (i, 0))
    full = lambda a: pl.BlockSpec(a.shape, lambda i: (0, 0))
    return pl.pallas_call(
        functools.partial(_final_kernel, tm=tm),
        grid=(t // tm,),
        in_specs=[any_spec, any_spec, row, pl.BlockSpec((TOP_K, tm), lambda i: (0, i)),
                  full(w_gate_s), full(w_up_s), full(w_down_s), full(ln2_g), full(ln2_b)],
        out_specs=row,
        out_shape=jax.ShapeDtypeStruct((t, D_MODEL), F32),
        scratch_shapes=[pltpu.SMEM((2, TOP_K, tm), jnp.int32),
                        pltpu.VMEM((2, TOP_K, tm * SLABS, LANES), F32),
                        pltpu.SemaphoreType.DMA((2,)),
                        pltpu.SemaphoreType.DMA((2,))],
        compiler_params=pltpu.CompilerParams(dimension_semantics=("arbitrary",),
                                             vmem_limit_bytes=VMEM_LIMIT),
        name="shared_combine_ln2",
    )(dest, ys, x1, wts, w_gate_s, w_up_s, w_down_s, ln2_g, ln2_b)


def _trunk(x, prm):
    b, l, d = x.shape
    t = b * l
    tm = 512
    row2 = lambda a: a.reshape(1, -1).astype(F32)
    xn, p_rwkv, q, k, v, ga, gn = _ln_proj(x.reshape(t, d), row2(prm['ln_in_g']), row2(prm['ln_in_b']),
                                           prm['w_in_rwkv'], prm['w_in_na'], prm['w_in_gate'], tm)
    wkv_f, wkv_b, bonus, gate = _rwkv_branch(p_rwkv, b, l, prm, tm)
    y_n = _na_attention(q.reshape(b, l, D_NA), k.reshape(b, l, D_NA), v.reshape(b, l, D_NA),
                        prm['na_bias'])
    x1, x1t, sct = _merge(wkv_f, wkv_b, bonus, gate, row2(prm['lnx_g']), row2(prm['lnx_b']),
                          y_n.reshape(t, D_NA), ga, gn, xn, prm['w_up_a'], prm['w_up_n'], prm['w_out'],
                          row2(prm['ln1_g']), row2(prm['ln1_b']), prm['w_router_t'], tm)
    return _moe(x1, x1t, sct, prm).reshape(b, l, d)


def _moe(x1, x1t, sct, prm):
    t = x1.shape[0]
    tm = MOE_TILE
    idx, wts, tile_cnt = _route(sct, prm['e_bias'], tm)
    tile_base, counts, padded, pstart, block_e, n_used, n_blocks = _block_plan(
        tile_cnt[:, 0, :].astype(jnp.int32), t)
    dest = _dest_rows(idx, tile_base, tm)
    xs = _dispatch(x1t, dest, counts, padded, pstart, n_blocks, tm)
    ys = _experts(xs, block_e, n_used, prm['w_gate_up_e'], prm['w_down_e'])
    row2 = lambda a: a.reshape(1, -1).astype(F32)
    return _final(dest, ys, x1, wts, prm['w_gate_s'], prm['w_up_s'], prm['w_down_s'],
                  row2(prm['ln2_g']), row2(prm['ln2_b']), tm)


def kernel(x_prompt, x_sample, ln_in_g, ln_in_b, w_in, mu_shift, w0, w2, a0, a2, g2, k_k, k_a, r_k,
           lnx_g, lnx_b, rpb, w_up_a, w_up_n, w_out, ln1_g, ln1_b, w_router, e_bias,
           w_gate_e, w_up_e, w_down_e, w_gate_s, w_up_s, w_down_s, ln2_g, ln2_b):
    assert w_in.shape[0] == DEPTH == 1
    w_in0 = w_in[0].astype(BF16)
    prm = dict(
        ln_in_g=ln_in_g, ln_in_b=ln_in_b,
        w_in_rwkv=w_in0[:, :RWKV_IN], w_in_na=w_in0[:, RWKV_IN:RWKV_IN + NA_IN],
        w_in_gate=w_in0[:, RWKV_IN + NA_IN:],
        mu_shift=mu_shift[0], w0=w0[0], w2=w2[0], a0=a0[0], a2=a2[0], g2=g2[0], k_k=k_k[0], k_a=k_a[0],
        r_k=r_k[0], lnx_g=lnx_g[0], lnx_b=lnx_b[0], na_bias=_na_bias_table(rpb[0]),
        w_up_a=w_up_a[0].astype(BF16), w_up_n=w_up_n[0].astype(BF16), w_out=w_out[0].astype(BF16),
        ln1_g=ln1_g[0], ln1_b=ln1_b[0], w_router_t=w_router[0].T, e_bias=e_bias[0],
        w_gate_up_e=jnp.concatenate([w_gate_e[0].astype(BF16), w_up_e[0].astype(BF16)], axis=-1),
        w_down_e=w_down_e[0].astype(BF16),
        w_gate_s=w_gate_s[0].astype(BF16), w_up_s=w_up_s[0].astype(BF16),
        w_down_s=w_down_s[0].astype(BF16), ln2_g=ln2_g[0], ln2_b=ln2_b[0])
    return (_trunk(x_prompt, prm), _trunk(x_sample, prm))
```

```python
import functools

import jax
import jax.numpy as jnp
import numpy as np
from jax import lax
from jax.experimental import pallas as pl
from jax.experimental.pallas import tpu as pltpu

F32 = jnp.float32
BF16 = jnp.bfloat16

D_MODEL = 1024
GRID_W = 64
HEAD_DIM = 64
D_RWKV = 512
D_NA = 512
NA_WIN_H = 8
NA_WIN_W = 16
DECAY_LORA = 64
ICL_LORA = 64
GATE_LORA = 128
N_EXPERTS = 256
TOP_K = 8
N_GROUPS = 8
TOPK_GROUPS = 4
D_EXPERT = 256
ROUTED_SCALE = 2.5
LN_EPS = 1e-5
GN_EPS = 64e-5
DEPTH = 1
ALPHA = (2 * DEPTH) ** 0.25
RWKV_SIZES = (D_RWKV, D_RWKV, D_RWKV, DECAY_LORA, DECAY_LORA, ICL_LORA, ICL_LORA, GATE_LORA)
RWKV_IN = sum(RWKV_SIZES)
NA_IN = 3 * D_NA

LANES = 128
HEADS_PER_SLAB = LANES // HEAD_DIM
SLABS = D_MODEL // LANES // 2
U32 = jnp.uint32
VMEM_LIMIT = 48 * 1024 * 1024

CHUNK = 64
RWKV_SEQS_PER_STEP = 2
HALO_ROWS = 16
MOE_ROWS = 128
MOE_TILE = 256
MOE_GROUP = 4
NEG_BIG = -1e30

HI = lax.Precision.HIGHEST


def _mm(a, b, precision=None):
    return jnp.dot(a, b, preferred_element_type=F32, precision=precision)


def _mm_nt(a, b, precision=None):
    return lax.dot_general(a, b, (((1,), (1,)), ((), ())), preferred_element_type=F32,
                           precision=precision)


def _ln(x, g, b):
    mu = jnp.mean(x, -1, keepdims=True)
    xc = x - mu
    var = jnp.mean(xc * xc, -1, keepdims=True)
    return xc * lax.rsqrt(var + LN_EPS) * g + b


def _ln_proj_kernel(x_ref, g_ref, b_ref, wr_ref, wn_ref, wg_ref,
                    xn_ref, pr_ref, q_ref, k_ref, v_ref, ga_ref, gn_ref):
    xn = _ln(x_ref[...], g_ref[...], b_ref[...])
    xn_ref[...] = xn
    xb = xn.astype(BF16)
    pr_ref[...] = _mm(xb, wr_ref[...]).astype(BF16)
    pn = _mm(xb, wn_ref[...])
    q_ref[...] = pn[:, :D_NA].astype(BF16)
    k_ref[...] = pn[:, D_NA:2 * D_NA].astype(BF16)
    v_ref[...] = pn[:, 2 * D_NA:].astype(BF16)
    pg = _mm(xb, wg_ref[...])
    ga_ref[...] = pg[:, :D_MODEL].astype(BF16)
    gn_ref[...] = pg[:, D_MODEL:].astype(BF16)


def _ln_proj(x, g, b, w_rwkv, w_na, w_gate, tm):
    t = x.shape[0]
    row = lambda n: pl.BlockSpec((tm, n), lambda i: (i, 0))
    full = lambda a: pl.BlockSpec(a.shape, lambda i: (0, 0))
    return pl.pallas_call(
        _ln_proj_kernel,
        grid=(t // tm,),
        in_specs=[row(D_MODEL), full(g), full(b), full(w_rwkv), full(w_na), full(w_gate)],
        out_specs=[row(D_MODEL), row(RWKV_IN), row(D_NA), row(D_NA), row(D_NA),
                   row(D_MODEL), row(D_MODEL)],
        out_shape=[jax.ShapeDtypeStruct((t, D_MODEL), F32),
                   jax.ShapeDtypeStruct((t, RWKV_IN), BF16),
                   jax.ShapeDtypeStruct((t, D_NA), BF16),
                   jax.ShapeDtypeStruct((t, D_NA), BF16),
                   jax.ShapeDtypeStruct((t, D_NA), BF16),
                   jax.ShapeDtypeStruct((t, D_MODEL), BF16),
                   jax.ShapeDtypeStruct((t, D_MODEL), BF16)],
        compiler_params=pltpu.CompilerParams(dimension_semantics=("parallel",),
                                             vmem_limit_bytes=VMEM_LIMIT),
        name="ln_proj",
    )(x, g, b, w_rwkv, w_na, w_gate)


def _rwkv_kernel(rf_ref, vf_ref, kkf_ref, kdf_ref, bef_ref, lwf_ref,
                 rb_ref, vb_ref, kkb_ref, kdb_ref, beb_ref, lwb_ref, yf_ref, yb_ref, z_ref, *, n_slabs, n_seq):
    @pl.when(pl.program_id(1) == 0)
    def _():
        z_ref[...] = jnp.zeros_like(z_ref)

    n2 = HEADS_PER_SLAB * CHUNK
    row = lax.broadcasted_iota(jnp.int32, (n2, n2), 0)
    col = lax.broadcasted_iota(jnp.int32, (n2, n2), 1)
    dts = (row & (CHUNK - 1)) - (col & (CHUNK - 1))
    blk16 = ((row & (CHUNK - 1)) >> 4) == ((col & (CHUNK - 1)) >> 4)
    eye = row == col
    rowi = lax.broadcasted_iota(jnp.int32, (CHUNK, LANES), 0)
    lane = lax.broadcasted_iota(jnp.int32, (CHUNK, LANES), 1)
    head0 = lane < HEAD_DIM

    def bd(x):
        return jnp.concatenate([jnp.where(head0, x, 0.0), jnp.where(head0, 0.0, x)], axis=0)

    def bf(x):
        return x.astype(BF16)

    chains = [(d, (q, p)) for q in range(n_seq) for d in range(2) for p in range(n_slabs)]
    sls = [(q, slice(p * LANES, (p + 1) * LANES)) for _, (q, p) in chains]
    strict = [dts > 0 if d == 0 else dts < 0 for d, _ in chains]
    incl = [dts >= 0 if d == 0 else dts <= 0 for d, _ in chains]
    r_in = [(rf_ref, rb_ref)[d] for d, _ in chains]
    v_in = [(vf_ref, vb_ref)[d] for d, _ in chains]
    kk_in = [(kkf_ref, kkb_ref)[d] for d, _ in chains]
    kd_in = [(kdf_ref, kdb_ref)[d] for d, _ in chains]
    be_in = [(bef_ref, beb_ref)[d] for d, _ in chains]
    lw_in = [(lwf_ref, lwb_ref)[d] for d, _ in chains]
    ident = jnp.where(eye, 1.0, 0.0).astype(F32)

    def each(f, *cols):
        return [f(*args) for args in zip(*cols)]

    lw = each(lambda ref, sl: ref[0, sl[0], :, sl[1]], lw_in, sls)
    def scan_rows(x, reverse):
        step = 1
        while step < CHUNK:
            if reverse:
                x = x + jnp.where(rowi < CHUNK - step, pltpu.roll(x, CHUNK - step, axis=0), 0.0)
            else:
                x = x + jnp.where(rowi >= step, pltpu.roll(x, step, axis=0), 0.0)
            step *= 2
        return x

    cum = [scan_rows(x, d == 1) for x, (d, _) in zip(lw, chains)]
    tot = each(lambda x: jnp.sum(x, axis=0, keepdims=True), lw)
    e_in = each(jnp.exp, cum)
    e_ex = each(lambda c_, l_: jnp.exp(c_ - l_), cum, lw)
    e_neg = each(lambda c_: jnp.exp(-c_), cum)
    e_rem = each(lambda t_, c_: jnp.exp(t_ - c_), tot, cum)
    gam = each(jnp.exp, tot)
    kk = each(lambda ref, sl: ref[sl[0], :, sl[1]].astype(F32), kk_in, sls)
    kd = each(lambda ref, sl: ref[0, sl[0], :, sl[1]].astype(F32), kd_in, sls)
    be = each(lambda ref, sl: ref[0, sl[0], :, sl[1]].astype(F32), be_in, sls)
    a_b = each(lambda x, e: bf(bd(x * e)), kk, e_ex)
    r_t = each(lambda ref, e, sl: bd(ref[sl[0], :, sl[1]].astype(F32) * e), r_in, e_in, sls)
    b_t = each(lambda x, e: bd(x * e), be, e_neg)
    k_t = each(lambda x, e: bd(x * e), kd, e_neg)
    b_h = each(lambda x, e: bd(x * e), be, e_rem)
    k_h = each(lambda x, e: bd(x * e), kd, e_rem)
    vv = each(lambda ref, sl: bf(bd(ref[sl[0], :, sl[1]].astype(F32))), v_in, sls)

    sc = each(lambda a, r, b, k: _mm_nt(jnp.concatenate([a, bf(r)], axis=0),
                                        bf(jnp.concatenate([b, k], axis=0))), a_b, r_t, b_t, k_t)
    lk = each(lambda m, s: bf(jnp.where(m, s[:n2, n2:], 0.0)), strict, sc)
    mb = each(lambda m, s: bf(jnp.where(m, s[n2:, :n2], 0.0)), incl, sc)
    mk = each(lambda m, s: bf(jnp.where(m, s[n2:, n2:], 0.0)), incl, sc)

    nn = each(lambda m, s: jnp.where(m, -s[:n2, :n2], 0.0), strict, sc)
    dg = each(lambda x: jnp.where(blk16, x, 0.0), nn)
    offb = each(lambda x, d_: bf(x - d_), nn, dg)
    def mm2(xs, ys):
        return each(_mm, xs, ys)

    d1b = each(bf, dg)
    d2b = each(bf, mm2(d1b, d1b))
    d4b = each(bf, mm2(d2b, d2b))
    d8b = each(bf, mm2(d4b, d4b))
    td = each(lambda d_: ident + d_, dg)
    for dpow in (d2b, d4b, d8b):
        td = each(lambda t_, m_: t_ + m_, td, mm2(each(bf, td), dpow))
    tdb = each(bf, td)
    e1 = mm2(tdb, offb)
    e1b = each(bf, e1)
    e2 = mm2(e1b, e1b)
    e3 = mm2(e1b, each(bf, e2))
    ttb = each(bf, mm2(each(lambda x1_, x2_, x3_: bf(ident + x1_ + x2_ + x3_), e1, e2, e3), tdb))

    lkv = mm2(lk, vv)
    wb = each(lambda t_, a, l_: bf(_mm(t_, jnp.concatenate([a, bf(l_)], axis=1))), ttb, a_b, lkv)
    mw = each(_mm, mb, wb)
    mkv = mm2(mk, vv)
    bw = each(lambda b, w: _mm(bf(b.T), w), b_h, wb)
    kv = mm2(each(lambda k: bf(k.T), k_h), vv)
    r_hat = each(lambda r, m: bf(r - m[:, :n2]), r_t, mw)
    y_loc = each(lambda m, w: m - w[:, n2:], mkv, mw)
    g = each(lambda g_, b: bf(jnp.where(eye, jnp.broadcast_to(g_, (n2, n2)), 0.0) - b[:, :n2]), gam, bw)
    h = each(lambda k, b: k - b[:, n2:], kv, bw)

    n_ch = len(chains)
    z = [z_ref[i] for i in range(n_ch)]
    z_hi = each(bf, z)
    z_lo = each(lambda z_, zh: bf(z_ - zh.astype(F32)), z, z_hi)
    y = each(lambda m_, yl: m_ + yl, mm2(r_hat, z_hi), y_loc)
    gz = each(lambda g_, zh, zl: _mm(g_, jnp.concatenate([zh, zl], axis=1)), g, z_hi, z_lo)
    z_new = each(lambda m_, h_: m_[:, :n2] + m_[:, n2:] + h_, gz, h)
    for i, (d, _) in enumerate(chains):
        z_ref[i] = z_new[i]
        (yf_ref, yb_ref)[d][sls[i][0], :, sls[i][1]] = (y[i][:CHUNK] + y[i][CHUNK:]).astype(BF16)


def _rwkv_scan(r, v, kk, kd, be, lw):
    b, l, dr = r.shape
    nc = l // CHUNK
    n_slabs = dr // LANES
    n_seq = RWKV_SEQS_PER_STEP if b % RWKV_SEQS_PER_STEP == 0 else 1
    fwd = pl.BlockSpec((n_seq, CHUNK, dr), lambda bi, c: (bi, c, 0))
    bwd = pl.BlockSpec((n_seq, CHUNK, dr), lambda bi, c: (bi, nc - 1 - c, 0))
    fwd_d = pl.BlockSpec((1, n_seq, CHUNK, dr), lambda bi, c: (0, bi, c, 0))
    bwd_d = pl.BlockSpec((1, n_seq, CHUNK, dr), lambda bi, c: (1, bi, nc - 1 - c, 0))
    return pl.pallas_call(
        functools.partial(_rwkv_kernel, n_slabs=n_slabs, n_seq=n_seq),
        grid=(b // n_seq, nc),
        in_specs=[fwd, fwd, fwd, fwd_d, fwd_d, fwd_d, bwd, bwd, bwd, bwd_d, bwd_d, bwd_d],
        out_specs=[fwd, bwd],
        out_shape=[jax.ShapeDtypeStruct((b, l, dr), BF16), jax.ShapeDtypeStruct((b, l, dr), BF16)],
        scratch_shapes=[pltpu.VMEM((2 * n_slabs * n_seq, HEADS_PER_SLAB * CHUNK, HEADS_PER_SLAB * HEAD_DIM),
                                   F32)],
        compiler_params=pltpu.CompilerParams(
            dimension_semantics=("parallel", "arbitrary"),
            vmem_limit_bytes=VMEM_LIMIT),
        name="rwkv_scan",
    )(r, v, kk, kd, be, lw, r, v, kk, kd, be, lw)


def _rwkv_prep_kernel(p_ref, prev_ref, next_ref, mu_ref, wdec_ref, w0_ref, wicl_ref, a0_ref, g2_ref,
                      kk_w_ref, ka_ref, rk_ref, hsum_ref,
                      r_ref, v_ref, kk_ref, kd_ref, be_ref, lw_ref, bonus_ref, gate_ref, *, tiles_per_seq):
    i = pl.program_id(0)
    p = p_ref[...].astype(F32)
    tm = p.shape[0]
    pos = i % tiles_per_seq
    prev_row = jnp.where(pos == 0, 0.0, prev_ref[...].astype(F32)[HALO_ROWS - 1:HALO_ROWS, :])
    next_row = jnp.where(pos == tiles_per_seq - 1, 0.0, next_ref[...].astype(F32)[0:1, :])
    rowi = lax.broadcasted_iota(jnp.int32, (tm, 1), 0)
    up = jnp.where(rowi == 0, prev_row, pltpu.roll(p, 1, axis=0))
    dn = jnp.where(rowi == tm - 1, next_row, pltpu.roll(p, tm - 1, axis=0))
    ps = p + mu_ref[...] * (0.5 * (up + dn) - p)

    c = np.cumsum((0,) + RWKV_SIZES)
    xr, xk, xv = ps[:, c[0]:c[1]], ps[:, c[1]:c[2]], ps[:, c[2]:c[3]]
    dw, da, dg = ps[:, c[3]:c[5]], ps[:, c[5]:c[7]], ps[:, c[7]:c[8]]
    wl = w0_ref[...] + _mm(jnp.tanh(dw).astype(BF16), wdec_ref[...])
    neg = -wl
    softplus = jnp.maximum(neg, 0.0) + jnp.log(1.0 + jnp.exp(-jnp.abs(neg)))
    lw = -jnp.exp(-softplus - 0.5)
    a = jax.nn.sigmoid(a0_ref[...] + _mm(da.astype(BF16), wicl_ref[...]))
    kkr = xk * kk_w_ref[...]
    ss = _mm((kkr * kkr).astype(BF16), hsum_ref[...])
    kk = kkr * lax.rsqrt(jnp.maximum(ss, 1e-24))
    kd_sum = jnp.zeros_like(xk)
    for d in range(2):
        a_d = a[:, d * D_RWKV:(d + 1) * D_RWKV]
        kd = xk * (1.0 + (a_d - 1.0) * ka_ref[...])
        kd_sum = kd_sum + kd
        kd_ref[d] = kd.astype(BF16)
        be_ref[d] = (a_d * kk).astype(BF16)
        lw_ref[d] = lw[:, d * D_RWKV:(d + 1) * D_RWKV]
    r_ref[...] = xr.astype(BF16)
    v_ref[...] = xv.astype(BF16)
    kk_ref[...] = kk.astype(BF16)
    bonus_ref[...] = (_mm((xr * rk_ref[...] * kd_sum).astype(BF16), hsum_ref[...]) * xv).astype(BF16)
    gate_ref[...] = _mm(jax.nn.sigmoid(dg).astype(BF16), g2_ref[...]).astype(BF16)


def _rwkv_branch(p_rwkv, b, l, prm, tm):
    t = b * l
    r, v, kk, kd, be, lw, bonus, gate = _rwkv_prep(p_rwkv, l, prm, tm)
    seq = lambda a: a.reshape(b, l, D_RWKV)
    seq2 = lambda a: a.reshape(2, b, l, D_RWKV)
    wkv_f, wkv_b = _rwkv_scan(seq(r), seq(v), seq(kk), seq2(kd), seq2(be), seq2(lw))
    return wkv_f.reshape(t, D_RWKV), wkv_b.reshape(t, D_RWKV), bonus, gate


def _block_diag2(w):
    z = jnp.zeros_like(w[0])
    return jnp.concatenate([jnp.concatenate([w[0], z], axis=1), jnp.concatenate([z, w[1]], axis=1)], axis=0)


def _head_sum_matrix(scale):
    h = np.arange(D_RWKV) // HEAD_DIM
    return jnp.asarray((h[:, None] == h[None, :]) * scale, BF16)


def _rwkv_prep(p_rwkv, seq_len, prm, tm):
    t = p_rwkv.shape[0]
    row2 = lambda a: a.reshape(1, -1).astype(F32)
    consts = [row2(prm['mu_shift']), _block_diag2(prm['w2']).astype(BF16), row2(prm['w0']),
              _block_diag2(prm['a2']).astype(BF16), row2(prm['a0']), prm['g2'].astype(BF16),
              row2(prm['k_k']), row2(prm['k_a']), row2(prm['r_k']), _head_sum_matrix(1.0)]
    full = lambda a: pl.BlockSpec(a.shape, lambda i: (0, 0))
    row = pl.BlockSpec((tm, D_RWKV), lambda i: (i, 0))
    row_d = pl.BlockSpec((2, tm, D_RWKV), lambda i: (0, i, 0))
    halo = tm // HALO_ROWS
    sd = lambda dt: jax.ShapeDtypeStruct((t, D_RWKV), dt)
    sd2 = lambda dt: jax.ShapeDtypeStruct((2, t, D_RWKV), dt)
    return pl.pallas_call(
        functools.partial(_rwkv_prep_kernel, tiles_per_seq=seq_len // tm),
        grid=(t // tm,),
        in_specs=[pl.BlockSpec((tm, RWKV_IN), lambda i: (i, 0)),
                  pl.BlockSpec((HALO_ROWS, RWKV_IN), lambda i: (jnp.maximum(i * halo - 1, 0), 0)),
                  pl.BlockSpec((HALO_ROWS, RWKV_IN),
                               lambda i: (jnp.minimum((i + 1) * halo, t // HALO_ROWS - 1), 0))]
        + [full(a) for a in consts],
        out_specs=[row, row, row, row_d, row_d, row_d, row, row],
        out_shape=[sd(BF16), sd(BF16), sd(BF16), sd2(BF16), sd2(BF16), sd2(F32), sd(BF16), sd(BF16)],
        compiler_params=pltpu.CompilerParams(dimension_semantics=("parallel",),
                                             vmem_limit_bytes=VMEM_LIMIT),
        name="rwkv_prep",
    )(p_rwkv, p_rwkv, p_rwkv, *consts)


def _na_bias_table(rpb):
    n_h = rpb.shape[0]
    var = np.arange(NA_WIN_H)[:, None]
    i = np.arange(NA_WIN_H)[None, :]
    c = np.arange(GRID_W)[:, None]
    kc = np.arange(GRID_W)[None, :]
    cs = np.clip(c - NA_WIN_W // 2, 0, GRID_W - NA_WIN_W)
    valid = (kc >= cs) & (kc < cs + NA_WIN_W)
    row_sel = (np.arange(2 * NA_WIN_H - 1)[None, None, :] == (i - var + NA_WIN_H - 1)[:, :, None])
    col_sel = (np.arange(2 * NA_WIN_W - 1)[None, None, :] == (kc - c + NA_WIN_W - 1)[:, :, None])
    col_sel = col_sel & valid[:, :, None]
    tab = jnp.einsum('hab,via,ckb->vhcik', rpb.astype(F32), row_sel.astype(np.float32),
                     col_sel.astype(np.float32), precision=HI)
    tab = jnp.where(valid[None, None, :, None, :], tab, NEG_BIG)
    return tab.reshape(NA_WIN_H, n_h * GRID_W, NA_WIN_H * GRID_W)


def _na_kernel(q_ref, k_ref, v_ref, bias_ref, o_ref, *, rows, n_slabs):
    r = pl.program_id(1)
    rs = jnp.clip(r - NA_WIN_H // 2, 0, rows - NA_WIN_H)
    start = pl.multiple_of(rs * GRID_W, GRID_W)
    band = NA_WIN_H * GRID_W
    scale = HEAD_DIM ** -0.5
    lane = lax.broadcasted_iota(jnp.int32, (GRID_W, LANES), 1)
    head0 = lane < HEAD_DIM
    sls = [slice(p * LANES, (p + 1) * LANES) for p in range(n_slabs)]

    def split_heads(q2):
        zero = jnp.zeros_like(q2)
        return jnp.concatenate([jnp.where(head0, q2, zero), jnp.where(head0, zero, q2)], axis=0)

    lhs = [split_heads(q_ref[0, :, sl]) for sl in sls]
    s = [_mm_nt(x, k_ref[0, pl.ds(start, band), sl]) for x, sl in zip(lhs, sls)]
    s = [x * scale + bias_ref[0, p * 2 * GRID_W:(p + 1) * 2 * GRID_W, :] for p, x in enumerate(s)]
    m = [jnp.max(x, axis=-1, keepdims=True) for x in s]
    e = [jnp.exp(x - mx) for x, mx in zip(s, m)]
    den = [jnp.sum(x, axis=-1, keepdims=True) for x in e]
    o = [_mm(x.astype(BF16), v_ref[0, pl.ds(start, band), sl]) for x, sl in zip(e, sls)]
    for x, d, sl in zip(o, den, sls):
        x = x / d
        o_ref[0, :, sl] = jnp.where(head0, x[:GRID_W], x[GRID_W:]).astype(o_ref.dtype)


def _na_attention(q, k, v, bias_tab):
    b, l, dn = q.shape
    rows = l // GRID_W
    assert rows >= NA_WIN_H
    n_slabs = dn // LANES

    def bias_idx(bi, r):
        rs = jnp.clip(r - NA_WIN_H // 2, 0, rows - NA_WIN_H)
        return (r - rs, 0, 0)

    return pl.pallas_call(
        functools.partial(_na_kernel, rows=rows, n_slabs=n_slabs),
        grid=(b, rows),
        in_specs=[pl.BlockSpec((1, GRID_W, dn), lambda bi, r: (bi, r, 0)),
                  pl.BlockSpec((1, l, dn), lambda bi, r: (bi, 0, 0)),
                  pl.BlockSpec((1, l, dn), lambda bi, r: (bi, 0, 0)),
                  pl.BlockSpec((1,) + bias_tab.shape[1:], bias_idx)],
        out_specs=pl.BlockSpec((1, GRID_W, dn), lambda bi, r: (bi, r, 0)),
        out_shape=jax.ShapeDtypeStruct((b, l, dn), BF16),
        compiler_params=pltpu.CompilerParams(dimension_semantics=("parallel", "arbitrary"),
                                             vmem_limit_bytes=VMEM_LIMIT),
        name="na_attention",
    )(q, k, v, bias_tab)


def _to_token_tiles(dst_ref, x):
    n = x.shape[0]
    for s in range(SLABS):
        lo = _bf16_bits(x[:, s * LANES:(s + 1) * LANES])
        hi = _bf16_bits(x[:, (s + SLABS) * LANES:(s + SLABS + 1) * LANES])
        dst_ref[pl.ds(s, n, stride=SLABS), :] = (lo >> 16) | hi


def _from_token_tiles(src_ref, n):
    words = [src_ref[pl.ds(s, n, stride=SLABS), :] for s in range(SLABS)]
    lo = [lax.bitcast_convert_type(w << 16, F32) for w in words]
    hi = [lax.bitcast_convert_type(w & U32(0xFFFF0000), F32) for w in words]
    return jnp.concatenate(lo + hi, axis=1)


def _bf16_bits(x):
    b = lax.bitcast_convert_type(x, U32)
    return (b + U32(0x7FFF) + ((b >> 16) & U32(1))) & U32(0xFFFF0000)


def _merge_kernel(wf_ref, wb_ref, bonus_ref, gate_ref, havg_ref, lg_ref, lb_ref,
                  yn_ref, ga_ref, gn_ref, xn_ref, wa_ref, wn_ref, wo_ref, g1_ref, b1_ref,
                  wrt_ref, x1_ref, x1t_ref, sct_ref):
    wkv = wf_ref[...].astype(F32) + wb_ref[...].astype(F32)
    xc = wkv - _mm(wkv.astype(BF16), havg_ref[...])
    var = _mm((xc * xc).astype(BF16), havg_ref[...])
    ya = ((xc * lax.rsqrt(var + GN_EPS) * lg_ref[...] + lb_ref[...] + bonus_ref[...].astype(F32))
          * gate_ref[...].astype(F32))
    up_a = _mm(ya.astype(BF16), wa_ref[...])
    up_n = _mm(yn_ref[...], wn_ref[...])
    merged = (jax.nn.sigmoid(ga_ref[...].astype(F32)) * up_a
              + jax.nn.sigmoid(gn_ref[...].astype(F32)) * up_n)
    mix = _mm(merged.astype(BF16), wo_ref[...])
    x1 = _ln(ALPHA * xn_ref[...] + mix, g1_ref[...], b1_ref[...])
    x1_ref[...] = x1
    _to_token_tiles(x1t_ref, x1)
    sct_ref[...] = jax.nn.sigmoid(_mm_nt(wrt_ref[...], x1, HI))


def _merge(wkv_f, wkv_b, bonus, gate, lnx_g, lnx_b, yn, ga, gn, xn, w_up_a, w_up_n, w_out, ln1_g, ln1_b,
           w_router_t, tm):
    t = xn.shape[0]
    row = lambda n: pl.BlockSpec((tm, n), lambda i: (i, 0))
    full = lambda a: pl.BlockSpec(a.shape, lambda i: (0, 0))
    havg = _head_sum_matrix(1.0 / HEAD_DIM)
    return pl.pallas_call(
        _merge_kernel,
        grid=(t // tm,),
        in_specs=[row(D_RWKV), row(D_RWKV), row(D_RWKV), row(D_RWKV), full(havg), full(lnx_g), full(lnx_b),
                  row(D_NA), row(D_MODEL), row(D_MODEL), row(D_MODEL),
                  full(w_up_a), full(w_up_n), full(w_out), full(ln1_g), full(ln1_b), full(w_router_t)],
        out_specs=[row(D_MODEL), pl.BlockSpec((tm * SLABS, LANES), lambda i: (i, 0)),
                   pl.BlockSpec((N_EXPERTS, tm), lambda i: (0, i))],
        out_shape=[jax.ShapeDtypeStruct((t, D_MODEL), F32),
                   jax.ShapeDtypeStruct((t * SLABS, LANES), U32),
                   jax.ShapeDtypeStruct((N_EXPERTS, t), F32)],
        compiler_params=pltpu.CompilerParams(dimension_semantics=("parallel",),
                                             vmem_limit_bytes=VMEM_LIMIT),
        name="merge_ln1_router",
    )(wkv_f, wkv_b, bonus, gate, havg, lnx_g, lnx_b, yn, ga, gn, xn, w_up_a, w_up_n, w_out, ln1_g, ln1_b,
      w_router_t)


def _first_argmax(vals, iota):
    m = jnp.max(vals, axis=0, keepdims=True)
    first = jnp.min(jnp.where(vals == m, iota, float(vals.shape[0])), axis=0, keepdims=True)
    return m, first


def _route_kernel(sct_ref, bias_ref, idx_ref, wts_ref, cnt_ref):
    s = sct_ref[...]
    tm = s.shape[1]
    sel = s + bias_ref[...]
    gsz = N_EXPERTS // N_GROUPS
    iota_g = lax.broadcasted_iota(jnp.int32, (gsz, tm), 0).astype(F32)
    iota_8 = lax.broadcasted_iota(jnp.int32, (N_GROUPS, tm), 0).astype(F32)
    iota_e = lax.broadcasted_iota(jnp.int32, (N_EXPERTS, tm), 0).astype(F32)

    gs = jnp.zeros((N_GROUPS, tm), F32)
    for g in range(N_GROUPS):
        blk = sel[g * gsz:(g + 1) * gsz, :]
        m1, i1 = _first_argmax(blk, iota_g)
        m2 = jnp.max(jnp.where(iota_g == i1, -jnp.inf, blk), axis=0, keepdims=True)
        gs = jnp.where(iota_8 == float(g), m1 + m2, gs)

    chosen = jnp.zeros((N_GROUPS, tm), F32)
    for _ in range(TOPK_GROUPS):
        _, gi = _first_argmax(gs, iota_8)
        hit = iota_8 == gi
        chosen = jnp.where(hit, 1.0, chosen)
        gs = jnp.where(hit, -jnp.inf, gs)
    mask_e = jnp.concatenate([jnp.broadcast_to(chosen[g:g + 1, :], (gsz, tm)) for g in range(N_GROUPS)],
                             axis=0)
    cand = jnp.where(mask_e > 0.0, sel, -jnp.inf)

    iota_k = lax.broadcasted_iota(jnp.int32, (TOP_K, tm), 0)
    idx = jnp.zeros((TOP_K, tm), F32)
    wts = jnp.zeros((TOP_K, tm), F32)
    member = jnp.zeros((N_EXPERTS, tm), F32)
    for j in range(TOP_K):
        _, ij = _first_argmax(cand, iota_e)
        hit = iota_e == ij
        wj = jnp.sum(jnp.where(hit, s, 0.0), axis=0, keepdims=True)
        cand = jnp.where(hit, -jnp.inf, cand)
        member = jnp.where(hit, 1.0, member)
        idx = jnp.where(iota_k == j, ij, idx)
        wts = jnp.where(iota_k == j, wj, wts)
    wts = wts / jnp.sum(wts, axis=0, keepdims=True) * ROUTED_SCALE
    idx_ref[...] = idx.astype(jnp.int32)
    wts_ref[...] = wts
    cnt_ref[0] = _mm_nt(jnp.ones((8, tm), BF16), member.astype(BF16))


def _route(sct, e_bias, tm):
    n_e, t = sct.shape
    nt = t // tm
    return pl.pallas_call(
        _route_kernel,
        grid=(nt,),
        in_specs=[pl.BlockSpec((n_e, tm), lambda i: (0, i)),
                  pl.BlockSpec((n_e, 1), lambda i: (0, 0))],
        out_specs=[pl.BlockSpec((TOP_K, tm), lambda i: (0, i)),
                   pl.BlockSpec((TOP_K, tm), lambda i: (0, i)),
                   pl.BlockSpec((1, 8, n_e), lambda i: (i, 0, 0))],
        out_shape=[jax.ShapeDtypeStruct((TOP_K, t), jnp.int32),
                   jax.ShapeDtypeStruct((TOP_K, t), F32),
                   jax.ShapeDtypeStruct((nt, 8, n_e), F32)],
        compiler_params=pltpu.CompilerParams(dimension_semantics=("parallel",),
                                             vmem_limit_bytes=VMEM_LIMIT),
        name="moe_route",
    )(sct, e_bias.reshape(n_e, 1).astype(F32))


def _dest_kernel(idx_ref, base_ref, dest_ref):
    idx = idx_ref[...].astype(F32)
    tm = idx.shape[1]
    iota_e = lax.broadcasted_iota(jnp.int32, (N_EXPERTS, tm), 0).astype(F32)
    hits = [iota_e == idx[j:j + 1, :] for j in range(TOP_K)]
    member = jnp.zeros((N_EXPERTS, tm), F32)
    for hit in hits:
        member = jnp.where(hit, 1.0, member)
    r = lax.broadcasted_iota(jnp.int32, (tm, tm), 0)
    c = lax.broadcasted_iota(jnp.int32, (tm, tm), 1)
    earlier = jnp.where(r < c, 1.0, 0.0).astype(BF16)
    rank = _mm(member.astype(BF16), earlier) + base_ref[0]
    iota_k = lax.broadcasted_iota(jnp.int32, (TOP_K, tm), 0)
    dest = jnp.zeros((TOP_K, tm), F32)
    for j, hit in enumerate(hits):
        dj = jnp.sum(jnp.where(hit, rank, 0.0), axis=0, keepdims=True)
        dest = jnp.where(iota_k == j, dj, dest)
    dest_ref[0] = dest.astype(jnp.int32)


def _dest_rows(idx, base, tm):
    t = idx.shape[1]
    nt = t // tm
    return pl.pallas_call(
        _dest_kernel,
        grid=(nt,),
        in_specs=[pl.BlockSpec((TOP_K, tm), lambda i: (0, i)),
                  pl.BlockSpec((1, N_EXPERTS, 1), lambda i: (i, 0, 0))],
        out_specs=pl.BlockSpec((1, TOP_K, tm), lambda i: (i, 0, 0)),
        out_shape=jax.ShapeDtypeStruct((nt, TOP_K, tm), jnp.int32),
        compiler_params=pltpu.CompilerParams(dimension_semantics=("parallel",),
                                             vmem_limit_bytes=VMEM_LIMIT),
        name="moe_dest",
    )(idx, base)


def _block_plan(tile_counts, n_tokens):
    n_blocks = (n_tokens * TOP_K + N_EXPERTS * (MOE_ROWS - 1) + MOE_ROWS - 1) // MOE_ROWS
    n_blocks = -(-n_blocks // MOE_GROUP) * MOE_GROUP
    counts = jnp.sum(tile_counts, axis=0)
    padded = (counts + MOE_ROWS - 1) // MOE_ROWS * MOE_ROWS
    extra = (-(jnp.sum(padded) // MOE_ROWS)) % MOE_GROUP
    padded = padded.at[N_EXPERTS - 1].add(extra * MOE_ROWS)
    pend = jnp.cumsum(padded)
    pstart = pend - padded
    tile_base = pstart[None, :] + jnp.cumsum(tile_counts, axis=0) - tile_counts
    block_start = jnp.arange(n_blocks, dtype=jnp.int32) * MOE_ROWS
    block_e = jnp.minimum(jnp.sum(pend[None, :] <= block_start[:, None], axis=1),
                          N_EXPERTS - 1).astype(jnp.int32)
    n_used = (pend[-1] // MOE_ROWS).astype(jnp.int32).reshape(1)
    return (tile_base.astype(F32)[:, :, None], counts.astype(jnp.int32), padded.astype(jnp.int32),
            pstart.astype(jnp.int32), block_e, n_used, n_blocks)


def _row_tile(ref, row):
    return ref.at[pl.ds(pl.multiple_of(row * SLABS, SLABS), SLABS)]


def _dispatch_kernel(cnt_ref, pad_ref, pst_ref, dest_hbm, x_ref, xs_hbm, dest_s, zrow, isem, csem, zsem,
                     *, tm):
    i = pl.program_id(0)
    n = pl.num_programs(0)
    cur = i % 2

    def idx_copy(tile, buf):
        return pltpu.make_async_copy(dest_hbm.at[tile], dest_s.at[buf], isem.at[buf])

    @pl.when(i == 0)
    def _():
        idx_copy(0, 0).start()
        zrow[...] = jnp.zeros_like(zrow)

        def per_expert(e, carry, wait):
            def per_row(r, c2):
                cp = pltpu.make_async_copy(zrow, _row_tile(xs_hbm, pst_ref[e] + r), zsem)
                if wait:
                    cp.wait()
                else:
                    cp.start()
                return c2
            return lax.fori_loop(cnt_ref[e], pad_ref[e], per_row, carry)

        lax.fori_loop(0, N_EXPERTS, functools.partial(per_expert, wait=False), 0)
        lax.fori_loop(0, N_EXPERTS, functools.partial(per_expert, wait=True), 0)

    idx_copy(i, cur).wait()

    @pl.when(i + 1 < n)
    def _():
        idx_copy(i + 1, 1 - cur).start()

    def per_token(t, carry):
        src = x_ref.at[pl.ds(pl.multiple_of(t * SLABS, SLABS), SLABS)]
        for j in range(TOP_K):
            pltpu.make_async_copy(src, _row_tile(xs_hbm, dest_s[cur, j, t]), csem).start()
        return carry

    lax.fori_loop(0, tm, per_token, 0)
    for j in range(TOP_K):
        pltpu.make_async_copy(x_ref, xs_hbm.at[pl.ds(0, tm * SLABS)], csem).wait()


def _dispatch(x1t, dest, counts, padded, pstart, n_blocks, tm):
    nt = dest.shape[0]
    any_spec = pl.BlockSpec(memory_space=pl.ANY)
    grid_spec = pltpu.PrefetchScalarGridSpec(
        num_scalar_prefetch=3,
        grid=(nt,),
        in_specs=[any_spec, pl.BlockSpec((tm * SLABS, LANES), lambda i, c, p, s: (i, 0))],
        out_specs=any_spec,
        scratch_shapes=[pltpu.SMEM((2, TOP_K, tm), jnp.int32),
                        pltpu.VMEM((SLABS, LANES), U32),
                        pltpu.SemaphoreType.DMA((2,)),
                        pltpu.SemaphoreType.DMA(()),
                        pltpu.SemaphoreType.DMA(())])
    return pl.pallas_call(
        functools.partial(_dispatch_kernel, tm=tm),
        grid_spec=grid_spec,
        out_shape=jax.ShapeDtypeStruct((n_blocks * MOE_ROWS * SLABS, LANES), U32),
        compiler_params=pltpu.CompilerParams(dimension_semantics=("arbitrary",),
                                             vmem_limit_bytes=VMEM_LIMIT),
        name="moe_dispatch",
    )(counts, padded, pstart, dest, x1t)


def _swiglu(xb, wg, wu, wd):
    hg = _mm(xb, wg)
    hu = _mm(xb, wu)
    return _mm((hg * jax.nn.sigmoid(hg) * hu).astype(BF16), wd)


def _experts_kernel(be_ref, nu_ref, xs_ref, *refs):
    w_refs, ys_ref = refs[:-1], refs[-1]
    s = pl.program_id(0)
    blk = MOE_ROWS * SLABS

    @pl.when(MOE_GROUP * s < nu_ref[0])
    def _():
        parts = [pl.ds(k * blk, blk) for k in range(MOE_GROUP)]
        wgu = [w_refs[2 * k] for k in range(MOE_GROUP)]
        wd = [w_refs[2 * k + 1] for k in range(MOE_GROUP)]
        xb, hh = {}, {}
        for step in range(MOE_GROUP + 2):
            k = step
            if k < MOE_GROUP:
                xb[k] = _from_token_tiles(xs_ref.at[parts[k]], MOE_ROWS).astype(BF16)
            k = step - 1
            if 0 <= k < MOE_GROUP:
                hgu = _mm(xb[k], wgu[k][0])
                hg, hu = hgu[:, :D_EXPERT], hgu[:, D_EXPERT:]
                hh[k] = (hg * jax.nn.sigmoid(hg) * hu).astype(BF16)
            k = step - 2
            if 0 <= k < MOE_GROUP:
                _to_token_tiles(ys_ref.at[parts[k]], _mm(hh[k], wd[k][0]))

    @pl.when(MOE_GROUP * s >= nu_ref[0])
    def _():
        ys_ref[...] = jnp.zeros_like(ys_ref)


def _experts(xs, block_e, n_used, w_gate_up_e, w_down_e):
    n_groups = block_e.shape[0] // MOE_GROUP
    blkg = MOE_GROUP * MOE_ROWS * SLABS
    wspec = lambda shp, k: pl.BlockSpec((1,) + shp, lambda s, be, nu: (be[MOE_GROUP * s + k], 0, 0))
    wspecs = [wspec(shp, k) for k in range(MOE_GROUP)
              for shp in ((D_MODEL, 2 * D_EXPERT), (D_EXPERT, D_MODEL))]
    grid_spec = pltpu.PrefetchScalarGridSpec(
        num_scalar_prefetch=2,
        grid=(n_groups,),
        in_specs=[pl.BlockSpec((blkg, LANES),
                               lambda s, be, nu: (jnp.minimum(s, nu[0] // MOE_GROUP - 1), 0))] + wspecs,
        out_specs=pl.BlockSpec((blkg, LANES), lambda s, be, nu: (s, 0)))
    return pl.pallas_call(
        _experts_kernel,
        grid_spec=grid_spec,
        out_shape=jax.ShapeDtypeStruct(xs.shape, xs.dtype),
        compiler_params=pltpu.CompilerParams(dimension_semantics=("arbitrary",),
                                             vmem_limit_bytes=VMEM_LIMIT),
        name="moe_experts",
    )(block_e, n_used, xs, *([w_gate_up_e, w_down_e] * MOE_GROUP))


def _final_kernel(dest_hbm, ys_hbm, x1_ref, wts_ref, wg_ref, wu_ref, wd_ref, g2_ref, b2_ref, o_ref,
                  dest_s, gbuf, isem, gsem, *, tm):
    i = pl.program_id(0)
    n = pl.num_programs(0)
    cur = i % 2
    nxt = 1 - cur

    def idx_copy(tile, buf):
        return pltpu.make_async_copy(dest_hbm.at[tile], dest_s.at[buf], isem.at[buf])

    def start_gather(buf):
        def per_token(t, carry):
            for j in range(TOP_K):
                dst = gbuf.at[buf, j, pl.ds(pl.multiple_of(t * SLABS, SLABS), SLABS)]
                pltpu.make_async_copy(_row_tile(ys_hbm, dest_s[buf, j, t]), dst, gsem.at[buf]).start()
            return carry
        lax.fori_loop(0, tm, per_token, 0)

    @pl.when(i == 0)
    def _():
        idx_copy(0, 0).start()
        idx_copy(0, 0).wait()
        start_gather(0)

        @pl.when(1 < n)
        def _():
            idx_copy(1, 1).start()

    @pl.when(i + 1 < n)
    def _():
        idx_copy(i + 1, nxt).wait()
        start_gather(nxt)

        @pl.when(i + 2 < n)
        def _():
            idx_copy(i + 2, cur).start()

    x1 = x1_ref[...]
    y = _swiglu(x1.astype(BF16), wg_ref[...], wu_ref[...], wd_ref[...])
    r = lax.broadcasted_iota(jnp.int32, (tm, tm), 0)
    c = lax.broadcasted_iota(jnp.int32, (tm, tm), 1)
    w_cols = _mm_nt(jnp.where(r == c, 1.0, 0.0).astype(F32), wts_ref[...], HI)
    for j in range(TOP_K):
        pltpu.make_async_copy(ys_hbm.at[pl.ds(0, tm * SLABS)], gbuf.at[cur, j], gsem.at[cur]).wait()
    for j in range(TOP_K):
        y = y + w_cols[:, j:j + 1] * _from_token_tiles(gbuf.at[cur, j], tm)
    o_ref[...] = _ln(ALPHA * x1 + y, g2_ref[...], b2_ref[...])


def _final(dest, ys, x1, wts, w_gate_s, w_up_s, w_down_s, ln2_g, ln2_b, tm):
    t = x1.shape[0]
    any_spec = pl.BlockSpec(memory_space=pl.ANY)
    row = pl.BlockSpec((tm, D_MODEL), lambda i: (i, 0))
    full = lambda a: pl.BlockSpec(a.shape, lambda i: (0, 0))
    return pl.pallas_call(
        functools.partial(_final_kernel, tm=tm),
        grid=(t // tm,),
        in_specs=[any_spec, any_spec, row, pl.BlockSpec((TOP_K, tm), lambda i: (0, i)),
                  full(w_gate_s), full(w_up_s), full(w_down_s), full(ln2_g), full(ln2_b)],
        out_specs=row,
        out_shape=jax.ShapeDtypeStruct((t, D_MODEL), F32),
        scratch_shapes=[pltpu.SMEM((2, TOP_K, tm), jnp.int32),
                        pltpu.VMEM((2, TOP_K, tm * SLABS, LANES), U32),
                        pltpu.SemaphoreType.DMA((2,)),
                        pltpu.SemaphoreType.DMA((2,))],
        compiler_params=pltpu.CompilerParams(dimension_semantics=("arbitrary",),
                                             vmem_limit_bytes=VMEM_LIMIT),
        name="shared_combine_ln2",
    )(dest, ys, x1, wts, w_gate_s, w_up_s, w_down_s, ln2_g, ln2_b)


def _trunk(x, prm):
    b, l, d = x.shape
    t = b * l
    tm = 512
    row2 = lambda a: a.reshape(1, -1).astype(F32)
    xn, p_rwkv, q, k, v, ga, gn = _ln_proj(x.reshape(t, d), row2(prm['ln_in_g']), row2(prm['ln_in_b']),
                                           prm['w_in_rwkv'], prm['w_in_na'], prm['w_in_gate'], tm)
    wkv_f, wkv_b, bonus, gate = _rwkv_branch(p_rwkv, b, l, prm, tm)
    y_n = _na_attention(q.reshape(b, l, D_NA), k.reshape(b, l, D_NA), v.reshape(b, l, D_NA),
                        prm['na_bias'])
    x1, x1t, sct = _merge(wkv_f, wkv_b, bonus, gate, row2(prm['lnx_g']), row2(prm['lnx_b']),
                          y_n.reshape(t, D_NA), ga, gn, xn, prm['w_up_a'], prm['w_up_n'], prm['w_out'],
                          row2(prm['ln1_g']), row2(prm['ln1_b']), prm['w_router_t'], tm)
    return _moe(x1, x1t, sct, prm).reshape(b, l, d)


def _moe(x1, x1t, sct, prm):
    t = x1.shape[0]
    tm = MOE_TILE
    idx, wts, tile_cnt = _route(sct, prm['e_bias'], tm)
    tile_base, counts, padded, pstart, block_e, n_used, n_blocks = _block_plan(
        tile_cnt[:, 0, :].astype(jnp.int32), t)
    dest = _dest_rows(idx, tile_base, tm)
    xs = _dispatch(x1t, dest, counts, padded, pstart, n_blocks, tm)
    ys = _experts(xs, block_e, n_used, prm['w_gate_up_e'], prm['w_down_e'])
    row2 = lambda a: a.reshape(1, -1).astype(F32)
    return _final(dest, ys, x1, wts, prm['w_gate_s'], prm['w_up_s'], prm['w_down_s'],
                  row2(prm['ln2_g']), row2(prm['ln2_b']), tm)


def kernel(x_prompt, x_sample, ln_in_g, ln_in_b, w_in, mu_shift, w0, w2, a0, a2, g2, k_k, k_a, r_k,
           lnx_g, lnx_b, rpb, w_up_a, w_up_n, w_out, ln1_g, ln1_b, w_router, e_bias,
           w_gate_e, w_up_e, w_down_e, w_gate_s, w_up_s, w_down_s, ln2_g, ln2_b):
    assert w_in.shape[0] == DEPTH == 1
    w_in0 = w_in[0].astype(BF16)
    prm = dict(
        ln_in_g=ln_in_g, ln_in_b=ln_in_b,
        w_in_rwkv=w_in0[:, :RWKV_IN], w_in_na=w_in0[:, RWKV_IN:RWKV_IN + NA_IN],
        w_in_gate=w_in0[:, RWKV_IN + NA_IN:],
        mu_shift=mu_shift[0], w0=w0[0], w2=w2[0], a0=a0[0], a2=a2[0], g2=g2[0], k_k=k_k[0], k_a=k_a[0],
        r_k=r_k[0], lnx_g=lnx_g[0], lnx_b=lnx_b[0], na_bias=_na_bias_table(rpb[0]),
        w_up_a=w_up_a[0].astype(BF16), w_up_n=w_up_n[0].astype(BF16), w_out=w_out[0].astype(BF16),
        ln1_g=ln1_g[0], ln1_b=ln1_b[0], w_router_t=w_router[0].T, e_bias=e_bias[0],
        w_gate_up_e=jnp.concatenate([w_gate_e[0].astype(BF16), w_up_e[0].astype(BF16)], axis=-1),
        w_down_e=w_down_e[0].astype(BF16),
        w_gate_s=w_gate_s[0].astype(BF16), w_up_s=w_up_s[0].astype(BF16),
        w_down_s=w_down_s[0].astype(BF16), ln2_g=ln2_g[0], ln2_b=ln2_b[0])
    return (_trunk(x_prompt, prm), _trunk(x_sample, prm))
```

```python
import functools

import jax
import jax.numpy as jnp
import numpy as np
from jax import lax
from jax.experimental import pallas as pl
from jax.experimental.pallas import tpu as pltpu

F32 = jnp.float32
BF16 = jnp.bfloat16

D_MODEL = 1024
GRID_W = 64
HEAD_DIM = 64
D_RWKV = 512
D_NA = 512
NA_WIN_H = 8
NA_WIN_W = 16
DECAY_LORA = 64
ICL_LORA = 64
GATE_LORA = 128
N_EXPERTS = 256
TOP_K = 8
N_GROUPS = 8
TOPK_GROUPS = 4
D_EXPERT = 256
ROUTED_SCALE = 2.5
LN_EPS = 1e-5
GN_EPS = 64e-5
DEPTH = 1
ALPHA = (2 * DEPTH) ** 0.25
RWKV_SIZES = (D_RWKV, D_RWKV, D_RWKV, DECAY_LORA, DECAY_LORA, ICL_LORA, ICL_LORA, GATE_LORA)
RWKV_IN = sum(RWKV_SIZES)
NA_IN = 3 * D_NA

LANES = 128
HEADS_PER_SLAB = LANES // HEAD_DIM
SLABS = D_MODEL // LANES // 2
U32 = jnp.uint32
VMEM_LIMIT = 48 * 1024 * 1024

CHUNK = 64
RWKV_SEQS_PER_STEP = 2
HALO_ROWS = 16
NA_ROWS_PER_STEP = 2
MOE_ROWS = 128
MOE_TILE = 256
MOE_GROUP = 4
NEG_BIG = -1e30

HI = lax.Precision.HIGHEST


def _mm(a, b, precision=None):
    return jnp.dot(a, b, preferred_element_type=F32, precision=precision)


def _mm_nt(a, b, precision=None):
    return lax.dot_general(a, b, (((1,), (1,)), ((), ())), preferred_element_type=F32,
                           precision=precision)


def _ln(x, g, b):
    mu = jnp.mean(x, -1, keepdims=True)
    xc = x - mu
    var = jnp.mean(xc * xc, -1, keepdims=True)
    return xc * lax.rsqrt(var + LN_EPS) * g + b


def _ln_proj_kernel(x_ref, g_ref, b_ref, wr_ref, wn_ref, wg_ref,
                    xn_ref, pr_ref, q_ref, k_ref, v_ref, ga_ref, gn_ref):
    xn = _ln(x_ref[...], g_ref[...], b_ref[...])
    xn_ref[...] = xn
    xb = xn.astype(BF16)
    pr_ref[...] = _mm(xb, wr_ref[...]).astype(BF16)
    pn = _mm(xb, wn_ref[...])
    q_ref[...] = pn[:, :D_NA].astype(BF16)
    k_ref[...] = pn[:, D_NA:2 * D_NA].astype(BF16)
    v_ref[...] = pn[:, 2 * D_NA:].astype(BF16)
    pg = _mm(xb, wg_ref[...])
    ga_ref[...] = pg[:, :D_MODEL].astype(BF16)
    gn_ref[...] = pg[:, D_MODEL:].astype(BF16)


def _ln_proj(x, g, b, w_rwkv, w_na, w_gate, tm):
    t = x.shape[0]
    row = lambda n: pl.BlockSpec((tm, n), lambda i: (i, 0))
    full = lambda a: pl.BlockSpec(a.shape, lambda i: (0, 0))
    return pl.pallas_call(
        _ln_proj_kernel,
        grid=(t // tm,),
        in_specs=[row(D_MODEL), full(g), full(b), full(w_rwkv), full(w_na), full(w_gate)],
        out_specs=[row(D_MODEL), row(RWKV_IN), row(D_NA), row(D_NA), row(D_NA),
                   row(D_MODEL), row(D_MODEL)],
        out_shape=[jax.ShapeDtypeStruct((t, D_MODEL), F32),
                   jax.ShapeDtypeStruct((t, RWKV_IN), BF16),
                   jax.ShapeDtypeStruct((t, D_NA), BF16),
                   jax.ShapeDtypeStruct((t, D_NA), BF16),
                   jax.ShapeDtypeStruct((t, D_NA), BF16),
                   jax.ShapeDtypeStruct((t, D_MODEL), BF16),
                   jax.ShapeDtypeStruct((t, D_MODEL), BF16)],
        compiler_params=pltpu.CompilerParams(dimension_semantics=("parallel",),
                                             vmem_limit_bytes=VMEM_LIMIT),
        name="ln_proj",
    )(x, g, b, w_rwkv, w_na, w_gate)


def _rwkv_kernel(rf_ref, vf_ref, kkf_ref, kdf_ref, bef_ref, lwf_ref,
                 rb_ref, vb_ref, kkb_ref, kdb_ref, beb_ref, lwb_ref, yf_ref, yb_ref, z_ref, *, n_slabs, n_seq):
    @pl.when(pl.program_id(1) == 0)
    def _():
        z_ref[...] = jnp.zeros_like(z_ref)

    n2 = HEADS_PER_SLAB * CHUNK
    row = lax.broadcasted_iota(jnp.int32, (n2, n2), 0)
    col = lax.broadcasted_iota(jnp.int32, (n2, n2), 1)
    dts = (row & (CHUNK - 1)) - (col & (CHUNK - 1))
    blk16 = ((row & (CHUNK - 1)) >> 4) == ((col & (CHUNK - 1)) >> 4)
    eye = row == col
    rowi = lax.broadcasted_iota(jnp.int32, (CHUNK, LANES), 0)
    lane = lax.broadcasted_iota(jnp.int32, (CHUNK, LANES), 1)
    head0 = lane < HEAD_DIM

    def bd(x):
        return jnp.concatenate([jnp.where(head0, x, 0.0), jnp.where(head0, 0.0, x)], axis=0)

    def bf(x):
        return x.astype(BF16)

    chains = [(d, (q, p)) for q in range(n_seq) for d in range(2) for p in range(n_slabs)]
    sls = [(q, slice(p * LANES, (p + 1) * LANES)) for _, (q, p) in chains]
    strict = [dts > 0 if d == 0 else dts < 0 for d, _ in chains]
    incl = [dts >= 0 if d == 0 else dts <= 0 for d, _ in chains]
    r_in = [(rf_ref, rb_ref)[d] for d, _ in chains]
    v_in = [(vf_ref, vb_ref)[d] for d, _ in chains]
    kk_in = [(kkf_ref, kkb_ref)[d] for d, _ in chains]
    kd_in = [(kdf_ref, kdb_ref)[d] for d, _ in chains]
    be_in = [(bef_ref, beb_ref)[d] for d, _ in chains]
    lw_in = [(lwf_ref, lwb_ref)[d] for d, _ in chains]
    ident = jnp.where(eye, 1.0, 0.0).astype(F32)

    def each(f, *cols):
        return [f(*args) for args in zip(*cols)]

    lw = each(lambda ref, sl: ref[0, sl[0], :, sl[1]], lw_in, sls)
    def scan_rows(x, reverse):
        step = 1
        while step < CHUNK:
            if reverse:
                x = x + jnp.where(rowi < CHUNK - step, pltpu.roll(x, CHUNK - step, axis=0), 0.0)
            else:
                x = x + jnp.where(rowi >= step, pltpu.roll(x, step, axis=0), 0.0)
            step *= 2
        return x

    cum = [scan_rows(x, d == 1) for x, (d, _) in zip(lw, chains)]
    tot = each(lambda x: jnp.sum(x, axis=0, keepdims=True), lw)
    e_in = each(jnp.exp, cum)
    e_ex = each(lambda c_, l_: jnp.exp(c_ - l_), cum, lw)
    e_neg = each(lambda c_: jnp.exp(-c_), cum)
    e_rem = each(lambda t_, c_: jnp.exp(t_ - c_), tot, cum)
    gam = each(jnp.exp, tot)
    kk = each(lambda ref, sl: ref[sl[0], :, sl[1]].astype(F32), kk_in, sls)
    kd = each(lambda ref, sl: ref[0, sl[0], :, sl[1]].astype(F32), kd_in, sls)
    be = each(lambda ref, sl: ref[0, sl[0], :, sl[1]].astype(F32), be_in, sls)
    a_b = each(lambda x, e: bf(bd(x * e)), kk, e_ex)
    r_t = each(lambda ref, e, sl: bd(ref[sl[0], :, sl[1]].astype(F32) * e), r_in, e_in, sls)
    b_t = each(lambda x, e: bd(x * e), be, e_neg)
    k_t = each(lambda x, e: bd(x * e), kd, e_neg)
    b_h = each(lambda x, e: bd(x * e), be, e_rem)
    k_h = each(lambda x, e: bd(x * e), kd, e_rem)
    vv = each(lambda ref, sl: bf(bd(ref[sl[0], :, sl[1]].astype(F32))), v_in, sls)

    sc = each(lambda a, r, b, k: _mm_nt(jnp.concatenate([a, bf(r)], axis=0),
                                        bf(jnp.concatenate([b, k], axis=0))), a_b, r_t, b_t, k_t)
    lk = each(lambda m, s: bf(jnp.where(m, s[:n2, n2:], 0.0)), strict, sc)
    mb = each(lambda m, s: bf(jnp.where(m, s[n2:, :n2], 0.0)), incl, sc)
    mk = each(lambda m, s: bf(jnp.where(m, s[n2:, n2:], 0.0)), incl, sc)

    nn = each(lambda m, s: jnp.where(m, -s[:n2, :n2], 0.0), strict, sc)
    dg = each(lambda x: jnp.where(blk16, x, 0.0), nn)
    offb = each(lambda x, d_: bf(x - d_), nn, dg)
    def mm2(xs, ys):
        return each(_mm, xs, ys)

    d1b = each(bf, dg)
    d2b = each(bf, mm2(d1b, d1b))
    d4b = each(bf, mm2(d2b, d2b))
    d8b = each(bf, mm2(d4b, d4b))
    td = each(lambda d_: ident + d_, dg)
    for dpow in (d2b, d4b, d8b):
        td = each(lambda t_, m_: t_ + m_, td, mm2(each(bf, td), dpow))
    tdb = each(bf, td)
    e1 = mm2(tdb, offb)
    e1b = each(bf, e1)
    e2 = mm2(e1b, e1b)
    e3 = mm2(e1b, each(bf, e2))
    ttb = each(bf, mm2(each(lambda x1_, x2_, x3_: bf(ident + x1_ + x2_ + x3_), e1, e2, e3), tdb))

    lkv = mm2(lk, vv)
    wb = each(lambda t_, a, l_: bf(_mm(t_, jnp.concatenate([a, bf(l_)], axis=1))), ttb, a_b, lkv)
    mw = each(_mm, mb, wb)
    mkv = mm2(mk, vv)
    bw = each(lambda b, w: _mm(bf(b.T), w), b_h, wb)
    kv = mm2(each(lambda k: bf(k.T), k_h), vv)
    r_hat = each(lambda r, m: bf(r - m[:, :n2]), r_t, mw)
    y_loc = each(lambda m, w: m - w[:, n2:], mkv, mw)
    g = each(lambda g_, b: bf(jnp.where(eye, jnp.broadcast_to(g_, (n2, n2)), 0.0) - b[:, :n2]), gam, bw)
    h = each(lambda k, b: k - b[:, n2:], kv, bw)

    n_ch = len(chains)
    z = [z_ref[i] for i in range(n_ch)]
    z_hi = each(bf, z)
    z_lo = each(lambda z_, zh: bf(z_ - zh.astype(F32)), z, z_hi)
    y = each(lambda m_, yl: m_ + yl, mm2(r_hat, z_hi), y_loc)
    gz = each(lambda g_, zh, zl: _mm(g_, jnp.concatenate([zh, zl], axis=1)), g, z_hi, z_lo)
    z_new = each(lambda m_, h_: m_[:, :n2] + m_[:, n2:] + h_, gz, h)
    for i, (d, _) in enumerate(chains):
        z_ref[i] = z_new[i]
        (yf_ref, yb_ref)[d][sls[i][0], :, sls[i][1]] = (y[i][:CHUNK] + y[i][CHUNK:]).astype(BF16)


def _rwkv_scan(r, v, kk, kd, be, lw):
    b, l, dr = r.shape
    nc = l // CHUNK
    n_slabs = dr // LANES
    n_seq = RWKV_SEQS_PER_STEP if b % RWKV_SEQS_PER_STEP == 0 else 1
    fwd = pl.BlockSpec((n_seq, CHUNK, dr), lambda bi, c: (bi, c, 0))
    bwd = pl.BlockSpec((n_seq, CHUNK, dr), lambda bi, c: (bi, nc - 1 - c, 0))
    fwd_d = pl.BlockSpec((1, n_seq, CHUNK, dr), lambda bi, c: (0, bi, c, 0))
    bwd_d = pl.BlockSpec((1, n_seq, CHUNK, dr), lambda bi, c: (1, bi, nc - 1 - c, 0))
    return pl.pallas_call(
        functools.partial(_rwkv_kernel, n_slabs=n_slabs, n_seq=n_seq),
        grid=(b // n_seq, nc),
        in_specs=[fwd, fwd, fwd, fwd_d, fwd_d, fwd_d, bwd, bwd, bwd, bwd_d, bwd_d, bwd_d],
        out_specs=[fwd, bwd],
        out_shape=[jax.ShapeDtypeStruct((b, l, dr), BF16), jax.ShapeDtypeStruct((b, l, dr), BF16)],
        scratch_shapes=[pltpu.VMEM((2 * n_slabs * n_seq, HEADS_PER_SLAB * CHUNK, HEADS_PER_SLAB * HEAD_DIM),
                                   F32)],
        compiler_params=pltpu.CompilerParams(
            dimension_semantics=("parallel", "arbitrary"),
            vmem_limit_bytes=VMEM_LIMIT),
        name="rwkv_scan",
    )(r, v, kk, kd, be, lw, r, v, kk, kd, be, lw)


def _rwkv_prep_kernel(p_ref, prev_ref, next_ref, mu_ref, wdec_ref, w0_ref, wicl_ref, a0_ref, g2_ref,
                      kk_w_ref, ka_ref, rk_ref, hsum_ref,
                      r_ref, v_ref, kk_ref, kd_ref, be_ref, lw_ref, bonus_ref, gate_ref, *, tiles_per_seq):
    i = pl.program_id(0)
    p = p_ref[...].astype(F32)
    tm = p.shape[0]
    pos = i % tiles_per_seq
    prev_row = jnp.where(pos == 0, 0.0, prev_ref[...].astype(F32)[HALO_ROWS - 1:HALO_ROWS, :])
    next_row = jnp.where(pos == tiles_per_seq - 1, 0.0, next_ref[...].astype(F32)[0:1, :])
    rowi = lax.broadcasted_iota(jnp.int32, (tm, 1), 0)
    up = jnp.where(rowi == 0, prev_row, pltpu.roll(p, 1, axis=0))
    dn = jnp.where(rowi == tm - 1, next_row, pltpu.roll(p, tm - 1, axis=0))
    ps = p + mu_ref[...] * (0.5 * (up + dn) - p)

    c = np.cumsum((0,) + RWKV_SIZES)
    xr, xk, xv = ps[:, c[0]:c[1]], ps[:, c[1]:c[2]], ps[:, c[2]:c[3]]
    dw, da, dg = ps[:, c[3]:c[5]], ps[:, c[5]:c[7]], ps[:, c[7]:c[8]]
    wl = w0_ref[...] + _mm(jnp.tanh(dw).astype(BF16), wdec_ref[...])
    neg = -wl
    softplus = jnp.maximum(neg, 0.0) + jnp.log(1.0 + jnp.exp(-jnp.abs(neg)))
    lw = -jnp.exp(-softplus - 0.5)
    a = jax.nn.sigmoid(a0_ref[...] + _mm(da.astype(BF16), wicl_ref[...]))
    kkr = xk * kk_w_ref[...]
    ss = _mm((kkr * kkr).astype(BF16), hsum_ref[...])
    kk = kkr * lax.rsqrt(jnp.maximum(ss, 1e-24))
    kd_sum = jnp.zeros_like(xk)
    for d in range(2):
        a_d = a[:, d * D_RWKV:(d + 1) * D_RWKV]
        kd = xk * (1.0 + (a_d - 1.0) * ka_ref[...])
        kd_sum = kd_sum + kd
        kd_ref[d] = kd.astype(BF16)
        be_ref[d] = (a_d * kk).astype(BF16)
        lw_ref[d] = lw[:, d * D_RWKV:(d + 1) * D_RWKV]
    r_ref[...] = xr.astype(BF16)
    v_ref[...] = xv.astype(BF16)
    kk_ref[...] = kk.astype(BF16)
    bonus_ref[...] = (_mm((xr * rk_ref[...] * kd_sum).astype(BF16), hsum_ref[...]) * xv).astype(BF16)
    gate_ref[...] = _mm(jax.nn.sigmoid(dg).astype(BF16), g2_ref[...]).astype(BF16)


def _rwkv_branch(p_rwkv, b, l, prm, tm):
    t = b * l
    r, v, kk, kd, be, lw, bonus, gate = _rwkv_prep(p_rwkv, l, prm, tm)
    seq = lambda a: a.reshape(b, l, D_RWKV)
    seq2 = lambda a: a.reshape(2, b, l, D_RWKV)
    wkv_f, wkv_b = _rwkv_scan(seq(r), seq(v), seq(kk), seq2(kd), seq2(be), seq2(lw))
    return wkv_f.reshape(t, D_RWKV), wkv_b.reshape(t, D_RWKV), bonus, gate


def _block_diag2(w):
    z = jnp.zeros_like(w[0])
    return jnp.concatenate([jnp.concatenate([w[0], z], axis=1), jnp.concatenate([z, w[1]], axis=1)], axis=0)


def _head_sum_matrix(scale):
    h = np.arange(D_RWKV) // HEAD_DIM
    return jnp.asarray((h[:, None] == h[None, :]) * scale, BF16)


def _rwkv_prep(p_rwkv, seq_len, prm, tm):
    t = p_rwkv.shape[0]
    row2 = lambda a: a.reshape(1, -1).astype(F32)
    consts = [row2(prm['mu_shift']), _block_diag2(prm['w2']).astype(BF16), row2(prm['w0']),
              _block_diag2(prm['a2']).astype(BF16), row2(prm['a0']), prm['g2'].astype(BF16),
              row2(prm['k_k']), row2(prm['k_a']), row2(prm['r_k']), _head_sum_matrix(1.0)]
    full = lambda a: pl.BlockSpec(a.shape, lambda i: (0, 0))
    row = pl.BlockSpec((tm, D_RWKV), lambda i: (i, 0))
    row_d = pl.BlockSpec((2, tm, D_RWKV), lambda i: (0, i, 0))
    halo = tm // HALO_ROWS
    sd = lambda dt: jax.ShapeDtypeStruct((t, D_RWKV), dt)
    sd2 = lambda dt: jax.ShapeDtypeStruct((2, t, D_RWKV), dt)
    return pl.pallas_call(
        functools.partial(_rwkv_prep_kernel, tiles_per_seq=seq_len // tm),
        grid=(t // tm,),
        in_specs=[pl.BlockSpec((tm, RWKV_IN), lambda i: (i, 0)),
                  pl.BlockSpec((HALO_ROWS, RWKV_IN), lambda i: (jnp.maximum(i * halo - 1, 0), 0)),
                  pl.BlockSpec((HALO_ROWS, RWKV_IN),
                               lambda i: (jnp.minimum((i + 1) * halo, t // HALO_ROWS - 1), 0))]
        + [full(a) for a in consts],
        out_specs=[row, row, row, row_d, row_d, row_d, row, row],
        out_shape=[sd(BF16), sd(BF16), sd(BF16), sd2(BF16), sd2(BF16), sd2(F32), sd(BF16), sd(BF16)],
        compiler_params=pltpu.CompilerParams(dimension_semantics=("parallel",),
                                             vmem_limit_bytes=VMEM_LIMIT),
        name="rwkv_prep",
    )(p_rwkv, p_rwkv, p_rwkv, *consts)


def _na_bias_table(rpb):
    n_h = rpb.shape[0]
    var = np.arange(NA_WIN_H)[:, None]
    i = np.arange(NA_WIN_H)[None, :]
    c = np.arange(GRID_W)[:, None]
    kc = np.arange(GRID_W)[None, :]
    cs = np.clip(c - NA_WIN_W // 2, 0, GRID_W - NA_WIN_W)
    valid = (kc >= cs) & (kc < cs + NA_WIN_W)
    row_sel = (np.arange(2 * NA_WIN_H - 1)[None, None, :] == (i - var + NA_WIN_H - 1)[:, :, None])
    col_sel = (np.arange(2 * NA_WIN_W - 1)[None, None, :] == (kc - c + NA_WIN_W - 1)[:, :, None])
    col_sel = col_sel & valid[:, :, None]
    tab = jnp.einsum('hab,via,ckb->vhcik', rpb.astype(F32), row_sel.astype(np.float32),
                     col_sel.astype(np.float32), precision=HI)
    tab = jnp.where(valid[None, None, :, None, :], tab, NEG_BIG)
    return tab.reshape(NA_WIN_H, n_h * GRID_W, NA_WIN_H * GRID_W)


def _na_kernel(q_ref, k_ref, v_ref, *refs, rows, n_slabs):
    bias_refs, o_ref = refs[:-1], refs[-1]
    band = NA_WIN_H * GRID_W
    scale = HEAD_DIM ** -0.5
    lane = lax.broadcasted_iota(jnp.int32, (GRID_W, LANES), 1)
    head0 = lane < HEAD_DIM
    chains = [(i, p) for i in range(NA_ROWS_PER_STEP) for p in range(n_slabs)]
    starts = []
    for i in range(NA_ROWS_PER_STEP):
        r = pl.program_id(1) * NA_ROWS_PER_STEP + i
        rs = jnp.clip(r - NA_WIN_H // 2, 0, rows - NA_WIN_H)
        starts.append(pl.multiple_of(rs * GRID_W, GRID_W))

    def lanes(p):
        return slice(p * LANES, (p + 1) * LANES)

    def split_heads(q2):
        zero = jnp.zeros_like(q2)
        return jnp.concatenate([jnp.where(head0, q2, zero), jnp.where(head0, zero, q2)], axis=0)

    lhs = [split_heads(q_ref[0, i * GRID_W:(i + 1) * GRID_W, lanes(p)]) for i, p in chains]
    s = [_mm_nt(x, k_ref[0, pl.ds(starts[i], band), lanes(p)]) for x, (i, p) in zip(lhs, chains)]
    s = [x * scale + bias_refs[i][0, p * 2 * GRID_W:(p + 1) * 2 * GRID_W, :] for x, (i, p) in zip(s, chains)]
    m = [jnp.max(x, axis=-1, keepdims=True) for x in s]
    e = [jnp.exp(x - mx) for x, mx in zip(s, m)]
    den = [jnp.sum(x, axis=-1, keepdims=True) for x in e]
    o = [_mm(x.astype(BF16), v_ref[0, pl.ds(starts[i], band), lanes(p)]) for x, (i, p) in zip(e, chains)]
    for x, d, (i, p) in zip(o, den, chains):
        x = x / d
        o_ref[0, i * GRID_W:(i + 1) * GRID_W, lanes(p)] = (
            jnp.where(head0, x[:GRID_W], x[GRID_W:]).astype(o_ref.dtype))


def _na_attention(q, k, v, bias_tab):
    b, l, dn = q.shape
    rows = l // GRID_W
    assert rows >= NA_WIN_H and rows % NA_ROWS_PER_STEP == 0
    n_slabs = dn // LANES
    qrows = NA_ROWS_PER_STEP * GRID_W

    def bias_spec(i):
        def bias_idx(bi, g):
            r = g * NA_ROWS_PER_STEP + i
            rs = jnp.clip(r - NA_WIN_H // 2, 0, rows - NA_WIN_H)
            return (r - rs, 0, 0)
        return pl.BlockSpec((1,) + bias_tab.shape[1:], bias_idx)

    return pl.pallas_call(
        functools.partial(_na_kernel, rows=rows, n_slabs=n_slabs),
        grid=(b, rows // NA_ROWS_PER_STEP),
        in_specs=[pl.BlockSpec((1, qrows, dn), lambda bi, g: (bi, g, 0)),
                  pl.BlockSpec((1, l, dn), lambda bi, g: (bi, 0, 0)),
                  pl.BlockSpec((1, l, dn), lambda bi, g: (bi, 0, 0))]
        + [bias_spec(i) for i in range(NA_ROWS_PER_STEP)],
        out_specs=pl.BlockSpec((1, qrows, dn), lambda bi, g: (bi, g, 0)),
        out_shape=jax.ShapeDtypeStruct((b, l, dn), BF16),
        compiler_params=pltpu.CompilerParams(dimension_semantics=("parallel", "arbitrary"),
                                             vmem_limit_bytes=VMEM_LIMIT),
        name="na_attention",
    )(q, k, v, *([bias_tab] * NA_ROWS_PER_STEP))


def _to_token_tiles(dst_ref, x):
    n = x.shape[0]
    for s in range(SLABS):
        lo = _bf16_bits(x[:, s * LANES:(s + 1) * LANES])
        hi = _bf16_bits(x[:, (s + SLABS) * LANES:(s + SLABS + 1) * LANES])
        dst_ref[pl.ds(s, n, stride=SLABS), :] = (lo >> 16) | hi


def _from_token_tiles(src_ref, n):
    words = [src_ref[pl.ds(s, n, stride=SLABS), :] for s in range(SLABS)]
    lo = [lax.bitcast_convert_type(w << 16, F32) for w in words]
    hi = [lax.bitcast_convert_type(w & U32(0xFFFF0000), F32) for w in words]
    return jnp.concatenate(lo + hi, axis=1)


def _bf16_bits(x):
    b = lax.bitcast_convert_type(x, U32)
    return (b + U32(0x7FFF) + ((b >> 16) & U32(1))) & U32(0xFFFF0000)


def _merge_kernel(wf_ref, wb_ref, bonus_ref, gate_ref, havg_ref, lg_ref, lb_ref,
                  yn_ref, ga_ref, gn_ref, xn_ref, wa_ref, wn_ref, wo_ref, g1_ref, b1_ref,
                  wrt_ref, x1_ref, x1t_ref, sct_ref):
    wkv = wf_ref[...].astype(F32) + wb_ref[...].astype(F32)
    xc = wkv - _mm(wkv.astype(BF16), havg_ref[...])
    var = _mm((xc * xc).astype(BF16), havg_ref[...])
    ya = ((xc * lax.rsqrt(var + GN_EPS) * lg_ref[...] + lb_ref[...] + bonus_ref[...].astype(F32))
          * gate_ref[...].astype(F32))
    up_a = _mm(ya.astype(BF16), wa_ref[...])
    up_n = _mm(yn_ref[...], wn_ref[...])
    merged = (jax.nn.sigmoid(ga_ref[...].astype(F32)) * up_a
              + jax.nn.sigmoid(gn_ref[...].astype(F32)) * up_n)
    mix = _mm(merged.astype(BF16), wo_ref[...])
    x1 = _ln(ALPHA * xn_ref[...] + mix, g1_ref[...], b1_ref[...])
    x1_ref[...] = x1
    _to_token_tiles(x1t_ref, x1)
    w_hi, w_lo = wrt_ref[0], wrt_ref[1]
    x_hi = x1.astype(BF16)
    x_lo = (x1 - x_hi.astype(F32)).astype(BF16)
    sct_ref[...] = jax.nn.sigmoid(_mm_nt(w_hi, x_hi) + _mm_nt(w_hi, x_lo) + _mm_nt(w_lo, x_hi))


def _merge(wkv_f, wkv_b, bonus, gate, lnx_g, lnx_b, yn, ga, gn, xn, w_up_a, w_up_n, w_out, ln1_g, ln1_b,
           w_router_t, tm):
    t = xn.shape[0]
    row = lambda n: pl.BlockSpec((tm, n), lambda i: (i, 0))
    full = lambda a: pl.BlockSpec(a.shape, lambda i: (0,) * a.ndim)
    havg = _head_sum_matrix(1.0 / HEAD_DIM)
    return pl.pallas_call(
        _merge_kernel,
        grid=(t // tm,),
        in_specs=[row(D_RWKV), row(D_RWKV), row(D_RWKV), row(D_RWKV), full(havg), full(lnx_g), full(lnx_b),
                  row(D_NA), row(D_MODEL), row(D_MODEL), row(D_MODEL),
                  full(w_up_a), full(w_up_n), full(w_out), full(ln1_g), full(ln1_b), full(w_router_t)],
        out_specs=[row(D_MODEL), pl.BlockSpec((tm * SLABS, LANES), lambda i: (i, 0)),
                   pl.BlockSpec((N_EXPERTS, tm), lambda i: (0, i))],
        out_shape=[jax.ShapeDtypeStruct((t, D_MODEL), F32),
                   jax.ShapeDtypeStruct((t * SLABS, LANES), U32),
                   jax.ShapeDtypeStruct((N_EXPERTS, t), F32)],
        compiler_params=pltpu.CompilerParams(dimension_semantics=("parallel",),
                                             vmem_limit_bytes=VMEM_LIMIT),
        name="merge_ln1_router",
    )(wkv_f, wkv_b, bonus, gate, havg, lnx_g, lnx_b, yn, ga, gn, xn, w_up_a, w_up_n, w_out, ln1_g, ln1_b,
      w_router_t)


def _first_argmax(vals, iota):
    m = jnp.max(vals, axis=0, keepdims=True)
    first = jnp.min(jnp.where(vals == m, iota, float(vals.shape[0])), axis=0, keepdims=True)
    return m, first


def _route_kernel(sct_ref, bias_ref, idx_ref, wts_ref, cnt_ref):
    s = sct_ref[...]
    tm = s.shape[1]
    sel = s + bias_ref[...]
    gsz = N_EXPERTS // N_GROUPS
    iota_g = lax.broadcasted_iota(jnp.int32, (gsz, tm), 0).astype(F32)
    iota_8 = lax.broadcasted_iota(jnp.int32, (N_GROUPS, tm), 0).astype(F32)
    iota_e = lax.broadcasted_iota(jnp.int32, (N_EXPERTS, tm), 0).astype(F32)

    gs = jnp.zeros((N_GROUPS, tm), F32)
    for g in range(N_GROUPS):
        blk = sel[g * gsz:(g + 1) * gsz, :]
        m1, i1 = _first_argmax(blk, iota_g)
        m2 = jnp.max(jnp.where(iota_g == i1, -jnp.inf, blk), axis=0, keepdims=True)
        gs = jnp.where(iota_8 == float(g), m1 + m2, gs)

    chosen = jnp.zeros((N_GROUPS, tm), F32)
    for _ in range(TOPK_GROUPS):
        _, gi = _first_argmax(gs, iota_8)
        hit = iota_8 == gi
        chosen = jnp.where(hit, 1.0, chosen)
        gs = jnp.where(hit, -jnp.inf, gs)
    mask_e = jnp.concatenate([jnp.broadcast_to(chosen[g:g + 1, :], (gsz, tm)) for g in range(N_GROUPS)],
                             axis=0)
    cand = jnp.where(mask_e > 0.0, sel, -jnp.inf)

    iota_k = lax.broadcasted_iota(jnp.int32, (TOP_K, tm), 0)
    idx = jnp.zeros((TOP_K, tm), F32)
    wts = jnp.zeros((TOP_K, tm), F32)
    member = jnp.zeros((N_EXPERTS, tm), F32)
    for j in range(TOP_K):
        _, ij = _first_argmax(cand, iota_e)
        hit = iota_e == ij
        wj = jnp.sum(jnp.where(hit, s, 0.0), axis=0, keepdims=True)
        cand = jnp.where(hit, -jnp.inf, cand)
        member = jnp.where(hit, 1.0, member)
        idx = jnp.where(iota_k == j, ij, idx)
        wts = jnp.where(iota_k == j, wj, wts)
    wts = wts / jnp.sum(wts, axis=0, keepdims=True) * ROUTED_SCALE
    idx_ref[...] = idx.astype(jnp.int32)
    wts_ref[...] = wts
    cnt_ref[0] = _mm_nt(jnp.ones((8, tm), BF16), member.astype(BF16))


def _route(sct, e_bias, tm):
    n_e, t = sct.shape
    nt = t // tm
    return pl.pallas_call(
        _route_kernel,
        grid=(nt,),
        in_specs=[pl.BlockSpec((n_e, tm), lambda i: (0, i)),
                  pl.BlockSpec((n_e, 1), lambda i: (0, 0))],
        out_specs=[pl.BlockSpec((TOP_K, tm), lambda i: (0, i)),
                   pl.BlockSpec((TOP_K, tm), lambda i: (0, i)),
                   pl.BlockSpec((1, 8, n_e), lambda i: (i, 0, 0))],
        out_shape=[jax.ShapeDtypeStruct((TOP_K, t), jnp.int32),
                   jax.ShapeDtypeStruct((TOP_K, t), F32),
                   jax.ShapeDtypeStruct((nt, 8, n_e), F32)],
        compiler_params=pltpu.CompilerParams(dimension_semantics=("parallel",),
                                             vmem_limit_bytes=VMEM_LIMIT),
        name="moe_route",
    )(sct, e_bias.reshape(n_e, 1).astype(F32))


def _dest_kernel(idx_ref, base_ref, dest_ref):
    idx = idx_ref[...].astype(F32)
    tm = idx.shape[1]
    iota_e = lax.broadcasted_iota(jnp.int32, (N_EXPERTS, tm), 0).astype(F32)
    hits = [iota_e == idx[j:j + 1, :] for j in range(TOP_K)]
    member = jnp.zeros((N_EXPERTS, tm), F32)
    for hit in hits:
        member = jnp.where(hit, 1.0, member)
    r = lax.broadcasted_iota(jnp.int32, (tm, tm), 0)
    c = lax.broadcasted_iota(jnp.int32, (tm, tm), 1)
    earlier = jnp.where(r < c, 1.0, 0.0).astype(BF16)
    rank = _mm(member.astype(BF16), earlier) + base_ref[0]
    iota_k = lax.broadcasted_iota(jnp.int32, (TOP_K, tm), 0)
    dest = jnp.zeros((TOP_K, tm), F32)
    for j, hit in enumerate(hits):
        dj = jnp.sum(jnp.where(hit, rank, 0.0), axis=0, keepdims=True)
        dest = jnp.where(iota_k == j, dj, dest)
    dest_ref[0] = dest.astype(jnp.int32)


def _dest_rows(idx, base, tm):
    t = idx.shape[1]
    nt = t // tm
    return pl.pallas_call(
        _dest_kernel,
        grid=(nt,),
        in_specs=[pl.BlockSpec((TOP_K, tm), lambda i: (0, i)),
                  pl.BlockSpec((1, N_EXPERTS, 1), lambda i: (i, 0, 0))],
        out_specs=pl.BlockSpec((1, TOP_K, tm), lambda i: (i, 0, 0)),
        out_shape=jax.ShapeDtypeStruct((nt, TOP_K, tm), jnp.int32),
        compiler_params=pltpu.CompilerParams(dimension_semantics=("parallel",),
                                             vmem_limit_bytes=VMEM_LIMIT),
        name="moe_dest",
    )(idx, base)


def _block_plan(tile_counts, n_tokens):
    n_blocks = (n_tokens * TOP_K + N_EXPERTS * (MOE_ROWS - 1) + MOE_ROWS - 1) // MOE_ROWS
    n_blocks = -(-n_blocks // MOE_GROUP) * MOE_GROUP
    counts = jnp.sum(tile_counts, axis=0)
    padded = (counts + MOE_ROWS - 1) // MOE_ROWS * MOE_ROWS
    extra = (-(jnp.sum(padded) // MOE_ROWS)) % MOE_GROUP
    padded = padded.at[N_EXPERTS - 1].add(extra * MOE_ROWS)
    pend = jnp.cumsum(padded)
    pstart = pend - padded
    tile_base = pstart[None, :] + jnp.cumsum(tile_counts, axis=0) - tile_counts
    block_start = jnp.arange(n_blocks, dtype=jnp.int32) * MOE_ROWS
    block_e = jnp.minimum(jnp.sum(pend[None, :] <= block_start[:, None], axis=1),
                          N_EXPERTS - 1).astype(jnp.int32)
    n_used = (pend[-1] // MOE_ROWS).astype(jnp.int32).reshape(1)
    return (tile_base.astype(F32)[:, :, None], counts.astype(jnp.int32), padded.astype(jnp.int32),
            pstart.astype(jnp.int32), block_e, n_used, n_blocks)


def _row_tile(ref, row):
    return ref.at[pl.ds(pl.multiple_of(row * SLABS, SLABS), SLABS)]


def _dispatch_kernel(cnt_ref, pad_ref, pst_ref, dest_hbm, x_ref, xs_hbm, dest_s, zrow, isem, csem, zsem,
                     *, tm):
    i = pl.program_id(0)
    n = pl.num_programs(0)
    cur = i % 2

    def idx_copy(tile, buf):
        return pltpu.make_async_copy(dest_hbm.at[tile], dest_s.at[buf], isem.at[buf])

    @pl.when(i == 0)
    def _():
        idx_copy(0, 0).start()
        zrow[...] = jnp.zeros_like(zrow)

        def per_expert(e, carry, wait):
            def per_row(r, c2):
                cp = pltpu.make_async_copy(zrow, _row_tile(xs_hbm, pst_ref[e] + r), zsem)
                if wait:
                    cp.wait()
                else:
                    cp.start()
                return c2
            return lax.fori_loop(cnt_ref[e], pad_ref[e], per_row, carry)

        lax.fori_loop(0, N_EXPERTS, functools.partial(per_expert, wait=False), 0)
        lax.fori_loop(0, N_EXPERTS, functools.partial(per_expert, wait=True), 0)

    idx_copy(i, cur).wait()

    @pl.when(i + 1 < n)
    def _():
        idx_copy(i + 1, 1 - cur).start()

    def per_token(t, carry):
        src = x_ref.at[pl.ds(pl.multiple_of(t * SLABS, SLABS), SLABS)]
        for j in range(TOP_K):
            pltpu.make_async_copy(src, _row_tile(xs_hbm, dest_s[cur, j, t]), csem).start()
        return carry

    lax.fori_loop(0, tm, per_token, 0)
    for j in range(TOP_K):
        pltpu.make_async_copy(x_ref, xs_hbm.at[pl.ds(0, tm * SLABS)], csem).wait()


def _dispatch(x1t, dest, counts, padded, pstart, n_blocks, tm):
    nt = dest.shape[0]
    any_spec = pl.BlockSpec(memory_space=pl.ANY)
    grid_spec = pltpu.PrefetchScalarGridSpec(
        num_scalar_prefetch=3,
        grid=(nt,),
        in_specs=[any_spec, pl.BlockSpec((tm * SLABS, LANES), lambda i, c, p, s: (i, 0))],
        out_specs=any_spec,
        scratch_shapes=[pltpu.SMEM((2, TOP_K, tm), jnp.int32),
                        pltpu.VMEM((SLABS, LANES), U32),
                        pltpu.SemaphoreType.DMA((2,)),
                        pltpu.SemaphoreType.DMA(()),
                        pltpu.SemaphoreType.DMA(())])
    return pl.pallas_call(
        functools.partial(_dispatch_kernel, tm=tm),
        grid_spec=grid_spec,
        out_shape=jax.ShapeDtypeStruct((n_blocks * MOE_ROWS * SLABS, LANES), U32),
        compiler_params=pltpu.CompilerParams(dimension_semantics=("arbitrary",),
                                             vmem_limit_bytes=VMEM_LIMIT),
        name="moe_dispatch",
    )(counts, padded, pstart, dest, x1t)


def _swiglu(xb, wg, wu, wd):
    hg = _mm(xb, wg)
    hu = _mm(xb, wu)
    return _mm((hg * jax.nn.sigmoid(hg) * hu).astype(BF16), wd)


def _experts_kernel(be_ref, nu_ref, xs_ref, *refs):
    w_refs, ys_ref = refs[:-1], refs[-1]
    s = pl.program_id(0)
    blk = MOE_ROWS * SLABS

    @pl.when(MOE_GROUP * s < nu_ref[0])
    def _():
        parts = [pl.ds(k * blk, blk) for k in range(MOE_GROUP)]
        wgu = [w_refs[2 * k] for k in range(MOE_GROUP)]
        wd = [w_refs[2 * k + 1] for k in range(MOE_GROUP)]
        xb, hh = {}, {}
        for step in range(MOE_GROUP + 2):
            k = step
            if k < MOE_GROUP:
                xb[k] = _from_token_tiles(xs_ref.at[parts[k]], MOE_ROWS).astype(BF16)
            k = step - 1
            if 0 <= k < MOE_GROUP:
                hgu = _mm(xb[k], wgu[k][0])
                hg, hu = hgu[:, :D_EXPERT], hgu[:, D_EXPERT:]
                hh[k] = (hg * jax.nn.sigmoid(hg) * hu).astype(BF16)
            k = step - 2
            if 0 <= k < MOE_GROUP:
                _to_token_tiles(ys_ref.at[parts[k]], _mm(hh[k], wd[k][0]))

    @pl.when(MOE_GROUP * s >= nu_ref[0])
    def _():
        ys_ref[...] = jnp.zeros_like(ys_ref)


def _experts(xs, block_e, n_used, w_gate_up_e, w_down_e):
    n_groups = block_e.shape[0] // MOE_GROUP
    blkg = MOE_GROUP * MOE_ROWS * SLABS
    wspec = lambda shp, k: pl.BlockSpec((1,) + shp, lambda s, be, nu: (be[MOE_GROUP * s + k], 0, 0))
    wspecs = [wspec(shp, k) for k in range(MOE_GROUP)
              for shp in ((D_MODEL, 2 * D_EXPERT), (D_EXPERT, D_MODEL))]
    grid_spec = pltpu.PrefetchScalarGridSpec(
        num_scalar_prefetch=2,
        grid=(n_groups,),
        in_specs=[pl.BlockSpec((blkg, LANES),
                               lambda s, be, nu: (jnp.minimum(s, nu[0] // MOE_GROUP - 1), 0))] + wspecs,
        out_specs=pl.BlockSpec((blkg, LANES), lambda s, be, nu: (s, 0)))
    return pl.pallas_call(
        _experts_kernel,
        grid_spec=grid_spec,
        out_shape=jax.ShapeDtypeStruct(xs.shape, xs.dtype),
        compiler_params=pltpu.CompilerParams(dimension_semantics=("arbitrary",),
                                             vmem_limit_bytes=VMEM_LIMIT),
        name="moe_experts",
    )(block_e, n_used, xs, *([w_gate_up_e, w_down_e] * MOE_GROUP))


def _final_kernel(dest_hbm, ys_hbm, x1_ref, wts_ref, wg_ref, wu_ref, wd_ref, g2_ref, b2_ref, o_ref,
                  dest_s, gbuf, isem, gsem, *, tm):
    i = pl.program_id(0)
    n = pl.num_programs(0)
    cur = i % 2
    nxt = 1 - cur

    def idx_copy(tile, buf):
        return pltpu.make_async_copy(dest_hbm.at[tile], dest_s.at[buf], isem.at[buf])

    def start_gather(buf):
        def per_token(t, carry):
            for j in range(TOP_K):
                dst = gbuf.at[buf, j, pl.ds(pl.multiple_of(t * SLABS, SLABS), SLABS)]
                pltpu.make_async_copy(_row_tile(ys_hbm, dest_s[buf, j, t]), dst, gsem.at[buf]).start()
            return carry
        lax.fori_loop(0, tm, per_token, 0)

    @pl.when(i == 0)
    def _():
        idx_copy(0, 0).start()
        idx_copy(0, 0).wait()
        start_gather(0)

        @pl.when(1 < n)
        def _():
            idx_copy(1, 1).start()

    @pl.when(i + 1 < n)
    def _():
        idx_copy(i + 1, nxt).wait()
        start_gather(nxt)

        @pl.when(i + 2 < n)
        def _():
            idx_copy(i + 2, cur).start()

    x1 = x1_ref[...]
    y = _swiglu(x1.astype(BF16), wg_ref[...], wu_ref[...], wd_ref[...])
    r = lax.broadcasted_iota(jnp.int32, (tm, tm), 0)
    c = lax.broadcasted_iota(jnp.int32, (tm, tm), 1)
    w_cols = _mm_nt(jnp.where(r == c, 1.0, 0.0).astype(F32), wts_ref[...], HI)
    for j in range(TOP_K):
        pltpu.make_async_copy(ys_hbm.at[pl.ds(0, tm * SLABS)], gbuf.at[cur, j], gsem.at[cur]).wait()
    for j in range(TOP_K):
        y = y + w_cols[:, j:j + 1] * _from_token_tiles(gbuf.at[cur, j], tm)
    o_ref[...] = _ln(ALPHA * x1 + y, g2_ref[...], b2_ref[...])


def _final(dest, ys, x1, wts, w_gate_s, w_up_s, w_down_s, ln2_g, ln2_b, tm):
    t = x1.shape[0]
    any_spec = pl.BlockSpec(memory_space=pl.ANY)
    row = pl.BlockSpec((tm, D_MODEL), lambda i: (i, 0))
    full = lambda a: pl.BlockSpec(a.shape, lambda i: (0, 0))
    return pl.pallas_call(
        functools.partial(_final_kernel, tm=tm),
        grid=(t // tm,),
        in_specs=[any_spec, any_spec, row, pl.BlockSpec((TOP_K, tm), lambda i: (0, i)),
                  full(w_gate_s), full(w_up_s), full(w_down_s), full(ln2_g), full(ln2_b)],
        out_specs=row,
        out_shape=jax.ShapeDtypeStruct((t, D_MODEL), F32),
        scratch_shapes=[pltpu.SMEM((2, TOP_K, tm), jnp.int32),
                        pltpu.VMEM((2, TOP_K, tm * SLABS, LANES), U32),
                        pltpu.SemaphoreType.DMA((2,)),
                        pltpu.SemaphoreType.DMA((2,))],
        compiler_params=pltpu.CompilerParams(dimension_semantics=("arbitrary",),
                                             vmem_limit_bytes=VMEM_LIMIT),
        name="shared_combine_ln2",
    )(dest, ys, x1, wts, w_gate_s, w_up_s, w_down_s, ln2_g, ln2_b)


def _trunk(x, prm):
    b, l, d = x.shape
    t = b * l
    tm = 512
    row2 = lambda a: a.reshape(1, -1).astype(F32)
    xn, p_rwkv, q, k, v, ga, gn = _ln_proj(x.reshape(t, d), row2(prm['ln_in_g']), row2(prm['ln_in_b']),
                                           prm['w_in_rwkv'], prm['w_in_na'], prm['w_in_gate'], tm)
    wkv_f, wkv_b, bonus, gate = _rwkv_branch(p_rwkv, b, l, prm, tm)
    y_n = _na_attention(q.reshape(b, l, D_NA), k.reshape(b, l, D_NA), v.reshape(b, l, D_NA),
                        prm['na_bias'])
    x1, x1t, sct = _merge(wkv_f, wkv_b, bonus, gate, row2(prm['lnx_g']), row2(prm['lnx_b']),
                          y_n.reshape(t, D_NA), ga, gn, xn, prm['w_up_a'], prm['w_up_n'], prm['w_out'],
                          row2(prm['ln1_g']), row2(prm['ln1_b']), prm['w_router_t'], tm)
    return _moe(x1, x1t, sct, prm).reshape(b, l, d)


def _moe(x1, x1t, sct, prm):
    t = x1.shape[0]
    tm = MOE_TILE
    idx, wts, tile_cnt = _route(sct, prm['e_bias'], tm)
    tile_base, counts, padded, pstart, block_e, n_used, n_blocks = _block_plan(
        tile_cnt[:, 0, :].astype(jnp.int32), t)
    dest = _dest_rows(idx, tile_base, tm)
    xs = _dispatch(x1t, dest, counts, padded, pstart, n_blocks, tm)
    ys = _experts(xs, block_e, n_used, prm['w_gate_up_e'], prm['w_down_e'])
    row2 = lambda a: a.reshape(1, -1).astype(F32)
    return _final(dest, ys, x1, wts, prm['w_gate_s'], prm['w_up_s'], prm['w_down_s'],
                  row2(prm['ln2_g']), row2(prm['ln2_b']), tm)


def _split_bf16(w):
    hi = w.astype(BF16)
    return jnp.stack([hi, (w - hi.astype(F32)).astype(BF16)])


def kernel(x_prompt, x_sample, ln_in_g, ln_in_b, w_in, mu_shift, w0, w2, a0, a2, g2, k_k, k_a, r_k,
           lnx_g, lnx_b, rpb, w_up_a, w_up_n, w_out, ln1_g, ln1_b, w_router, e_bias,
           w_gate_e, w_up_e, w_down_e, w_gate_s, w_up_s, w_down_s, ln2_g, ln2_b):
    assert w_in.shape[0] == DEPTH == 1
    w_in0 = w_in[0].astype(BF16)
    prm = dict(
        ln_in_g=ln_in_g, ln_in_b=ln_in_b,
        w_in_rwkv=w_in0[:, :RWKV_IN], w_in_na=w_in0[:, RWKV_IN:RWKV_IN + NA_IN],
        w_in_gate=w_in0[:, RWKV_IN + NA_IN:],
        mu_shift=mu_shift[0], w0=w0[0], w2=w2[0], a0=a0[0], a2=a2[0], g2=g2[0], k_k=k_k[0], k_a=k_a[0],
        r_k=r_k[0], lnx_g=lnx_g[0], lnx_b=lnx_b[0], na_bias=_na_bias_table(rpb[0]),
        w_up_a=w_up_a[0].astype(BF16), w_up_n=w_up_n[0].astype(BF16), w_out=w_out[0].astype(BF16),
        ln1_g=ln1_g[0], ln1_b=ln1_b[0], w_router_t=_split_bf16(w_router[0].T), e_bias=e_bias[0],
        w_gate_up_e=jnp.concatenate([w_gate_e[0].astype(BF16), w_up_e[0].astype(BF16)], axis=-1),
        w_down_e=w_down_e[0].astype(BF16),
        w_gate_s=w_gate_s[0].astype(BF16), w_up_s=w_up_s[0].astype(BF16),
        w_down_s=w_down_s[0].astype(BF16), ln2_g=ln2_g[0], ln2_b=ln2_b[0])
    return (_trunk(x_prompt, prm), _trunk(x_sample, prm))
```

```python
import functools

import jax
import jax.numpy as jnp
import numpy as np
from jax import lax
from jax.experimental import pallas as pl
from jax.experimental.pallas import tpu as pltpu

F32 = jnp.float32
BF16 = jnp.bfloat16

D_MODEL = 1024
GRID_W = 64
HEAD_DIM = 64
D_RWKV = 512
D_NA = 512
NA_WIN_H = 8
NA_WIN_W = 16
DECAY_LORA = 64
ICL_LORA = 64
GATE_LORA = 128
N_EXPERTS = 256
TOP_K = 8
N_GROUPS = 8
TOPK_GROUPS = 4
D_EXPERT = 256
ROUTED_SCALE = 2.5
LN_EPS = 1e-5
GN_EPS = 64e-5
DEPTH = 1
ALPHA = (2 * DEPTH) ** 0.25
RWKV_SIZES = (D_RWKV, D_RWKV, D_RWKV, DECAY_LORA, DECAY_LORA, ICL_LORA, ICL_LORA, GATE_LORA)
RWKV_IN = sum(RWKV_SIZES)
NA_IN = 3 * D_NA

LANES = 128
HEADS_PER_SLAB = LANES // HEAD_DIM
SLABS = D_MODEL // LANES // 2
U32 = jnp.uint32
VMEM_LIMIT = 48 * 1024 * 1024

CHUNK = 64
RWKV_SEQS_PER_STEP = 2
HALO_ROWS = 16
NA_ROWS_PER_STEP = 2
MOE_ROWS = 128
MOE_TILE = 512
MOE_GROUP = 8
NEG_BIG = -1e30

HI = lax.Precision.HIGHEST


def _mm(a, b, precision=None):
    return jnp.dot(a, b, preferred_element_type=F32, precision=precision)


def _mm_nt(a, b, precision=None):
    return lax.dot_general(a, b, (((1,), (1,)), ((), ())), preferred_element_type=F32,
                           precision=precision)


def _ln(x, g, b):
    mu = jnp.mean(x, -1, keepdims=True)
    xc = x - mu
    var = jnp.mean(xc * xc, -1, keepdims=True)
    return xc * lax.rsqrt(var + LN_EPS) * g + b


def _ln_proj_kernel(x_ref, g_ref, b_ref, wr_ref, wn_ref, wg_ref,
                    xn_ref, pr_ref, q_ref, k_ref, v_ref, ga_ref, gn_ref):
    xn = _ln(x_ref[...], g_ref[...], b_ref[...])
    xn_ref[...] = xn
    xb = xn.astype(BF16)
    pr_ref[...] = _mm(xb, wr_ref[...]).astype(BF16)
    pn = _mm(xb, wn_ref[...])
    q_ref[...] = pn[:, :D_NA].astype(BF16)
    k_ref[...] = pn[:, D_NA:2 * D_NA].astype(BF16)
    v_ref[...] = pn[:, 2 * D_NA:].astype(BF16)
    pg = _mm(xb, wg_ref[...])
    ga_ref[...] = pg[:, :D_MODEL].astype(BF16)
    gn_ref[...] = pg[:, D_MODEL:].astype(BF16)


def _ln_proj(x, g, b, w_rwkv, w_na, w_gate, tm):
    t = x.shape[0]
    row = lambda n: pl.BlockSpec((tm, n), lambda i: (i, 0))
    full = lambda a: pl.BlockSpec(a.shape, lambda i: (0, 0))
    return pl.pallas_call(
        _ln_proj_kernel,
        grid=(t // tm,),
        in_specs=[row(D_MODEL), full(g), full(b), full(w_rwkv), full(w_na), full(w_gate)],
        out_specs=[row(D_MODEL), row(RWKV_IN), row(D_NA), row(D_NA), row(D_NA),
                   row(D_MODEL), row(D_MODEL)],
        out_shape=[jax.ShapeDtypeStruct((t, D_MODEL), F32),
                   jax.ShapeDtypeStruct((t, RWKV_IN), BF16),
                   jax.ShapeDtypeStruct((t, D_NA), BF16),
                   jax.ShapeDtypeStruct((t, D_NA), BF16),
                   jax.ShapeDtypeStruct((t, D_NA), BF16),
                   jax.ShapeDtypeStruct((t, D_MODEL), BF16),
                   jax.ShapeDtypeStruct((t, D_MODEL), BF16)],
        compiler_params=pltpu.CompilerParams(dimension_semantics=("parallel",),
                                             vmem_limit_bytes=VMEM_LIMIT),
        name="ln_proj",
    )(x, g, b, w_rwkv, w_na, w_gate)


def _rwkv_kernel(rf_ref, vf_ref, kkf_ref, kdf_ref, bef_ref, lwf_ref,
                 rb_ref, vb_ref, kkb_ref, kdb_ref, beb_ref, lwb_ref, yf_ref, yb_ref, z_ref, *, n_slabs, n_seq):
    @pl.when(pl.program_id(1) == 0)
    def _():
        z_ref[...] = jnp.zeros_like(z_ref)

    n2 = HEADS_PER_SLAB * CHUNK
    row = lax.broadcasted_iota(jnp.int32, (n2, n2), 0)
    col = lax.broadcasted_iota(jnp.int32, (n2, n2), 1)
    dts = (row & (CHUNK - 1)) - (col & (CHUNK - 1))
    blk16 = ((row & (CHUNK - 1)) >> 4) == ((col & (CHUNK - 1)) >> 4)
    eye = row == col
    rowi = lax.broadcasted_iota(jnp.int32, (CHUNK, LANES), 0)
    lane = lax.broadcasted_iota(jnp.int32, (CHUNK, LANES), 1)
    head0 = lane < HEAD_DIM

    def bd(x):
        return jnp.concatenate([jnp.where(head0, x, 0.0), jnp.where(head0, 0.0, x)], axis=0)

    def bf(x):
        return x.astype(BF16)

    chains = [(d, (q, p)) for q in range(n_seq) for d in range(2) for p in range(n_slabs)]
    sls = [(q, slice(p * LANES, (p + 1) * LANES)) for _, (q, p) in chains]
    strict = [dts > 0 if d == 0 else dts < 0 for d, _ in chains]
    incl = [dts >= 0 if d == 0 else dts <= 0 for d, _ in chains]
    r_in = [(rf_ref, rb_ref)[d] for d, _ in chains]
    v_in = [(vf_ref, vb_ref)[d] for d, _ in chains]
    kk_in = [(kkf_ref, kkb_ref)[d] for d, _ in chains]
    kd_in = [(kdf_ref, kdb_ref)[d] for d, _ in chains]
    be_in = [(bef_ref, beb_ref)[d] for d, _ in chains]
    lw_in = [(lwf_ref, lwb_ref)[d] for d, _ in chains]
    ident = jnp.where(eye, 1.0, 0.0).astype(F32)

    def each(f, *cols):
        return [f(*args) for args in zip(*cols)]

    lw = each(lambda ref, sl: ref[0, sl[0], :, sl[1]], lw_in, sls)
    def scan_rows(x, reverse):
        step = 1
        while step < CHUNK:
            if reverse:
                x = x + jnp.where(rowi < CHUNK - step, pltpu.roll(x, CHUNK - step, axis=0), 0.0)
            else:
                x = x + jnp.where(rowi >= step, pltpu.roll(x, step, axis=0), 0.0)
            step *= 2
        return x

    cum = [scan_rows(x, d == 1) for x, (d, _) in zip(lw, chains)]
    tot = each(lambda x: jnp.sum(x, axis=0, keepdims=True), lw)
    e_in = each(jnp.exp, cum)
    e_ex = each(lambda c_, l_: jnp.exp(c_ - l_), cum, lw)
    e_neg = each(lambda c_: jnp.exp(-c_), cum)
    e_rem = each(lambda t_, c_: jnp.exp(t_ - c_), tot, cum)
    gam = each(jnp.exp, tot)
    kk = each(lambda ref, sl: ref[sl[0], :, sl[1]].astype(F32), kk_in, sls)
    kd = each(lambda ref, sl: ref[0, sl[0], :, sl[1]].astype(F32), kd_in, sls)
    be = each(lambda ref, sl: ref[0, sl[0], :, sl[1]].astype(F32), be_in, sls)
    a_b = each(lambda x, e: bf(bd(x * e)), kk, e_ex)
    r_t = each(lambda ref, e, sl: bd(ref[sl[0], :, sl[1]].astype(F32) * e), r_in, e_in, sls)
    b_t = each(lambda x, e: bd(x * e), be, e_neg)
    k_t = each(lambda x, e: bd(x * e), kd, e_neg)
    b_h = each(lambda x, e: bd(x * e), be, e_rem)
    k_h = each(lambda x, e: bd(x * e), kd, e_rem)
    vv = each(lambda ref, sl: bf(bd(ref[sl[0], :, sl[1]].astype(F32))), v_in, sls)

    sc = each(lambda a, r, b, k: _mm_nt(jnp.concatenate([a, bf(r)], axis=0),
                                        bf(jnp.concatenate([b, k], axis=0))), a_b, r_t, b_t, k_t)
    lk = each(lambda m, s: bf(jnp.where(m, s[:n2, n2:], 0.0)), strict, sc)
    mb = each(lambda m, s: bf(jnp.where(m, s[n2:, :n2], 0.0)), incl, sc)
    mk = each(lambda m, s: bf(jnp.where(m, s[n2:, n2:], 0.0)), incl, sc)

    nn = each(lambda m, s: jnp.where(m, -s[:n2, :n2], 0.0), strict, sc)
    dg = each(lambda x: jnp.where(blk16, x, 0.0), nn)
    offb = each(lambda x, d_: bf(x - d_), nn, dg)
    def mm2(xs, ys):
        return each(_mm, xs, ys)

    d1b = each(bf, dg)
    d2b = each(bf, mm2(d1b, d1b))
    d4b = each(bf, mm2(d2b, d2b))
    d8b = each(bf, mm2(d4b, d4b))
    td = each(lambda d_: ident + d_, dg)
    for dpow in (d2b, d4b, d8b):
        td = each(lambda t_, m_: t_ + m_, td, mm2(each(bf, td), dpow))
    tdb = each(bf, td)
    e1 = mm2(tdb, offb)
    e1b = each(bf, e1)
    e2 = mm2(e1b, e1b)
    e3 = mm2(e1b, each(bf, e2))
    ttb = each(bf, mm2(each(lambda x1_, x2_, x3_: bf(ident + x1_ + x2_ + x3_), e1, e2, e3), tdb))

    lkv = mm2(lk, vv)
    wb = each(lambda t_, a, l_: bf(_mm(t_, jnp.concatenate([a, bf(l_)], axis=1))), ttb, a_b, lkv)
    mw = each(_mm, mb, wb)
    mkv = mm2(mk, vv)
    bw = each(lambda b, w: _mm(bf(b.T), w), b_h, wb)
    kv = mm2(each(lambda k: bf(k.T), k_h), vv)
    r_hat = each(lambda r, m: bf(r - m[:, :n2]), r_t, mw)
    y_loc = each(lambda m, w: m - w[:, n2:], mkv, mw)
    g = each(lambda g_, b: bf(jnp.where(eye, jnp.broadcast_to(g_, (n2, n2)), 0.0) - b[:, :n2]), gam, bw)
    h = each(lambda k, b: k - b[:, n2:], kv, bw)

    n_ch = len(chains)
    z = [z_ref[i] for i in range(n_ch)]
    z_hi = each(bf, z)
    z_lo = each(lambda z_, zh: bf(z_ - zh.astype(F32)), z, z_hi)
    y = each(lambda m_, yl: m_ + yl, mm2(r_hat, z_hi), y_loc)
    gz = each(lambda g_, zh, zl: _mm(g_, jnp.concatenate([zh, zl], axis=1)), g, z_hi, z_lo)
    z_new = each(lambda m_, h_: m_[:, :n2] + m_[:, n2:] + h_, gz, h)
    for i, (d, _) in enumerate(chains):
        z_ref[i] = z_new[i]
        (yf_ref, yb_ref)[d][sls[i][0], :, sls[i][1]] = (y[i][:CHUNK] + y[i][CHUNK:]).astype(BF16)


def _rwkv_scan(r, v, kk, kd, be, lw):
    b, l, dr = r.shape
    nc = l // CHUNK
    n_slabs = dr // LANES
    n_seq = RWKV_SEQS_PER_STEP if b % RWKV_SEQS_PER_STEP == 0 else 1
    fwd = pl.BlockSpec((n_seq, CHUNK, dr), lambda bi, c: (bi, c, 0))
    bwd = pl.BlockSpec((n_seq, CHUNK, dr), lambda bi, c: (bi, nc - 1 - c, 0))
    fwd_d = pl.BlockSpec((1, n_seq, CHUNK, dr), lambda bi, c: (0, bi, c, 0))
    bwd_d = pl.BlockSpec((1, n_seq, CHUNK, dr), lambda bi, c: (1, bi, nc - 1 - c, 0))
    return pl.pallas_call(
        functools.partial(_rwkv_kernel, n_slabs=n_slabs, n_seq=n_seq),
        grid=(b // n_seq, nc),
        in_specs=[fwd, fwd, fwd, fwd_d, fwd_d, fwd_d, bwd, bwd, bwd, bwd_d, bwd_d, bwd_d],
        out_specs=[fwd, bwd],
        out_shape=[jax.ShapeDtypeStruct((b, l, dr), BF16), jax.ShapeDtypeStruct((b, l, dr), BF16)],
        scratch_shapes=[pltpu.VMEM((2 * n_slabs * n_seq, HEADS_PER_SLAB * CHUNK, HEADS_PER_SLAB * HEAD_DIM),
                                   F32)],
        compiler_params=pltpu.CompilerParams(
            dimension_semantics=("parallel", "arbitrary"),
            vmem_limit_bytes=VMEM_LIMIT),
        name="rwkv_scan",
    )(r, v, kk, kd, be, lw, r, v, kk, kd, be, lw)


def _rwkv_prep_kernel(p_ref, prev_ref, next_ref, mu_ref, wdec_ref, w0_ref, wicl_ref, a0_ref, g2_ref,
                      kk_w_ref, ka_ref, rk_ref, hsum_ref,
                      r_ref, v_ref, kk_ref, kd_ref, be_ref, lw_ref, bonus_ref, gate_ref, *, tiles_per_seq):
    i = pl.program_id(0)
    p = p_ref[...].astype(F32)
    tm = p.shape[0]
    pos = i % tiles_per_seq
    prev_row = jnp.where(pos == 0, 0.0, prev_ref[...].astype(F32)[HALO_ROWS - 1:HALO_ROWS, :])
    next_row = jnp.where(pos == tiles_per_seq - 1, 0.0, next_ref[...].astype(F32)[0:1, :])
    rowi = lax.broadcasted_iota(jnp.int32, (tm, 1), 0)
    up = jnp.where(rowi == 0, prev_row, pltpu.roll(p, 1, axis=0))
    dn = jnp.where(rowi == tm - 1, next_row, pltpu.roll(p, tm - 1, axis=0))
    ps = p + mu_ref[...] * (0.5 * (up + dn) - p)

    c = np.cumsum((0,) + RWKV_SIZES)
    xr, xk, xv = ps[:, c[0]:c[1]], ps[:, c[1]:c[2]], ps[:, c[2]:c[3]]
    dw, da, dg = ps[:, c[3]:c[5]], ps[:, c[5]:c[7]], ps[:, c[7]:c[8]]
    wl = w0_ref[...] + _mm(jnp.tanh(dw).astype(BF16), wdec_ref[...])
    neg = -wl
    softplus = jnp.maximum(neg, 0.0) + jnp.log(1.0 + jnp.exp(-jnp.abs(neg)))
    lw = -jnp.exp(-softplus - 0.5)
    a = jax.nn.sigmoid(a0_ref[...] + _mm(da.astype(BF16), wicl_ref[...]))
    kkr = xk * kk_w_ref[...]
    ss = _mm((kkr * kkr).astype(BF16), hsum_ref[...])
    kk = kkr * lax.rsqrt(jnp.maximum(ss, 1e-24))
    kd_sum = jnp.zeros_like(xk)
    for d in range(2):
        a_d = a[:, d * D_RWKV:(d + 1) * D_RWKV]
        kd = xk * (1.0 + (a_d - 1.0) * ka_ref[...])
        kd_sum = kd_sum + kd
        kd_ref[d] = kd.astype(BF16)
        be_ref[d] = (a_d * kk).astype(BF16)
        lw_ref[d] = lw[:, d * D_RWKV:(d + 1) * D_RWKV]
    r_ref[...] = xr.astype(BF16)
    v_ref[...] = xv.astype(BF16)
    kk_ref[...] = kk.astype(BF16)
    bonus_ref[...] = (_mm((xr * rk_ref[...] * kd_sum).astype(BF16), hsum_ref[...]) * xv).astype(BF16)
    gate_ref[...] = _mm(jax.nn.sigmoid(dg).astype(BF16), g2_ref[...]).astype(BF16)


def _rwkv_branch(p_rwkv, b, l, prm, tm):
    t = b * l
    r, v, kk, kd, be, lw, bonus, gate = _rwkv_prep(p_rwkv, l, prm, tm)
    seq = lambda a: a.reshape(b, l, D_RWKV)
    seq2 = lambda a: a.reshape(2, b, l, D_RWKV)
    wkv_f, wkv_b = _rwkv_scan(seq(r), seq(v), seq(kk), seq2(kd), seq2(be), seq2(lw))
    return wkv_f.reshape(t, D_RWKV), wkv_b.reshape(t, D_RWKV), bonus, gate


def _block_diag2(w):
    z = jnp.zeros_like(w[0])
    return jnp.concatenate([jnp.concatenate([w[0], z], axis=1), jnp.concatenate([z, w[1]], axis=1)], axis=0)


def _head_sum_matrix(scale):
    h = np.arange(D_RWKV) // HEAD_DIM
    return jnp.asarray((h[:, None] == h[None, :]) * scale, BF16)


def _rwkv_prep(p_rwkv, seq_len, prm, tm):
    t = p_rwkv.shape[0]
    row2 = lambda a: a.reshape(1, -1).astype(F32)
    consts = [row2(prm['mu_shift']), _block_diag2(prm['w2']).astype(BF16), row2(prm['w0']),
              _block_diag2(prm['a2']).astype(BF16), row2(prm['a0']), prm['g2'].astype(BF16),
              row2(prm['k_k']), row2(prm['k_a']), row2(prm['r_k']), _head_sum_matrix(1.0)]
    full = lambda a: pl.BlockSpec(a.shape, lambda i: (0, 0))
    row = pl.BlockSpec((tm, D_RWKV), lambda i: (i, 0))
    row_d = pl.BlockSpec((2, tm, D_RWKV), lambda i: (0, i, 0))
    halo = tm // HALO_ROWS
    sd = lambda dt: jax.ShapeDtypeStruct((t, D_RWKV), dt)
    sd2 = lambda dt: jax.ShapeDtypeStruct((2, t, D_RWKV), dt)
    return pl.pallas_call(
        functools.partial(_rwkv_prep_kernel, tiles_per_seq=seq_len // tm),
        grid=(t // tm,),
        in_specs=[pl.BlockSpec((tm, RWKV_IN), lambda i: (i, 0)),
                  pl.BlockSpec((HALO_ROWS, RWKV_IN), lambda i: (jnp.maximum(i * halo - 1, 0), 0)),
                  pl.BlockSpec((HALO_ROWS, RWKV_IN),
                               lambda i: (jnp.minimum((i + 1) * halo, t // HALO_ROWS - 1), 0))]
        + [full(a) for a in consts],
        out_specs=[row, row, row, row_d, row_d, row_d, row, row],
        out_shape=[sd(BF16), sd(BF16), sd(BF16), sd2(BF16), sd2(BF16), sd2(F32), sd(BF16), sd(BF16)],
        compiler_params=pltpu.CompilerParams(dimension_semantics=("parallel",),
                                             vmem_limit_bytes=VMEM_LIMIT),
        name="rwkv_prep",
    )(p_rwkv, p_rwkv, p_rwkv, *consts)


def _na_bias_table(rpb):
    n_h = rpb.shape[0]
    var = np.arange(NA_WIN_H)[:, None]
    i = np.arange(NA_WIN_H)[None, :]
    c = np.arange(GRID_W)[:, None]
    kc = np.arange(GRID_W)[None, :]
    cs = np.clip(c - NA_WIN_W // 2, 0, GRID_W - NA_WIN_W)
    valid = (kc >= cs) & (kc < cs + NA_WIN_W)
    row_sel = (np.arange(2 * NA_WIN_H - 1)[None, None, :] == (i - var + NA_WIN_H - 1)[:, :, None])
    col_sel = (np.arange(2 * NA_WIN_W - 1)[None, None, :] == (kc - c + NA_WIN_W - 1)[:, :, None])
    col_sel = col_sel & valid[:, :, None]
    tab = jnp.einsum('hab,via,ckb->vhcik', rpb.astype(F32), row_sel.astype(np.float32),
                     col_sel.astype(np.float32), precision=HI)
    tab = jnp.where(valid[None, None, :, None, :], tab, NEG_BIG)
    return tab.reshape(NA_WIN_H, n_h * GRID_W, NA_WIN_H * GRID_W)


def _na_kernel(q_ref, k_ref, v_ref, *refs, rows, n_slabs):
    bias_refs, o_ref = refs[:-1], refs[-1]
    band = NA_WIN_H * GRID_W
    scale = HEAD_DIM ** -0.5
    lane = lax.broadcasted_iota(jnp.int32, (GRID_W, LANES), 1)
    head0 = lane < HEAD_DIM
    chains = [(i, p) for i in range(NA_ROWS_PER_STEP) for p in range(n_slabs)]
    starts = []
    for i in range(NA_ROWS_PER_STEP):
        r = pl.program_id(1) * NA_ROWS_PER_STEP + i
        rs = jnp.clip(r - NA_WIN_H // 2, 0, rows - NA_WIN_H)
        starts.append(pl.multiple_of(rs * GRID_W, GRID_W))

    def lanes(p):
        return slice(p * LANES, (p + 1) * LANES)

    def split_heads(q2):
        zero = jnp.zeros_like(q2)
        return jnp.concatenate([jnp.where(head0, q2, zero), jnp.where(head0, zero, q2)], axis=0)

    lhs = [split_heads(q_ref[0, i * GRID_W:(i + 1) * GRID_W, lanes(p)]) for i, p in chains]
    s = [_mm_nt(x, k_ref[0, pl.ds(starts[i], band), lanes(p)]) for x, (i, p) in zip(lhs, chains)]
    s = [x * scale + bias_refs[i][0, p * 2 * GRID_W:(p + 1) * 2 * GRID_W, :] for x, (i, p) in zip(s, chains)]
    m = [jnp.max(x, axis=-1, keepdims=True) for x in s]
    e = [jnp.exp(x - mx) for x, mx in zip(s, m)]
    den = [jnp.sum(x, axis=-1, keepdims=True) for x in e]
    o = [_mm(x.astype(BF16), v_ref[0, pl.ds(starts[i], band), lanes(p)]) for x, (i, p) in zip(e, chains)]
    for x, d, (i, p) in zip(o, den, chains):
        x = x / d
        o_ref[0, i * GRID_W:(i + 1) * GRID_W, lanes(p)] = (
            jnp.where(head0, x[:GRID_W], x[GRID_W:]).astype(o_ref.dtype))


def _na_attention(q, k, v, bias_tab):
    b, l, dn = q.shape
    rows = l // GRID_W
    assert rows >= NA_WIN_H and rows % NA_ROWS_PER_STEP == 0
    n_slabs = dn // LANES
    qrows = NA_ROWS_PER_STEP * GRID_W

    def bias_spec(i):
        def bias_idx(bi, g):
            r = g * NA_ROWS_PER_STEP + i
            rs = jnp.clip(r - NA_WIN_H // 2, 0, rows - NA_WIN_H)
            return (r - rs, 0, 0)
        return pl.BlockSpec((1,) + bias_tab.shape[1:], bias_idx)

    return pl.pallas_call(
        functools.partial(_na_kernel, rows=rows, n_slabs=n_slabs),
        grid=(b, rows // NA_ROWS_PER_STEP),
        in_specs=[pl.BlockSpec((1, qrows, dn), lambda bi, g: (bi, g, 0)),
                  pl.BlockSpec((1, l, dn), lambda bi, g: (bi, 0, 0)),
                  pl.BlockSpec((1, l, dn), lambda bi, g: (bi, 0, 0))]
        + [bias_spec(i) for i in range(NA_ROWS_PER_STEP)],
        out_specs=pl.BlockSpec((1, qrows, dn), lambda bi, g: (bi, g, 0)),
        out_shape=jax.ShapeDtypeStruct((b, l, dn), BF16),
        compiler_params=pltpu.CompilerParams(dimension_semantics=("parallel", "arbitrary"),
                                             vmem_limit_bytes=VMEM_LIMIT),
        name="na_attention",
    )(q, k, v, *([bias_tab] * NA_ROWS_PER_STEP))


def _to_token_tiles(dst_ref, x):
    n = x.shape[0]
    for s in range(SLABS):
        lo = _bf16_bits(x[:, s * LANES:(s + 1) * LANES])
        hi = _bf16_bits(x[:, (s + SLABS) * LANES:(s + SLABS + 1) * LANES])
        dst_ref[pl.ds(s, n, stride=SLABS), :] = (lo >> 16) | hi


def _from_token_tiles(src_ref, n):
    words = [src_ref[pl.ds(s, n, stride=SLABS), :] for s in range(SLABS)]
    lo = [lax.bitcast_convert_type(w << 16, F32) for w in words]
    hi = [lax.bitcast_convert_type(w & U32(0xFFFF0000), F32) for w in words]
    return jnp.concatenate(lo + hi, axis=1)


def _bf16_bits(x):
    b = lax.bitcast_convert_type(x, U32)
    return (b + U32(0x7FFF) + ((b >> 16) & U32(1))) & U32(0xFFFF0000)


def _merge_kernel(wf_ref, wb_ref, bonus_ref, gate_ref, havg_ref, lg_ref, lb_ref,
                  yn_ref, ga_ref, gn_ref, xn_ref, wa_ref, wn_ref, wo_ref, g1_ref, b1_ref,
                  wrt_ref, x1_ref, x1t_ref, sct_ref):
    wkv = wf_ref[...].astype(F32) + wb_ref[...].astype(F32)
    xc = wkv - _mm(wkv.astype(BF16), havg_ref[...])
    var = _mm((xc * xc).astype(BF16), havg_ref[...])
    ya = ((xc * lax.rsqrt(var + GN_EPS) * lg_ref[...] + lb_ref[...] + bonus_ref[...].astype(F32))
          * gate_ref[...].astype(F32))
    up_a = _mm(ya.astype(BF16), wa_ref[...])
    up_n = _mm(yn_ref[...], wn_ref[...])
    merged = (jax.nn.sigmoid(ga_ref[...].astype(F32)) * up_a
              + jax.nn.sigmoid(gn_ref[...].astype(F32)) * up_n)
    mix = _mm(merged.astype(BF16), wo_ref[...])
    x1 = _ln(ALPHA * xn_ref[...] + mix, g1_ref[...], b1_ref[...])
    x1_ref[...] = x1
    _to_token_tiles(x1t_ref, x1)
    w_hi, w_lo = wrt_ref[0], wrt_ref[1]
    x_hi = x1.astype(BF16)
    x_lo = (x1 - x_hi.astype(F32)).astype(BF16)
    sct_ref[...] = jax.nn.sigmoid(_mm_nt(w_hi, x_hi) + _mm_nt(w_hi, x_lo) + _mm_nt(w_lo, x_hi))


def _merge(wkv_f, wkv_b, bonus, gate, lnx_g, lnx_b, yn, ga, gn, xn, w_up_a, w_up_n, w_out, ln1_g, ln1_b,
           w_router_t, tm):
    t = xn.shape[0]
    row = lambda n: pl.BlockSpec((tm, n), lambda i: (i, 0))
    full = lambda a: pl.BlockSpec(a.shape, lambda i: (0,) * a.ndim)
    havg = _head_sum_matrix(1.0 / HEAD_DIM)
    return pl.pallas_call(
        _merge_kernel,
        grid=(t // tm,),
        in_specs=[row(D_RWKV), row(D_RWKV), row(D_RWKV), row(D_RWKV), full(havg), full(lnx_g), full(lnx_b),
                  row(D_NA), row(D_MODEL), row(D_MODEL), row(D_MODEL),
                  full(w_up_a), full(w_up_n), full(w_out), full(ln1_g), full(ln1_b), full(w_router_t)],
        out_specs=[row(D_MODEL), pl.BlockSpec((tm * SLABS, LANES), lambda i: (i, 0)),
                   pl.BlockSpec((N_EXPERTS, tm), lambda i: (0, i))],
        out_shape=[jax.ShapeDtypeStruct((t, D_MODEL), F32),
                   jax.ShapeDtypeStruct((t * SLABS, LANES), U32),
                   jax.ShapeDtypeStruct((N_EXPERTS, t), F32)],
        compiler_params=pltpu.CompilerParams(dimension_semantics=("parallel",),
                                             vmem_limit_bytes=VMEM_LIMIT),
        name="merge_ln1_router",
    )(wkv_f, wkv_b, bonus, gate, havg, lnx_g, lnx_b, yn, ga, gn, xn, w_up_a, w_up_n, w_out, ln1_g, ln1_b,
      w_router_t)


def _first_argmax(vals, iota):
    m = jnp.max(vals, axis=0, keepdims=True)
    first = jnp.min(jnp.where(vals == m, iota, float(vals.shape[0])), axis=0, keepdims=True)
    return m, first


def _route_kernel(sct_ref, bias_ref, idx_ref, wts_ref, cnt_ref):
    s = sct_ref[...]
    tm = s.shape[1]
    sel = s + bias_ref[...]
    gsz = N_EXPERTS // N_GROUPS
    iota_g = lax.broadcasted_iota(jnp.int32, (gsz, tm), 0).astype(F32)
    iota_8 = lax.broadcasted_iota(jnp.int32, (N_GROUPS, tm), 0).astype(F32)
    iota_e = lax.broadcasted_iota(jnp.int32, (N_EXPERTS, tm), 0).astype(F32)

    gs = jnp.zeros((N_GROUPS, tm), F32)
    for g in range(N_GROUPS):
        blk = sel[g * gsz:(g + 1) * gsz, :]
        m1, i1 = _first_argmax(blk, iota_g)
        m2 = jnp.max(jnp.where(iota_g == i1, -jnp.inf, blk), axis=0, keepdims=True)
        gs = jnp.where(iota_8 == float(g), m1 + m2, gs)

    chosen = jnp.zeros((N_GROUPS, tm), F32)
    for _ in range(TOPK_GROUPS):
        _, gi = _first_argmax(gs, iota_8)
        hit = iota_8 == gi
        chosen = jnp.where(hit, 1.0, chosen)
        gs = jnp.where(hit, -jnp.inf, gs)
    mask_e = jnp.concatenate([jnp.broadcast_to(chosen[g:g + 1, :], (gsz, tm)) for g in range(N_GROUPS)],
                             axis=0)
    cand = jnp.where(mask_e > 0.0, sel, -jnp.inf)

    iota_k = lax.broadcasted_iota(jnp.int32, (TOP_K, tm), 0)
    idx = jnp.zeros((TOP_K, tm), F32)
    wts = jnp.zeros((TOP_K, tm), F32)
    member = jnp.zeros((N_EXPERTS, tm), F32)
    for j in range(TOP_K):
        _, ij = _first_argmax(cand, iota_e)
        hit = iota_e == ij
        wj = jnp.sum(jnp.where(hit, s, 0.0), axis=0, keepdims=True)
        cand = jnp.where(hit, -jnp.inf, cand)
        member = jnp.where(hit, 1.0, member)
        idx = jnp.where(iota_k == j, ij, idx)
        wts = jnp.where(iota_k == j, wj, wts)
    wts = wts / jnp.sum(wts, axis=0, keepdims=True) * ROUTED_SCALE
    idx_ref[...] = idx.astype(jnp.int32)
    wts_ref[...] = wts
    cnt_ref[0] = _mm_nt(jnp.ones((8, tm), BF16), member.astype(BF16))


def _route(sct, e_bias, tm):
    n_e, t = sct.shape
    nt = t // tm
    return pl.pallas_call(
        _route_kernel,
        grid=(nt,),
        in_specs=[pl.BlockSpec((n_e, tm), lambda i: (0, i)),
                  pl.BlockSpec((n_e, 1), lambda i: (0, 0))],
        out_specs=[pl.BlockSpec((TOP_K, tm), lambda i: (0, i)),
                   pl.BlockSpec((TOP_K, tm), lambda i: (0, i)),
                   pl.BlockSpec((1, 8, n_e), lambda i: (i, 0, 0))],
        out_shape=[jax.ShapeDtypeStruct((TOP_K, t), jnp.int32),
                   jax.ShapeDtypeStruct((TOP_K, t), F32),
                   jax.ShapeDtypeStruct((nt, 8, n_e), F32)],
        compiler_params=pltpu.CompilerParams(dimension_semantics=("parallel",),
                                             vmem_limit_bytes=VMEM_LIMIT),
        name="moe_route",
    )(sct, e_bias.reshape(n_e, 1).astype(F32))


def _dest_kernel(idx_ref, base_ref, dest_ref):
    idx = idx_ref[...].astype(F32)
    tm = idx.shape[1]
    iota_e = lax.broadcasted_iota(jnp.int32, (N_EXPERTS, tm), 0).astype(F32)
    hits = [iota_e == idx[j:j + 1, :] for j in range(TOP_K)]
    member = jnp.zeros((N_EXPERTS, tm), F32)
    for hit in hits:
        member = jnp.where(hit, 1.0, member)
    r = lax.broadcasted_iota(jnp.int32, (tm, tm), 0)
    c = lax.broadcasted_iota(jnp.int32, (tm, tm), 1)
    earlier = jnp.where(r < c, 1.0, 0.0).astype(BF16)
    rank = _mm(member.astype(BF16), earlier) + base_ref[0]
    iota_k = lax.broadcasted_iota(jnp.int32, (TOP_K, tm), 0)
    dest = jnp.zeros((TOP_K, tm), F32)
    for j, hit in enumerate(hits):
        dj = jnp.sum(jnp.where(hit, rank, 0.0), axis=0, keepdims=True)
        dest = jnp.where(iota_k == j, dj, dest)
    dest_ref[0] = dest.astype(jnp.int32)


def _dest_rows(idx, base, tm):
    t = idx.shape[1]
    nt = t // tm
    return pl.pallas_call(
        _dest_kernel,
        grid=(nt,),
        in_specs=[pl.BlockSpec((TOP_K, tm), lambda i: (0, i)),
                  pl.BlockSpec((1, N_EXPERTS, 1), lambda i: (i, 0, 0))],
        out_specs=pl.BlockSpec((1, TOP_K, tm), lambda i: (i, 0, 0)),
        out_shape=jax.ShapeDtypeStruct((nt, TOP_K, tm), jnp.int32),
        compiler_params=pltpu.CompilerParams(dimension_semantics=("parallel",),
                                             vmem_limit_bytes=VMEM_LIMIT),
        name="moe_dest",
    )(idx, base)


def _block_plan(tile_counts, n_tokens):
    n_blocks = (n_tokens * TOP_K + N_EXPERTS * (MOE_ROWS - 1) + MOE_ROWS - 1) // MOE_ROWS
    n_blocks = -(-n_blocks // MOE_GROUP) * MOE_GROUP
    counts = jnp.sum(tile_counts, axis=0)
    padded = (counts + MOE_ROWS - 1) // MOE_ROWS * MOE_ROWS
    extra = (-(jnp.sum(padded) // MOE_ROWS)) % MOE_GROUP
    padded = padded.at[N_EXPERTS - 1].add(extra * MOE_ROWS)
    pend = jnp.cumsum(padded)
    pstart = pend - padded
    tile_base = pstart[None, :] + jnp.cumsum(tile_counts, axis=0) - tile_counts
    block_start = jnp.arange(n_blocks, dtype=jnp.int32) * MOE_ROWS
    block_e = jnp.minimum(jnp.sum(pend[None, :] <= block_start[:, None], axis=1),
                          N_EXPERTS - 1).astype(jnp.int32)
    n_used = (pend[-1] // MOE_ROWS).astype(jnp.int32).reshape(1)
    return (tile_base.astype(F32)[:, :, None], counts.astype(jnp.int32), padded.astype(jnp.int32),
            pstart.astype(jnp.int32), block_e, n_used, n_blocks)


def _row_tile(ref, row):
    return ref.at[pl.ds(pl.multiple_of(row * SLABS, SLABS), SLABS)]


def _dispatch_kernel(cnt_ref, pad_ref, pst_ref, dest_hbm, x_ref, xs_hbm, dest_s, zrow, isem, csem, zsem,
                     *, tm):
    i = pl.program_id(0)
    n = pl.num_programs(0)
    cur = i % 2

    def idx_copy(tile, buf):
        return pltpu.make_async_copy(dest_hbm.at[tile], dest_s.at[buf], isem.at[buf])

    @pl.when(i == 0)
    def _():
        idx_copy(0, 0).start()
        zrow[...] = jnp.zeros_like(zrow)

        def per_expert(e, carry, wait):
            def per_row(r, c2):
                cp = pltpu.make_async_copy(zrow, _row_tile(xs_hbm, pst_ref[e] + r), zsem)
                if wait:
                    cp.wait()
                else:
                    cp.start()
                return c2
            return lax.fori_loop(cnt_ref[e], pad_ref[e], per_row, carry)

        lax.fori_loop(0, N_EXPERTS, functools.partial(per_expert, wait=False), 0)
        lax.fori_loop(0, N_EXPERTS, functools.partial(per_expert, wait=True), 0)

    idx_copy(i, cur).wait()

    @pl.when(i + 1 < n)
    def _():
        idx_copy(i + 1, 1 - cur).start()

    def per_token(t, carry):
        src = x_ref.at[pl.ds(pl.multiple_of(t * SLABS, SLABS), SLABS)]
        for j in range(TOP_K):
            pltpu.make_async_copy(src, _row_tile(xs_hbm, dest_s[cur, j, t]), csem).start()
        return carry

    lax.fori_loop(0, tm, per_token, 0)
    for j in range(TOP_K):
        pltpu.make_async_copy(x_ref, xs_hbm.at[pl.ds(0, tm * SLABS)], csem).wait()


def _dispatch(x1t, dest, counts, padded, pstart, n_blocks, tm):
    nt = dest.shape[0]
    any_spec = pl.BlockSpec(memory_space=pl.ANY)
    grid_spec = pltpu.PrefetchScalarGridSpec(
        num_scalar_prefetch=3,
        grid=(nt,),
        in_specs=[any_spec, pl.BlockSpec((tm * SLABS, LANES), lambda i, c, p, s: (i, 0))],
        out_specs=any_spec,
        scratch_shapes=[pltpu.SMEM((2, TOP_K, tm), jnp.int32),
                        pltpu.VMEM((SLABS, LANES), U32),
                        pltpu.SemaphoreType.DMA((2,)),
                        pltpu.SemaphoreType.DMA(()),
                        pltpu.SemaphoreType.DMA(())])
    return pl.pallas_call(
        functools.partial(_dispatch_kernel, tm=tm),
        grid_spec=grid_spec,
        out_shape=jax.ShapeDtypeStruct((n_blocks * MOE_ROWS * SLABS, LANES), U32),
        compiler_params=pltpu.CompilerParams(dimension_semantics=("arbitrary",),
                                             vmem_limit_bytes=VMEM_LIMIT),
        name="moe_dispatch",
    )(counts, padded, pstart, dest, x1t)


def _swiglu(xb, wg, wu, wd):
    hg = _mm(xb, wg)
    hu = _mm(xb, wu)
    return _mm((hg * jax.nn.sigmoid(hg) * hu).astype(BF16), wd)


def _experts_kernel(be_ref, nu_ref, xs_ref, *refs):
    w_refs, ys_ref = refs[:-1], refs[-1]
    s = pl.program_id(0)
    blk = MOE_ROWS * SLABS

    @pl.when(MOE_GROUP * s < nu_ref[0])
    def _():
        parts = [pl.ds(k * blk, blk) for k in range(MOE_GROUP)]
        wgu = [w_refs[2 * k] for k in range(MOE_GROUP)]
        wd = [w_refs[2 * k + 1] for k in range(MOE_GROUP)]
        xb, hh = {}, {}
        for step in range(MOE_GROUP + 2):
            k = step
            if k < MOE_GROUP:
                xb[k] = _from_token_tiles(xs_ref.at[parts[k]], MOE_ROWS).astype(BF16)
            k = step - 1
            if 0 <= k < MOE_GROUP:
                hgu = _mm(xb[k], wgu[k][0])
                hg, hu = hgu[:, :D_EXPERT], hgu[:, D_EXPERT:]
                hh[k] = (hg * jax.nn.sigmoid(hg) * hu).astype(BF16)
            k = step - 2
            if 0 <= k < MOE_GROUP:
                _to_token_tiles(ys_ref.at[parts[k]], _mm(hh[k], wd[k][0]))

    @pl.when(MOE_GROUP * s >= nu_ref[0])
    def _():
        ys_ref[...] = jnp.zeros_like(ys_ref)


def _experts(xs, block_e, n_used, w_gate_up_e, w_down_e):
    n_groups = block_e.shape[0] // MOE_GROUP
    blkg = MOE_GROUP * MOE_ROWS * SLABS
    wspec = lambda shp, k: pl.BlockSpec((1,) + shp, lambda s, be, nu: (be[MOE_GROUP * s + k], 0, 0))
    wspecs = [wspec(shp, k) for k in range(MOE_GROUP)
              for shp in ((D_MODEL, 2 * D_EXPERT), (D_EXPERT, D_MODEL))]
    grid_spec = pltpu.PrefetchScalarGridSpec(
        num_scalar_prefetch=2,
        grid=(n_groups,),
        in_specs=[pl.BlockSpec((blkg, LANES),
                               lambda s, be, nu: (jnp.minimum(s, nu[0] // MOE_GROUP - 1), 0))] + wspecs,
        out_specs=pl.BlockSpec((blkg, LANES), lambda s, be, nu: (s, 0)))
    return pl.pallas_call(
        _experts_kernel,
        grid_spec=grid_spec,
        out_shape=jax.ShapeDtypeStruct(xs.shape, xs.dtype),
        compiler_params=pltpu.CompilerParams(dimension_semantics=("arbitrary",),
                                             vmem_limit_bytes=VMEM_LIMIT),
        name="moe_experts",
    )(block_e, n_used, xs, *([w_gate_up_e, w_down_e] * MOE_GROUP))


def _final_kernel(dest_hbm, ys_hbm, x1_ref, wts_ref, wg_ref, wu_ref, wd_ref, g2_ref, b2_ref, o_ref,
                  dest_s, gbuf, isem, gsem, *, tm):
    i = pl.program_id(0)
    n = pl.num_programs(0)
    cur = i % 2
    nxt = 1 - cur

    def idx_copy(tile, buf):
        return pltpu.make_async_copy(dest_hbm.at[tile], dest_s.at[buf], isem.at[buf])

    def start_gather(buf):
        def per_token(t, carry):
            for j in range(TOP_K):
                dst = gbuf.at[buf, j, pl.ds(pl.multiple_of(t * SLABS, SLABS), SLABS)]
                pltpu.make_async_copy(_row_tile(ys_hbm, dest_s[buf, j, t]), dst, gsem.at[buf]).start()
            return carry
        lax.fori_loop(0, tm, per_token, 0)

    @pl.when(i == 0)
    def _():
        idx_copy(0, 0).start()
        idx_copy(0, 0).wait()
        start_gather(0)

        @pl.when(1 < n)
        def _():
            idx_copy(1, 1).start()

    @pl.when(i + 1 < n)
    def _():
        idx_copy(i + 1, nxt).wait()
        start_gather(nxt)

        @pl.when(i + 2 < n)
        def _():
            idx_copy(i + 2, cur).start()

    x1 = x1_ref[...]
    y = _swiglu(x1.astype(BF16), wg_ref[...], wu_ref[...], wd_ref[...])
    r = lax.broadcasted_iota(jnp.int32, (tm, tm), 0)
    c = lax.broadcasted_iota(jnp.int32, (tm, tm), 1)
    w_cols = _mm_nt(jnp.where(r == c, 1.0, 0.0).astype(F32), wts_ref[...], HI)
    for j in range(TOP_K):
        pltpu.make_async_copy(ys_hbm.at[pl.ds(0, tm * SLABS)], gbuf.at[cur, j], gsem.at[cur]).wait()
    for j in range(TOP_K):
        y = y + w_cols[:, j:j + 1] * _from_token_tiles(gbuf.at[cur, j], tm)
    o_ref[...] = _ln(ALPHA * x1 + y, g2_ref[...], b2_ref[...])


def _final(dest, ys, x1, wts, w_gate_s, w_up_s, w_down_s, ln2_g, ln2_b, tm):
    t = x1.shape[0]
    any_spec = pl.BlockSpec(memory_space=pl.ANY)
    row = pl.BlockSpec((tm, D_MODEL), lambda i: (i, 0))
    full = lambda a: pl.BlockSpec(a.shape, lambda i: (0, 0))
    return pl.pallas_call(
        functools.partial(_final_kernel, tm=tm),
        grid=(t // tm,),
        in_specs=[any_spec, any_spec, row, pl.BlockSpec((TOP_K, tm), lambda i: (0, i)),
                  full(w_gate_s), full(w_up_s), full(w_down_s), full(ln2_g), full(ln2_b)],
        out_specs=row,
        out_shape=jax.ShapeDtypeStruct((t, D_MODEL), F32),
        scratch_shapes=[pltpu.SMEM((2, TOP_K, tm), jnp.int32),
                        pltpu.VMEM((2, TOP_K, tm * SLABS, LANES), U32),
                        pltpu.SemaphoreType.DMA((2,)),
                        pltpu.SemaphoreType.DMA((2,))],
        compiler_params=pltpu.CompilerParams(dimension_semantics=("arbitrary",),
                                             vmem_limit_bytes=VMEM_LIMIT),
        name="shared_combine_ln2",
    )(dest, ys, x1, wts, w_gate_s, w_up_s, w_down_s, ln2_g, ln2_b)


def _trunk(x, prm):
    b, l, d = x.shape
    t = b * l
    tm = 512
    row2 = lambda a: a.reshape(1, -1).astype(F32)
    xn, p_rwkv, q, k, v, ga, gn = _ln_proj(x.reshape(t, d), row2(prm['ln_in_g']), row2(prm['ln_in_b']),
                                           prm['w_in_rwkv'], prm['w_in_na'], prm['w_in_gate'], tm)
    wkv_f, wkv_b, bonus, gate = _rwkv_branch(p_rwkv, b, l, prm, tm)
    y_n = _na_attention(q.reshape(b, l, D_NA), k.reshape(b, l, D_NA), v.reshape(b, l, D_NA),
                        prm['na_bias'])
    x1, x1t, sct = _merge(wkv_f, wkv_b, bonus, gate, row2(prm['lnx_g']), row2(prm['lnx_b']),
                          y_n.reshape(t, D_NA), ga, gn, xn, prm['w_up_a'], prm['w_up_n'], prm['w_out'],
                          row2(prm['ln1_g']), row2(prm['ln1_b']), prm['w_router_t'], tm)
    return _moe(x1, x1t, sct, prm).reshape(b, l, d)


def _moe(x1, x1t, sct, prm):
    t = x1.shape[0]
    tm = MOE_TILE
    idx, wts, tile_cnt = _route(sct, prm['e_bias'], tm)
    tile_base, counts, padded, pstart, block_e, n_used, n_blocks = _block_plan(
        tile_cnt[:, 0, :].astype(jnp.int32), t)
    dest = _dest_rows(idx, tile_base, tm)
    xs = _dispatch(x1t, dest, counts, padded, pstart, n_blocks, tm)
    ys = _experts(xs, block_e, n_used, prm['w_gate_up_e'], prm['w_down_e'])
    row2 = lambda a: a.reshape(1, -1).astype(F32)
    return _final(dest, ys, x1, wts, prm['w_gate_s'], prm['w_up_s'], prm['w_down_s'],
                  row2(prm['ln2_g']), row2(prm['ln2_b']), tm)


def _split_bf16(w):
    hi = w.astype(BF16)
    return jnp.stack([hi, (w - hi.astype(F32)).astype(BF16)])


def kernel(x_prompt, x_sample, ln_in_g, ln_in_b, w_in, mu_shift, w0, w2, a0, a2, g2, k_k, k_a, r_k,
           lnx_g, lnx_b, rpb, w_up_a, w_up_n, w_out, ln1_g, ln1_b, w_router, e_bias,
           w_gate_e, w_up_e, w_down_e, w_gate_s, w_up_s, w_down_s, ln2_g, ln2_b):
    assert w_in.shape[0] == DEPTH == 1
    w_in0 = w_in[0].astype(BF16)
    prm = dict(
        ln_in_g=ln_in_g, ln_in_b=ln_in_b,
        w_in_rwkv=w_in0[:, :RWKV_IN], w_in_na=w_in0[:, RWKV_IN:RWKV_IN + NA_IN],
        w_in_gate=w_in0[:, RWKV_IN + NA_IN:],
        mu_shift=mu_shift[0], w0=w0[0], w2=w2[0], a0=a0[0], a2=a2[0], g2=g2[0], k_k=k_k[0], k_a=k_a[0],
        r_k=r_k[0], lnx_g=lnx_g[0], lnx_b=lnx_b[0], na_bias=_na_bias_table(rpb[0]),
        w_up_a=w_up_a[0].astype(BF16), w_up_n=w_up_n[0].astype(BF16), w_out=w_out[0].astype(BF16),
        ln1_g=ln1_g[0], ln1_b=ln1_b[0], w_router_t=_split_bf16(w_router[0].T), e_bias=e_bias[0],
        w_gate_up_e=jnp.concatenate([w_gate_e[0].astype(BF16), w_up_e[0].astype(BF16)], axis=-1),
        w_down_e=w_down_e[0].astype(BF16),
        w_gate_s=w_gate_s[0].astype(BF16), w_up_s=w_up_s[0].astype(BF16),
        w_down_s=w_down_s[0].astype(BF16), ln2_g=ln2_g[0], ln2_b=ln2_b[0])
    return (_trunk(x_prompt, prm), _trunk(x_sample, prm))
```

```python
import functools

import jax
import jax.numpy as jnp
import numpy as np
from jax import lax
from jax.experimental import pallas as pl
from jax.experimental.pallas import tpu as pltpu

F32 = jnp.float32
BF16 = jnp.bfloat16

D_MODEL = 1024
GRID_W = 64
HEAD_DIM = 64
D_RWKV = 512
D_NA = 512
NA_WIN_H = 8
NA_WIN_W = 16
DECAY_LORA = 64
ICL_LORA = 64
GATE_LORA = 128
N_EXPERTS = 256
TOP_K = 8
N_GROUPS = 8
TOPK_GROUPS = 4
D_EXPERT = 256
ROUTED_SCALE = 2.5
LN_EPS = 1e-5
GN_EPS = 64e-5
DEPTH = 1
ALPHA = (2 * DEPTH) ** 0.25
RWKV_SIZES = (D_RWKV, D_RWKV, D_RWKV, DECAY_LORA, DECAY_LORA, ICL_LORA, ICL_LORA, GATE_LORA)
RWKV_IN = sum(RWKV_SIZES)
NA_IN = 3 * D_NA

LANES = 128
HEADS_PER_SLAB = LANES // HEAD_DIM
SLABS = D_MODEL // LANES // 2
U32 = jnp.uint32
VMEM_LIMIT = 48 * 1024 * 1024

CHUNK = 64
RWKV_SEQS_PER_STEP = 2
HALO_ROWS = 16
NA_ROWS_PER_STEP = 2
MOE_ROWS = 128
MOE_TILE = 256
MOE_GROUP = 4
NEG_BIG = -1e30

HI = lax.Precision.HIGHEST


def _mm(a, b, precision=None):
    return jnp.dot(a, b, preferred_element_type=F32, precision=precision)


def _mm_nt(a, b, precision=None):
    return lax.dot_general(a, b, (((1,), (1,)), ((), ())), preferred_element_type=F32,
                           precision=precision)


def _ln(x, g, b):
    mu = jnp.mean(x, -1, keepdims=True)
    xc = x - mu
    var = jnp.mean(xc * xc, -1, keepdims=True)
    return xc * lax.rsqrt(var + LN_EPS) * g + b


def _ln_proj_kernel(x_ref, g_ref, b_ref, wr_ref, wn_ref, wg_ref,
                    xn_ref, pr_ref, q_ref, k_ref, v_ref, ga_ref, gn_ref):
    xn = _ln(x_ref[...], g_ref[...], b_ref[...])
    xb = xn.astype(BF16)
    xn_ref[...] = xb
    pr_ref[...] = _mm(xb, wr_ref[...]).astype(BF16)
    pn = _mm(xb, wn_ref[...])
    q_ref[...] = pn[:, :D_NA].astype(BF16)
    k_ref[...] = pn[:, D_NA:2 * D_NA].astype(BF16)
    v_ref[...] = pn[:, 2 * D_NA:].astype(BF16)
    pg = _mm(xb, wg_ref[...])
    ga_ref[...] = pg[:, :D_MODEL].astype(BF16)
    gn_ref[...] = pg[:, D_MODEL:].astype(BF16)


def _ln_proj(x, g, b, w_rwkv, w_na, w_gate, tm):
    t = x.shape[0]
    row = lambda n: pl.BlockSpec((tm, n), lambda i: (i, 0))
    full = lambda a: pl.BlockSpec(a.shape, lambda i: (0, 0))
    return pl.pallas_call(
        _ln_proj_kernel,
        grid=(t // tm,),
        in_specs=[row(D_MODEL), full(g), full(b), full(w_rwkv), full(w_na), full(w_gate)],
        out_specs=[row(D_MODEL), row(RWKV_IN), row(D_NA), row(D_NA), row(D_NA),
                   row(D_MODEL), row(D_MODEL)],
        out_shape=[jax.ShapeDtypeStruct((t, D_MODEL), BF16),
                   jax.ShapeDtypeStruct((t, RWKV_IN), BF16),
                   jax.ShapeDtypeStruct((t, D_NA), BF16),
                   jax.ShapeDtypeStruct((t, D_NA), BF16),
                   jax.ShapeDtypeStruct((t, D_NA), BF16),
                   jax.ShapeDtypeStruct((t, D_MODEL), BF16),
                   jax.ShapeDtypeStruct((t, D_MODEL), BF16)],
        compiler_params=pltpu.CompilerParams(dimension_semantics=("parallel",),
                                             vmem_limit_bytes=VMEM_LIMIT),
        name="ln_proj",
    )(x, g, b, w_rwkv, w_na, w_gate)


def _rwkv_kernel(rf_ref, vf_ref, kkf_ref, kdf_ref, bef_ref, lwf_ref,
                 rb_ref, vb_ref, kkb_ref, kdb_ref, beb_ref, lwb_ref, yf_ref, yb_ref, z_ref, *, n_slabs, n_seq):
    @pl.when(pl.program_id(1) == 0)
    def _():
        z_ref[...] = jnp.zeros_like(z_ref)

    n2 = HEADS_PER_SLAB * CHUNK
    row = lax.broadcasted_iota(jnp.int32, (n2, n2), 0)
    col = lax.broadcasted_iota(jnp.int32, (n2, n2), 1)
    dts = (row & (CHUNK - 1)) - (col & (CHUNK - 1))
    blk16 = ((row & (CHUNK - 1)) >> 4) == ((col & (CHUNK - 1)) >> 4)
    eye = row == col
    rowi = lax.broadcasted_iota(jnp.int32, (CHUNK, LANES), 0)
    lane = lax.broadcasted_iota(jnp.int32, (CHUNK, LANES), 1)
    head0 = lane < HEAD_DIM

    def bd(x):
        return jnp.concatenate([jnp.where(head0, x, 0.0), jnp.where(head0, 0.0, x)], axis=0)

    def bf(x):
        return x.astype(BF16)

    chains = [(d, (q, p)) for q in range(n_seq) for d in range(2) for p in range(n_slabs)]
    sls = [(q, slice(p * LANES, (p + 1) * LANES)) for _, (q, p) in chains]
    strict = [dts > 0 if d == 0 else dts < 0 for d, _ in chains]
    incl = [dts >= 0 if d == 0 else dts <= 0 for d, _ in chains]
    r_in = [(rf_ref, rb_ref)[d] for d, _ in chains]
    v_in = [(vf_ref, vb_ref)[d] for d, _ in chains]
    kk_in = [(kkf_ref, kkb_ref)[d] for d, _ in chains]
    kd_in = [(kdf_ref, kdb_ref)[d] for d, _ in chains]
    be_in = [(bef_ref, beb_ref)[d] for d, _ in chains]
    lw_in = [(lwf_ref, lwb_ref)[d] for d, _ in chains]
    ident = jnp.where(eye, 1.0, 0.0).astype(F32)

    def each(f, *cols):
        return [f(*args) for args in zip(*cols)]

    lw = each(lambda ref, sl: ref[0, sl[0], :, sl[1]], lw_in, sls)
    def scan_rows(x, reverse):
        step = 1
        while step < CHUNK:
            if reverse:
                x = x + jnp.where(rowi < CHUNK - step, pltpu.roll(x, CHUNK - step, axis=0), 0.0)
            else:
                x = x + jnp.where(rowi >= step, pltpu.roll(x, step, axis=0), 0.0)
            step *= 2
        return x

    cum = [scan_rows(x, d == 1) for x, (d, _) in zip(lw, chains)]
    tot = each(lambda x: jnp.sum(x, axis=0, keepdims=True), lw)
    e_in = each(jnp.exp, cum)
    e_ex = each(lambda c_, l_: jnp.exp(c_ - l_), cum, lw)
    e_neg = each(lambda c_: jnp.exp(-c_), cum)
    e_rem = each(lambda t_, c_: jnp.exp(t_ - c_), tot, cum)
    gam = each(jnp.exp, tot)
    kk = each(lambda ref, sl: ref[sl[0], :, sl[1]].astype(F32), kk_in, sls)
    kd = each(lambda ref, sl: ref[0, sl[0], :, sl[1]].astype(F32), kd_in, sls)
    be = each(lambda ref, sl: ref[0, sl[0], :, sl[1]].astype(F32), be_in, sls)
    a_b = each(lambda x, e: bf(bd(x * e)), kk, e_ex)
    r_t = each(lambda ref, e, sl: bd(ref[sl[0], :, sl[1]].astype(F32) * e), r_in, e_in, sls)
    b_t = each(lambda x, e: bd(x * e), be, e_neg)
    k_t = each(lambda x, e: bd(x * e), kd, e_neg)
    b_h = each(lambda x, e: bd(x * e), be, e_rem)
    k_h = each(lambda x, e: bd(x * e), kd, e_rem)
    vv = each(lambda ref, sl: bf(bd(ref[sl[0], :, sl[1]].astype(F32))), v_in, sls)

    sc = each(lambda a, r, b, k: _mm_nt(jnp.concatenate([a, bf(r)], axis=0),
                                        bf(jnp.concatenate([b, k], axis=0))), a_b, r_t, b_t, k_t)
    lk = each(lambda m, s: bf(jnp.where(m, s[:n2, n2:], 0.0)), strict, sc)
    mb = each(lambda m, s: bf(jnp.where(m, s[n2:, :n2], 0.0)), incl, sc)
    mk = each(lambda m, s: bf(jnp.where(m, s[n2:, n2:], 0.0)), incl, sc)

    nn = each(lambda m, s: jnp.where(m, -s[:n2, :n2], 0.0), strict, sc)
    dg = each(lambda x: jnp.where(blk16, x, 0.0), nn)
    offb = each(lambda x, d_: bf(x - d_), nn, dg)
    def mm2(xs, ys):
        return each(_mm, xs, ys)

    d1b = each(bf, dg)
    d2b = each(bf, mm2(d1b, d1b))
    d4b = each(bf, mm2(d2b, d2b))
    d8b = each(bf, mm2(d4b, d4b))
    td = each(lambda d_: ident + d_, dg)
    for dpow in (d2b, d4b, d8b):
        td = each(lambda t_, m_: t_ + m_, td, mm2(each(bf, td), dpow))
    tdb = each(bf, td)
    e1 = mm2(tdb, offb)
    e1b = each(bf, e1)
    e2 = mm2(e1b, e1b)
    e3 = mm2(e1b, each(bf, e2))
    ttb = each(bf, mm2(each(lambda x1_, x2_, x3_: bf(ident + x1_ + x2_ + x3_), e1, e2, e3), tdb))

    lkv = mm2(lk, vv)
    wb = each(lambda t_, a, l_: bf(_mm(t_, jnp.concatenate([a, bf(l_)], axis=1))), ttb, a_b, lkv)
    mw = each(_mm, mb, wb)
    mkv = mm2(mk, vv)
    bw = each(lambda b, w: _mm(bf(b.T), w), b_h, wb)
    kv = mm2(each(lambda k: bf(k.T), k_h), vv)
    r_hat = each(lambda r, m: bf(r - m[:, :n2]), r_t, mw)
    y_loc = each(lambda m, w: m - w[:, n2:], mkv, mw)
    g = each(lambda g_, b: bf(jnp.where(eye, jnp.broadcast_to(g_, (n2, n2)), 0.0) - b[:, :n2]), gam, bw)
    h = each(lambda k, b: k - b[:, n2:], kv, bw)

    n_ch = len(chains)
    z = [z_ref[i] for i in range(n_ch)]
    z_hi = each(bf, z)
    z_lo = each(lambda z_, zh: bf(z_ - zh.astype(F32)), z, z_hi)
    y = each(lambda m_, yl: m_ + yl, mm2(r_hat, z_hi), y_loc)
    gz = each(lambda g_, zh, zl: _mm(g_, jnp.concatenate([zh, zl], axis=1)), g, z_hi, z_lo)
    z_new = each(lambda m_, h_: m_[:, :n2] + m_[:, n2:] + h_, gz, h)
    for i, (d, _) in enumerate(chains):
        z_ref[i] = z_new[i]
        (yf_ref, yb_ref)[d][sls[i][0], :, sls[i][1]] = (y[i][:CHUNK] + y[i][CHUNK:]).astype(BF16)


def _rwkv_scan(r, v, kk, kd, be, lw):
    b, l, dr = r.shape
    nc = l // CHUNK
    n_slabs = dr // LANES
    n_seq = RWKV_SEQS_PER_STEP if b % RWKV_SEQS_PER_STEP == 0 else 1
    fwd = pl.BlockSpec((n_seq, CHUNK, dr), lambda bi, c: (bi, c, 0))
    bwd = pl.BlockSpec((n_seq, CHUNK, dr), lambda bi, c: (bi, nc - 1 - c, 0))
    fwd_d = pl.BlockSpec((1, n_seq, CHUNK, dr), lambda bi, c: (0, bi, c, 0))
    bwd_d = pl.BlockSpec((1, n_seq, CHUNK, dr), lambda bi, c: (1, bi, nc - 1 - c, 0))
    return pl.pallas_call(
        functools.partial(_rwkv_kernel, n_slabs=n_slabs, n_seq=n_seq),
        grid=(b // n_seq, nc),
        in_specs=[fwd, fwd, fwd, fwd_d, fwd_d, fwd_d, bwd, bwd, bwd, bwd_d, bwd_d, bwd_d],
        out_specs=[fwd, bwd],
        out_shape=[jax.ShapeDtypeStruct((b, l, dr), BF16), jax.ShapeDtypeStruct((b, l, dr), BF16)],
        scratch_shapes=[pltpu.VMEM((2 * n_slabs * n_seq, HEADS_PER_SLAB * CHUNK, HEADS_PER_SLAB * HEAD_DIM),
                                   F32)],
        compiler_params=pltpu.CompilerParams(
            dimension_semantics=("parallel", "arbitrary"),
            vmem_limit_bytes=VMEM_LIMIT),
        name="rwkv_scan",
    )(r, v, kk, kd, be, lw, r, v, kk, kd, be, lw)


def _rwkv_prep_kernel(p_ref, prev_ref, next_ref, mu_ref, wdec_ref, w0_ref, wicl_ref, a0_ref, g2_ref,
                      kk_w_ref, ka_ref, rk_ref, hsum_ref,
                      r_ref, v_ref, kk_ref, kd_ref, be_ref, lw_ref, bonus_ref, gate_ref, *, tiles_per_seq):
    i = pl.program_id(0)
    p = p_ref[...].astype(F32)
    tm = p.shape[0]
    pos = i % tiles_per_seq
    prev_row = jnp.where(pos == 0, 0.0, prev_ref[...].astype(F32)[HALO_ROWS - 1:HALO_ROWS, :])
    next_row = jnp.where(pos == tiles_per_seq - 1, 0.0, next_ref[...].astype(F32)[0:1, :])
    rowi = lax.broadcasted_iota(jnp.int32, (tm, 1), 0)
    up = jnp.where(rowi == 0, prev_row, pltpu.roll(p, 1, axis=0))
    dn = jnp.where(rowi == tm - 1, next_row, pltpu.roll(p, tm - 1, axis=0))
    ps = p + mu_ref[...] * (0.5 * (up + dn) - p)

    c = np.cumsum((0,) + RWKV_SIZES)
    xr, xk, xv = ps[:, c[0]:c[1]], ps[:, c[1]:c[2]], ps[:, c[2]:c[3]]
    dw, da, dg = ps[:, c[3]:c[5]], ps[:, c[5]:c[7]], ps[:, c[7]:c[8]]
    wl = w0_ref[...] + _mm(jnp.tanh(dw).astype(BF16), wdec_ref[...])
    neg = -wl
    softplus = jnp.maximum(neg, 0.0) + jnp.log(1.0 + jnp.exp(-jnp.abs(neg)))
    lw = -jnp.exp(-softplus - 0.5)
    a = jax.nn.sigmoid(a0_ref[...] + _mm(da.astype(BF16), wicl_ref[...]))
    kkr = xk * kk_w_ref[...]
    ss = _mm((kkr * kkr).astype(BF16), hsum_ref[...])
    kk = kkr * lax.rsqrt(jnp.maximum(ss, 1e-24))
    kd_sum = jnp.zeros_like(xk)
    for d in range(2):
        a_d = a[:, d * D_RWKV:(d + 1) * D_RWKV]
        kd = xk * (1.0 + (a_d - 1.0) * ka_ref[...])
        kd_sum = kd_sum + kd
        kd_ref[d] = kd.astype(BF16)
        be_ref[d] = (a_d * kk).astype(BF16)
        lw_ref[d] = lw[:, d * D_RWKV:(d + 1) * D_RWKV]
    r_ref[...] = xr.astype(BF16)
    v_ref[...] = xv.astype(BF16)
    kk_ref[...] = kk.astype(BF16)
    bonus_ref[...] = (_mm((xr * rk_ref[...] * kd_sum).astype(BF16), hsum_ref[...]) * xv).astype(BF16)
    gate_ref[...] = _mm(jax.nn.sigmoid(dg).astype(BF16), g2_ref[...]).astype(BF16)


def _rwkv_branch(p_rwkv, b, l, prm, tm):
    t = b * l
    r, v, kk, kd, be, lw, bonus, gate = _rwkv_prep(p_rwkv, l, prm, tm)
    seq = lambda a: a.reshape(b, l, D_RWKV)
    seq2 = lambda a: a.reshape(2, b, l, D_RWKV)
    wkv_f, wkv_b = _rwkv_scan(seq(r), seq(v), seq(kk), seq2(kd), seq2(be), seq2(lw))
    return wkv_f.reshape(t, D_RWKV), wkv_b.reshape(t, D_RWKV), bonus, gate


def _block_diag2(w):
    z = jnp.zeros_like(w[0])
    return jnp.concatenate([jnp.concatenate([w[0], z], axis=1), jnp.concatenate([z, w[1]], axis=1)], axis=0)


def _head_sum_matrix(scale):
    h = np.arange(D_RWKV) // HEAD_DIM
    return jnp.asarray((h[:, None] == h[None, :]) * scale, BF16)


def _rwkv_prep(p_rwkv, seq_len, prm, tm):
    t = p_rwkv.shape[0]
    row2 = lambda a: a.reshape(1, -1).astype(F32)
    consts = [row2(prm['mu_shift']), _block_diag2(prm['w2']).astype(BF16), row2(prm['w0']),
              _block_diag2(prm['a2']).astype(BF16), row2(prm['a0']), prm['g2'].astype(BF16),
              row2(prm['k_k']), row2(prm['k_a']), row2(prm['r_k']), _head_sum_matrix(1.0)]
    full = lambda a: pl.BlockSpec(a.shape, lambda i: (0, 0))
    row = pl.BlockSpec((tm, D_RWKV), lambda i: (i, 0))
    row_d = pl.BlockSpec((2, tm, D_RWKV), lambda i: (0, i, 0))
    halo = tm // HALO_ROWS
    sd = lambda dt: jax.ShapeDtypeStruct((t, D_RWKV), dt)
    sd2 = lambda dt: jax.ShapeDtypeStruct((2, t, D_RWKV), dt)
    return pl.pallas_call(
        functools.partial(_rwkv_prep_kernel, tiles_per_seq=seq_len // tm),
        grid=(t // tm,),
        in_specs=[pl.BlockSpec((tm, RWKV_IN), lambda i: (i, 0)),
                  pl.BlockSpec((HALO_ROWS, RWKV_IN), lambda i: (jnp.maximum(i * halo - 1, 0), 0)),
                  pl.BlockSpec((HALO_ROWS, RWKV_IN),
                               lambda i: (jnp.minimum((i + 1) * halo, t // HALO_ROWS - 1), 0))]
        + [full(a) for a in consts],
        out_specs=[row, row, row, row_d, row_d, row_d, row, row],
        out_shape=[sd(BF16), sd(BF16), sd(BF16), sd2(BF16), sd2(BF16), sd2(F32), sd(BF16), sd(BF16)],
        compiler_params=pltpu.CompilerParams(dimension_semantics=("parallel",),
                                             vmem_limit_bytes=VMEM_LIMIT),
        name="rwkv_prep",
    )(p_rwkv, p_rwkv, p_rwkv, *consts)


def _na_bias_table(rpb):
    n_h = rpb.shape[0]
    var = np.arange(NA_WIN_H)[:, None]
    i = np.arange(NA_WIN_H)[None, :]
    c = np.arange(GRID_W)[:, None]
    kc = np.arange(GRID_W)[None, :]
    cs = np.clip(c - NA_WIN_W // 2, 0, GRID_W - NA_WIN_W)
    valid = (kc >= cs) & (kc < cs + NA_WIN_W)
    row_sel = (np.arange(2 * NA_WIN_H - 1)[None, None, :] == (i - var + NA_WIN_H - 1)[:, :, None])
    col_sel = (np.arange(2 * NA_WIN_W - 1)[None, None, :] == (kc - c + NA_WIN_W - 1)[:, :, None])
    col_sel = col_sel & valid[:, :, None]
    tab = jnp.einsum('hab,via,ckb->vhcik', rpb.astype(F32), row_sel.astype(np.float32),
                     col_sel.astype(np.float32), precision=HI)
    tab = jnp.where(valid[None, None, :, None, :], tab, NEG_BIG)
    return tab.reshape(NA_WIN_H, n_h * GRID_W, NA_WIN_H * GRID_W)


def _na_kernel(q_ref, k_ref, v_ref, *refs, rows, n_slabs):
    bias_refs, o_ref = refs[:-1], refs[-1]
    band = NA_WIN_H * GRID_W
    scale = HEAD_DIM ** -0.5
    lane = lax.broadcasted_iota(jnp.int32, (GRID_W, LANES), 1)
    head0 = lane < HEAD_DIM
    chains = [(i, p) for i in range(NA_ROWS_PER_STEP) for p in range(n_slabs)]
    starts = []
    for i in range(NA_ROWS_PER_STEP):
        r = pl.program_id(1) * NA_ROWS_PER_STEP + i
        rs = jnp.clip(r - NA_WIN_H // 2, 0, rows - NA_WIN_H)
        starts.append(pl.multiple_of(rs * GRID_W, GRID_W))

    def lanes(p):
        return slice(p * LANES, (p + 1) * LANES)

    def split_heads(q2):
        zero = jnp.zeros_like(q2)
        return jnp.concatenate([jnp.where(head0, q2, zero), jnp.where(head0, zero, q2)], axis=0)

    lhs = [split_heads(q_ref[0, i * GRID_W:(i + 1) * GRID_W, lanes(p)]) for i, p in chains]
    s = [_mm_nt(x, k_ref[0, pl.ds(starts[i], band), lanes(p)]) for x, (i, p) in zip(lhs, chains)]
    s = [x * scale + bias_refs[i][0, p * 2 * GRID_W:(p + 1) * 2 * GRID_W, :] for x, (i, p) in zip(s, chains)]
    m = [jnp.max(x, axis=-1, keepdims=True) for x in s]
    e = [jnp.exp(x - mx) for x, mx in zip(s, m)]
    den = [jnp.sum(x, axis=-1, keepdims=True) for x in e]
    o = [_mm(x.astype(BF16), v_ref[0, pl.ds(starts[i], band), lanes(p)]) for x, (i, p) in zip(e, chains)]
    for x, d, (i, p) in zip(o, den, chains):
        x = x / d
        o_ref[0, i * GRID_W:(i + 1) * GRID_W, lanes(p)] = (
            jnp.where(head0, x[:GRID_W], x[GRID_W:]).astype(o_ref.dtype))


def _na_attention(q, k, v, bias_tab):
    b, l, dn = q.shape
    rows = l // GRID_W
    assert rows >= NA_WIN_H and rows % NA_ROWS_PER_STEP == 0
    n_slabs = dn // LANES
    qrows = NA_ROWS_PER_STEP * GRID_W

    def bias_spec(i):
        def bias_idx(bi, g):
            r = g * NA_ROWS_PER_STEP + i
            rs = jnp.clip(r - NA_WIN_H // 2, 0, rows - NA_WIN_H)
            return (r - rs, 0, 0)
        return pl.BlockSpec((1,) + bias_tab.shape[1:], bias_idx)

    return pl.pallas_call(
        functools.partial(_na_kernel, rows=rows, n_slabs=n_slabs),
        grid=(b, rows // NA_ROWS_PER_STEP),
        in_specs=[pl.BlockSpec((1, qrows, dn), lambda bi, g: (bi, g, 0)),
                  pl.BlockSpec((1, l, dn), lambda bi, g: (bi, 0, 0)),
                  pl.BlockSpec((1, l, dn), lambda bi, g: (bi, 0, 0))]
        + [bias_spec(i) for i in range(NA_ROWS_PER_STEP)],
        out_specs=pl.BlockSpec((1, qrows, dn), lambda bi, g: (bi, g, 0)),
        out_shape=jax.ShapeDtypeStruct((b, l, dn), BF16),
        compiler_params=pltpu.CompilerParams(dimension_semantics=("parallel", "arbitrary"),
                                             vmem_limit_bytes=VMEM_LIMIT),
        name="na_attention",
    )(q, k, v, *([bias_tab] * NA_ROWS_PER_STEP))


def _to_token_tiles(dst_ref, x):
    n = x.shape[0]
    for s in range(SLABS):
        lo = _bf16_bits(x[:, s * LANES:(s + 1) * LANES])
        hi = _bf16_bits(x[:, (s + SLABS) * LANES:(s + SLABS + 1) * LANES])
        dst_ref[pl.ds(s, n, stride=SLABS), :] = (lo >> 16) | hi


def _from_token_tiles(src_ref, n):
    words = [src_ref[pl.ds(s, n, stride=SLABS), :] for s in range(SLABS)]
    lo = [lax.bitcast_convert_type(w << 16, F32) for w in words]
    hi = [lax.bitcast_convert_type(w & U32(0xFFFF0000), F32) for w in words]
    return jnp.concatenate(lo + hi, axis=1)


def _bf16_bits(x):
    b = lax.bitcast_convert_type(x, U32)
    return (b + U32(0x7FFF) + ((b >> 16) & U32(1))) & U32(0xFFFF0000)


def _merge_kernel(wf_ref, wb_ref, bonus_ref, gate_ref, havg_ref, lg_ref, lb_ref,
                  yn_ref, ga_ref, gn_ref, xn_ref, wa_ref, wn_ref, wo_ref, g1_ref, b1_ref,
                  wrt_ref, x1_ref, x1t_ref, sct_ref):
    wkv = wf_ref[...].astype(F32) + wb_ref[...].astype(F32)
    xc = wkv - _mm(wkv.astype(BF16), havg_ref[...])
    var = _mm((xc * xc).astype(BF16), havg_ref[...])
    ya = ((xc * lax.rsqrt(var + GN_EPS) * lg_ref[...] + lb_ref[...] + bonus_ref[...].astype(F32))
          * gate_ref[...].astype(F32))
    up_a = _mm(ya.astype(BF16), wa_ref[...])
    up_n = _mm(yn_ref[...], wn_ref[...])
    merged = (jax.nn.sigmoid(ga_ref[...].astype(F32)) * up_a
              + jax.nn.sigmoid(gn_ref[...].astype(F32)) * up_n)
    mix = _mm(merged.astype(BF16), wo_ref[...])
    x1 = _ln(ALPHA * xn_ref[...].astype(F32) + mix, g1_ref[...], b1_ref[...])
    x1_ref[...] = x1
    _to_token_tiles(x1t_ref, x1)
    w_hi, w_lo = wrt_ref[0], wrt_ref[1]
    x_hi = x1.astype(BF16)
    x_lo = (x1 - x_hi.astype(F32)).astype(BF16)
    sct_ref[...] = jax.nn.sigmoid(_mm_nt(w_hi, x_hi) + _mm_nt(w_hi, x_lo) + _mm_nt(w_lo, x_hi))


def _merge(wkv_f, wkv_b, bonus, gate, lnx_g, lnx_b, yn, ga, gn, xn, w_up_a, w_up_n, w_out, ln1_g, ln1_b,
           w_router_t, tm):
    t = xn.shape[0]
    row = lambda n: pl.BlockSpec((tm, n), lambda i: (i, 0))
    full = lambda a: pl.BlockSpec(a.shape, lambda i: (0,) * a.ndim)
    havg = _head_sum_matrix(1.0 / HEAD_DIM)
    return pl.pallas_call(
        _merge_kernel,
        grid=(t // tm,),
        in_specs=[row(D_RWKV), row(D_RWKV), row(D_RWKV), row(D_RWKV), full(havg), full(lnx_g), full(lnx_b),
                  row(D_NA), row(D_MODEL), row(D_MODEL), row(D_MODEL),
                  full(w_up_a), full(w_up_n), full(w_out), full(ln1_g), full(ln1_b), full(w_router_t)],
        out_specs=[row(D_MODEL), pl.BlockSpec((tm * SLABS, LANES), lambda i: (i, 0)),
                   pl.BlockSpec((N_EXPERTS, tm), lambda i: (0, i))],
        out_shape=[jax.ShapeDtypeStruct((t, D_MODEL), F32),
                   jax.ShapeDtypeStruct((t * SLABS, LANES), U32),
                   jax.ShapeDtypeStruct((N_EXPERTS, t), F32)],
        compiler_params=pltpu.CompilerParams(dimension_semantics=("parallel",),
                                             vmem_limit_bytes=VMEM_LIMIT),
        name="merge_ln1_router",
    )(wkv_f, wkv_b, bonus, gate, havg, lnx_g, lnx_b, yn, ga, gn, xn, w_up_a, w_up_n, w_out, ln1_g, ln1_b,
      w_router_t)


def _first_argmax(vals, iota):
    m = jnp.max(vals, axis=0, keepdims=True)
    first = jnp.min(jnp.where(vals == m, iota, float(vals.shape[0])), axis=0, keepdims=True)
    return m, first


def _route_kernel(sct_ref, bias_ref, idx_ref, wts_ref, cnt_ref):
    s = sct_ref[...]
    tm = s.shape[1]
    sel = s + bias_ref[...]
    gsz = N_EXPERTS // N_GROUPS
    iota_g = lax.broadcasted_iota(jnp.int32, (gsz, tm), 0).astype(F32)
    iota_8 = lax.broadcasted_iota(jnp.int32, (N_GROUPS, tm), 0).astype(F32)
    iota_e = lax.broadcasted_iota(jnp.int32, (N_EXPERTS, tm), 0).astype(F32)

    gs = jnp.zeros((N_GROUPS, tm), F32)
    for g in range(N_GROUPS):
        blk = sel[g * gsz:(g + 1) * gsz, :]
        m1, i1 = _first_argmax(blk, iota_g)
        m2 = jnp.max(jnp.where(iota_g == i1, -jnp.inf, blk), axis=0, keepdims=True)
        gs = jnp.where(iota_8 == float(g), m1 + m2, gs)

    chosen = jnp.zeros((N_GROUPS, tm), F32)
    for _ in range(TOPK_GROUPS):
        _, gi = _first_argmax(gs, iota_8)
        hit = iota_8 == gi
        chosen = jnp.where(hit, 1.0, chosen)
        gs = jnp.where(hit, -jnp.inf, gs)
    mask_e = jnp.concatenate([jnp.broadcast_to(chosen[g:g + 1, :], (gsz, tm)) for g in range(N_GROUPS)],
                             axis=0)
    cand = jnp.where(mask_e > 0.0, sel, -jnp.inf)

    iota_k = lax.broadcasted_iota(jnp.int32, (TOP_K, tm), 0)
    idx = jnp.zeros((TOP_K, tm), F32)
    wts = jnp.zeros((TOP_K, tm), F32)
    member = jnp.zeros((N_EXPERTS, tm), F32)
    for j in range(TOP_K):
        _, ij = _first_argmax(cand, iota_e)
        hit = iota_e == ij
        wj = jnp.sum(jnp.where(hit, s, 0.0), axis=0, keepdims=True)
        cand = jnp.where(hit, -jnp.inf, cand)
        member = jnp.where(hit, 1.0, member)
        idx = jnp.where(iota_k == j, ij, idx)
        wts = jnp.where(iota_k == j, wj, wts)
    wts = wts / jnp.sum(wts, axis=0, keepdims=True) * ROUTED_SCALE
    idx_ref[...] = idx.astype(jnp.int32)
    wts_ref[...] = wts
    cnt_ref[0] = _mm_nt(jnp.ones((8, tm), BF16), member.astype(BF16))


def _route(sct, e_bias, tm):
    n_e, t = sct.shape
    nt = t // tm
    return pl.pallas_call(
        _route_kernel,
        grid=(nt,),
        in_specs=[pl.BlockSpec((n_e, tm), lambda i: (0, i)),
                  pl.BlockSpec((n_e, 1), lambda i: (0, 0))],
        out_specs=[pl.BlockSpec((TOP_K, tm), lambda i: (0, i)),
                   pl.BlockSpec((TOP_K, tm), lambda i: (0, i)),
                   pl.BlockSpec((1, 8, n_e), lambda i: (i, 0, 0))],
        out_shape=[jax.ShapeDtypeStruct((TOP_K, t), jnp.int32),
                   jax.ShapeDtypeStruct((TOP_K, t), F32),
                   jax.ShapeDtypeStruct((nt, 8, n_e), F32)],
        compiler_params=pltpu.CompilerParams(dimension_semantics=("parallel",),
                                             vmem_limit_bytes=VMEM_LIMIT),
        name="moe_route",
    )(sct, e_bias.reshape(n_e, 1).astype(F32))


def _dest_kernel(idx_ref, base_ref, dest_ref):
    idx = idx_ref[...].astype(F32)
    tm = idx.shape[1]
    iota_e = lax.broadcasted_iota(jnp.int32, (N_EXPERTS, tm), 0).astype(F32)
    hits = [iota_e == idx[j:j + 1, :] for j in range(TOP_K)]
    member = jnp.zeros((N_EXPERTS, tm), F32)
    for hit in hits:
        member = jnp.where(hit, 1.0, member)
    r = lax.broadcasted_iota(jnp.int32, (tm, tm), 0)
    c = lax.broadcasted_iota(jnp.int32, (tm, tm), 1)
    earlier = jnp.where(r < c, 1.0, 0.0).astype(BF16)
    rank = _mm(member.astype(BF16), earlier) + base_ref[0]
    iota_k = lax.broadcasted_iota(jnp.int32, (TOP_K, tm), 0)
    dest = jnp.zeros((TOP_K, tm), F32)
    for j, hit in enumerate(hits):
        dj = jnp.sum(jnp.where(hit, rank, 0.0), axis=0, keepdims=True)
        dest = jnp.where(iota_k == j, dj, dest)
    dest_ref[0] = dest.astype(jnp.int32)


def _dest_rows(idx, base, tm):
    t = idx.shape[1]
    nt = t // tm
    return pl.pallas_call(
        _dest_kernel,
        grid=(nt,),
        in_specs=[pl.BlockSpec((TOP_K, tm), lambda i: (0, i)),
                  pl.BlockSpec((1, N_EXPERTS, 1), lambda i: (i, 0, 0))],
        out_specs=pl.BlockSpec((1, TOP_K, tm), lambda i: (i, 0, 0)),
        out_shape=jax.ShapeDtypeStruct((nt, TOP_K, tm), jnp.int32),
        compiler_params=pltpu.CompilerParams(dimension_semantics=("parallel",),
                                             vmem_limit_bytes=VMEM_LIMIT),
        name="moe_dest",
    )(idx, base)


def _block_plan(tile_counts, n_tokens):
    n_blocks = (n_tokens * TOP_K + N_EXPERTS * (MOE_ROWS - 1) + MOE_ROWS - 1) // MOE_ROWS
    n_blocks = -(-n_blocks // MOE_GROUP) * MOE_GROUP
    counts = jnp.sum(tile_counts, axis=0)
    padded = (counts + MOE_ROWS - 1) // MOE_ROWS * MOE_ROWS
    extra = (-(jnp.sum(padded) // MOE_ROWS)) % MOE_GROUP
    padded = padded.at[N_EXPERTS - 1].add(extra * MOE_ROWS)
    pend = jnp.cumsum(padded)
    pstart = pend - padded
    tile_base = pstart[None, :] + jnp.cumsum(tile_counts, axis=0) - tile_counts
    block_start = jnp.arange(n_blocks, dtype=jnp.int32) * MOE_ROWS
    block_e = jnp.minimum(jnp.sum(pend[None, :] <= block_start[:, None], axis=1),
                          N_EXPERTS - 1).astype(jnp.int32)
    n_used = (pend[-1] // MOE_ROWS).astype(jnp.int32).reshape(1)
    return (tile_base.astype(F32)[:, :, None], counts.astype(jnp.int32), padded.astype(jnp.int32),
            pstart.astype(jnp.int32), block_e, n_used, n_blocks)


def _row_tile(ref, row):
    return ref.at[pl.ds(pl.multiple_of(row * SLABS, SLABS), SLABS)]


def _dispatch_kernel(cnt_ref, pad_ref, pst_ref, dest_hbm, x_ref, xs_hbm, dest_s, zrow, isem, csem, zsem,
                     *, tm):
    i = pl.program_id(0)
    n = pl.num_programs(0)
    cur = i % 2

    def idx_copy(tile, buf):
        return pltpu.make_async_copy(dest_hbm.at[tile], dest_s.at[buf], isem.at[buf])

    @pl.when(i == 0)
    def _():
        idx_copy(0, 0).start()
        zrow[...] = jnp.zeros_like(zrow)

        def per_expert(e, carry, wait):
            def per_row(r, c2):
                cp = pltpu.make_async_copy(zrow, _row_tile(xs_hbm, pst_ref[e] + r), zsem)
                if wait:
                    cp.wait()
                else:
                    cp.start()
                return c2
            return lax.fori_loop(cnt_ref[e], pad_ref[e], per_row, carry)

        lax.fori_loop(0, N_EXPERTS, functools.partial(per_expert, wait=False), 0)
        lax.fori_loop(0, N_EXPERTS, functools.partial(per_expert, wait=True), 0)

    idx_copy(i, cur).wait()

    @pl.when(i + 1 < n)
    def _():
        idx_copy(i + 1, 1 - cur).start()

    def per_token(t, carry):
        src = x_ref.at[pl.ds(pl.multiple_of(t * SLABS, SLABS), SLABS)]
        for j in range(TOP_K):
            pltpu.make_async_copy(src, _row_tile(xs_hbm, dest_s[cur, j, t]), csem).start()
        return carry

    lax.fori_loop(0, tm, per_token, 0)
    for j in range(TOP_K):
        pltpu.make_async_copy(x_ref, xs_hbm.at[pl.ds(0, tm * SLABS)], csem).wait()


def _dispatch(x1t, dest, counts, padded, pstart, n_blocks, tm):
    nt = dest.shape[0]
    any_spec = pl.BlockSpec(memory_space=pl.ANY)
    grid_spec = pltpu.PrefetchScalarGridSpec(
        num_scalar_prefetch=3,
        grid=(nt,),
        in_specs=[any_spec, pl.BlockSpec((tm * SLABS, LANES), lambda i, c, p, s: (i, 0))],
        out_specs=any_spec,
        scratch_shapes=[pltpu.SMEM((2, TOP_K, tm), jnp.int32),
                        pltpu.VMEM((SLABS, LANES), U32),
                        pltpu.SemaphoreType.DMA((2,)),
                        pltpu.SemaphoreType.DMA(()),
                        pltpu.SemaphoreType.DMA(())])
    return pl.pallas_call(
        functools.partial(_dispatch_kernel, tm=tm),
        grid_spec=grid_spec,
        out_shape=jax.ShapeDtypeStruct((n_blocks * MOE_ROWS * SLABS, LANES), U32),
        compiler_params=pltpu.CompilerParams(dimension_semantics=("arbitrary",),
                                             vmem_limit_bytes=VMEM_LIMIT),
        name="moe_dispatch",
    )(counts, padded, pstart, dest, x1t)


def _swiglu(xb, wg, wu, wd):
    hg = _mm(xb, wg)
    hu = _mm(xb, wu)
    return _mm((hg * jax.nn.sigmoid(hg) * hu).astype(BF16), wd)


def _experts_kernel(be_ref, nu_ref, xs_ref, *refs):
    w_refs, ys_ref = refs[:-1], refs[-1]
    s = pl.program_id(0)
    blk = MOE_ROWS * SLABS

    @pl.when(MOE_GROUP * s < nu_ref[0])
    def _():
        parts = [pl.ds(k * blk, blk) for k in range(MOE_GROUP)]
        wgu = [w_refs[2 * k] for k in range(MOE_GROUP)]
        wd = [w_refs[2 * k + 1] for k in range(MOE_GROUP)]
        xb, hh = {}, {}
        for step in range(MOE_GROUP + 2):
            k = step
            if k < MOE_GROUP:
                xb[k] = _from_token_tiles(xs_ref.at[parts[k]], MOE_ROWS).astype(BF16)
            k = step - 1
            if 0 <= k < MOE_GROUP:
                hgu = _mm(xb[k], wgu[k][0])
                hg, hu = hgu[:, :D_EXPERT], hgu[:, D_EXPERT:]
                hh[k] = (hg * jax.nn.sigmoid(hg) * hu).astype(BF16)
            k = step - 2
            if 0 <= k < MOE_GROUP:
                _to_token_tiles(ys_ref.at[parts[k]], _mm(hh[k], wd[k][0]))

    @pl.when(MOE_GROUP * s >= nu_ref[0])
    def _():
        ys_ref[...] = jnp.zeros_like(ys_ref)


def _experts(xs, block_e, n_used, w_gate_up_e, w_down_e):
    n_groups = block_e.shape[0] // MOE_GROUP
    blkg = MOE_GROUP * MOE_ROWS * SLABS
    wspec = lambda shp, k: pl.BlockSpec((1,) + shp, lambda s, be, nu: (be[MOE_GROUP * s + k], 0, 0))
    wspecs = [wspec(shp, k) for k in range(MOE_GROUP)
              for shp in ((D_MODEL, 2 * D_EXPERT), (D_EXPERT, D_MODEL))]
    grid_spec = pltpu.PrefetchScalarGridSpec(
        num_scalar_prefetch=2,
        grid=(n_groups,),
        in_specs=[pl.BlockSpec((blkg, LANES),
                               lambda s, be, nu: (jnp.minimum(s, nu[0] // MOE_GROUP - 1), 0))] + wspecs,
        out_specs=pl.BlockSpec((blkg, LANES), lambda s, be, nu: (s, 0)))
    return pl.pallas_call(
        _experts_kernel,
        grid_spec=grid_spec,
        out_shape=jax.ShapeDtypeStruct(xs.shape, xs.dtype),
        compiler_params=pltpu.CompilerParams(dimension_semantics=("arbitrary",),
                                             vmem_limit_bytes=VMEM_LIMIT),
        name="moe_experts",
    )(block_e, n_used, xs, *([w_gate_up_e, w_down_e] * MOE_GROUP))


def _final_kernel(dest_hbm, ys_hbm, x1_ref, wts_ref, wg_ref, wu_ref, wd_ref, g2_ref, b2_ref, o_ref,
                  dest_s, gbuf, isem, gsem, *, tm):
    i = pl.program_id(0)
    n = pl.num_programs(0)
    cur = i % 2
    nxt = 1 - cur

    def idx_copy(tile, buf):
        return pltpu.make_async_copy(dest_hbm.at[tile], dest_s.at[buf], isem.at[buf])

    def start_gather(buf):
        def per_token(t, carry):
            for j in range(TOP_K):
                dst = gbuf.at[buf, j, pl.ds(pl.multiple_of(t * SLABS, SLABS), SLABS)]
                pltpu.make_async_copy(_row_tile(ys_hbm, dest_s[buf, j, t]), dst, gsem.at[buf]).start()
            return carry
        lax.fori_loop(0, tm, per_token, 0)

    @pl.when(i == 0)
    def _():
        idx_copy(0, 0).start()
        idx_copy(0, 0).wait()
        start_gather(0)

        @pl.when(1 < n)
        def _():
            idx_copy(1, 1).start()

    @pl.when(i + 1 < n)
    def _():
        idx_copy(i + 1, nxt).wait()
        start_gather(nxt)

        @pl.when(i + 2 < n)
        def _():
            idx_copy(i + 2, cur).start()

    x1 = x1_ref[...]
    y = _swiglu(x1.astype(BF16), wg_ref[...], wu_ref[...], wd_ref[...])
    r = lax.broadcasted_iota(jnp.int32, (tm, tm), 0)
    c = lax.broadcasted_iota(jnp.int32, (tm, tm), 1)
    w_cols = _mm_nt(jnp.where(r == c, 1.0, 0.0).astype(F32), wts_ref[...], HI)
    for j in range(TOP_K):
        pltpu.make_async_copy(ys_hbm.at[pl.ds(0, tm * SLABS)], gbuf.at[cur, j], gsem.at[cur]).wait()
    for j in range(TOP_K):
        y = y + w_cols[:, j:j + 1] * _from_token_tiles(gbuf.at[cur, j], tm)
    o_ref[...] = _ln(ALPHA * x1 + y, g2_ref[...], b2_ref[...])


def _final(dest, ys, x1, wts, w_gate_s, w_up_s, w_down_s, ln2_g, ln2_b, tm):
    t = x1.shape[0]
    any_spec = pl.BlockSpec(memory_space=pl.ANY)
    row = pl.BlockSpec((tm, D_MODEL), lambda i: (i, 0))
    full = lambda a: pl.BlockSpec(a.shape, lambda i: (0, 0))
    return pl.pallas_call(
        functools.partial(_final_kernel, tm=tm),
        grid=(t // tm,),
        in_specs=[any_spec, any_spec, row, pl.BlockSpec((TOP_K, tm), lambda i: (0, i)),
                  full(w_gate_s), full(w_up_s), full(w_down_s), full(ln2_g), full(ln2_b)],
        out_specs=row,
        out_shape=jax.ShapeDtypeStruct((t, D_MODEL), F32),
        scratch_shapes=[pltpu.SMEM((2, TOP_K, tm), jnp.int32),
                        pltpu.VMEM((2, TOP_K, tm * SLABS, LANES), U32),
                        pltpu.SemaphoreType.DMA((2,)),
                        pltpu.SemaphoreType.DMA((2,))],
        compiler_params=pltpu.CompilerParams(dimension_semantics=("arbitrary",),
                                             vmem_limit_bytes=VMEM_LIMIT),
        name="shared_combine_ln2",
    )(dest, ys, x1, wts, w_gate_s, w_up_s, w_down_s, ln2_g, ln2_b)


def _trunk(x, prm):
    b, l, d = x.shape
    t = b * l
    tm = 512
    row2 = lambda a: a.reshape(1, -1).astype(F32)
    xn, p_rwkv, q, k, v, ga, gn = _ln_proj(x.reshape(t, d), row2(prm['ln_in_g']), row2(prm['ln_in_b']),
                                           prm['w_in_rwkv'], prm['w_in_na'], prm['w_in_gate'], tm)
    wkv_f, wkv_b, bonus, gate = _rwkv_branch(p_rwkv, b, l, prm, tm)
    y_n = _na_attention(q.reshape(b, l, D_NA), k.reshape(b, l, D_NA), v.reshape(b, l, D_NA),
                        prm['na_bias'])
    x1, x1t, sct = _merge(wkv_f, wkv_b, bonus, gate, row2(prm['lnx_g']), row2(prm['lnx_b']),
                          y_n.reshape(t, D_NA), ga, gn, xn, prm['w_up_a'], prm['w_up_n'], prm['w_out'],
                          row2(prm['ln1_g']), row2(prm['ln1_b']), prm['w_router_t'], tm)
    return _moe(x1, x1t, sct, prm).reshape(b, l, d)


def _moe(x1, x1t, sct, prm):
    t = x1.shape[0]
    tm = MOE_TILE
    idx, wts, tile_cnt = _route(sct, prm['e_bias'], tm)
    tile_base, counts, padded, pstart, block_e, n_used, n_blocks = _block_plan(
        tile_cnt[:, 0, :].astype(jnp.int32), t)
    dest = _dest_rows(idx, tile_base, tm)
    xs = _dispatch(x1t, dest, counts, padded, pstart, n_blocks, tm)
    ys = _experts(xs, block_e, n_used, prm['w_gate_up_e'], prm['w_down_e'])
    row2 = lambda a: a.reshape(1, -1).astype(F32)
    return _final(dest, ys, x1, wts, prm['w_gate_s'], prm['w_up_s'], prm['w_down_s'],
                  row2(prm['ln2_g']), row2(prm['ln2_b']), tm)


def _split_bf16(w):
    hi = w.astype(BF16)
    return jnp.stack([hi, (w - hi.astype(F32)).astype(BF16)])


def kernel(x_prompt, x_sample, ln_in_g, ln_in_b, w_in, mu_shift, w0, w2, a0, a2, g2, k_k, k_a, r_k,
           lnx_g, lnx_b, rpb, w_up_a, w_up_n, w_out, ln1_g, ln1_b, w_router, e_bias,
           w_gate_e, w_up_e, w_down_e, w_gate_s, w_up_s, w_down_s, ln2_g, ln2_b):
    assert w_in.shape[0] == DEPTH == 1
    w_in0 = w_in[0].astype(BF16)
    prm = dict(
        ln_in_g=ln_in_g, ln_in_b=ln_in_b,
        w_in_rwkv=w_in0[:, :RWKV_IN], w_in_na=w_in0[:, RWKV_IN:RWKV_IN + NA_IN],
        w_in_gate=w_in0[:, RWKV_IN + NA_IN:],
        mu_shift=mu_shift[0], w0=w0[0], w2=w2[0], a0=a0[0], a2=a2[0], g2=g2[0], k_k=k_k[0], k_a=k_a[0],
        r_k=r_k[0], lnx_g=lnx_g[0], lnx_b=lnx_b[0], na_bias=_na_bias_table(rpb[0]),
        w_up_a=w_up_a[0].astype(BF16), w_up_n=w_up_n[0].astype(BF16), w_out=w_out[0].astype(BF16),
        ln1_g=ln1_g[0], ln1_b=ln1_b[0], w_router_t=_split_bf16(w_router[0].T), e_bias=e_bias[0],
        w_gate_up_e=jnp.concatenate([w_gate_e[0].astype(BF16), w_up_e[0].astype(BF16)], axis=-1),
        w_down_e=w_down_e[0].astype(BF16),
        w_gate_s=w_gate_s[0].astype(BF16), w_up_s=w_up_s[0].astype(BF16),
        w_down_s=w_down_s[0].astype(BF16), ln2_g=ln2_g[0], ln2_b=ln2_b[0])
    return (_trunk(x_prompt, prm), _trunk(x_sample, prm))
```

```python
import functools

import jax
import jax.numpy as jnp
import numpy as np
from jax import lax
from jax.experimental import pallas as pl
from jax.experimental.pallas import tpu as pltpu

F32 = jnp.float32
BF16 = jnp.bfloat16

D_MODEL = 1024
GRID_W = 64
HEAD_DIM = 64
D_RWKV = 512
D_NA = 512
NA_WIN_H = 8
NA_WIN_W = 16
DECAY_LORA = 64
ICL_LORA = 64
GATE_LORA = 128
N_EXPERTS = 256
TOP_K = 8
N_GROUPS = 8
TOPK_GROUPS = 4
D_EXPERT = 256
ROUTED_SCALE = 2.5
LN_EPS = 1e-5
GN_EPS = 64e-5
DEPTH = 1
ALPHA = (2 * DEPTH) ** 0.25
RWKV_SIZES = (D_RWKV, D_RWKV, D_RWKV, DECAY_LORA, DECAY_LORA, ICL_LORA, ICL_LORA, GATE_LORA)
RWKV_IN = sum(RWKV_SIZES)
NA_IN = 3 * D_NA

LANES = 128
HEADS_PER_SLAB = LANES // HEAD_DIM
SLABS = D_MODEL // LANES // 2
U32 = jnp.uint32
VMEM_LIMIT = 48 * 1024 * 1024

CHUNK = 64
RWKV_SEQS_PER_STEP = 2
HALO_ROWS = 16
NA_ROWS_PER_STEP = 2
MOE_ROWS = 128
MOE_TILE = 256
MOE_GROUP = 4
NEG_BIG = -1e30

HI = lax.Precision.HIGHEST


def _mm(a, b, precision=None):
    return jnp.dot(a, b, preferred_element_type=F32, precision=precision)


def _mm_nt(a, b, precision=None):
    return lax.dot_general(a, b, (((1,), (1,)), ((), ())), preferred_element_type=F32,
                           precision=precision)


def _ln(x, g, b):
    mu = jnp.mean(x, -1, keepdims=True)
    xc = x - mu
    var = jnp.mean(xc * xc, -1, keepdims=True)
    return xc * lax.rsqrt(var + LN_EPS) * g + b


def _ln_proj_kernel(x_ref, g_ref, b_ref, wr_ref, wn_ref, wg_ref,
                    xn_ref, pr_ref, q_ref, k_ref, v_ref, ga_ref, gn_ref):
    xn = _ln(x_ref[...], g_ref[...], b_ref[...])
    xn_ref[...] = xn
    xb = xn.astype(BF16)
    pr_ref[...] = _mm(xb, wr_ref[...]).astype(BF16)
    pn = _mm(xb, wn_ref[...])
    q_ref[...] = pn[:, :D_NA].astype(BF16)
    k_ref[...] = pn[:, D_NA:2 * D_NA].astype(BF16)
    v_ref[...] = pn[:, 2 * D_NA:].astype(BF16)
    pg = _mm(xb, wg_ref[...])
    ga_ref[...] = pg[:, :D_MODEL].astype(BF16)
    gn_ref[...] = pg[:, D_MODEL:].astype(BF16)


def _ln_proj(x, g, b, w_rwkv, w_na, w_gate, tm):
    t = x.shape[0]
    row = lambda n: pl.BlockSpec((tm, n), lambda i: (i, 0))
    full = lambda a: pl.BlockSpec(a.shape, lambda i: (0, 0))
    return pl.pallas_call(
        _ln_proj_kernel,
        grid=(t // tm,),
        in_specs=[row(D_MODEL), full(g), full(b), full(w_rwkv), full(w_na), full(w_gate)],
        out_specs=[row(D_MODEL), row(RWKV_IN), row(D_NA), row(D_NA), row(D_NA),
                   row(D_MODEL), row(D_MODEL)],
        out_shape=[jax.ShapeDtypeStruct((t, D_MODEL), F32),
                   jax.ShapeDtypeStruct((t, RWKV_IN), BF16),
                   jax.ShapeDtypeStruct((t, D_NA), BF16),
                   jax.ShapeDtypeStruct((t, D_NA), BF16),
                   jax.ShapeDtypeStruct((t, D_NA), BF16),
                   jax.ShapeDtypeStruct((t, D_MODEL), BF16),
                   jax.ShapeDtypeStruct((t, D_MODEL), BF16)],
        compiler_params=pltpu.CompilerParams(dimension_semantics=("parallel",),
                                             vmem_limit_bytes=VMEM_LIMIT),
        name="ln_proj",
    )(x, g, b, w_rwkv, w_na, w_gate)


def _rwkv_kernel(rf_ref, vf_ref, kkf_ref, kdf_ref, bef_ref, lwf_ref,
                 rb_ref, vb_ref, kkb_ref, kdb_ref, beb_ref, lwb_ref, yf_ref, yb_ref, z_ref, *, n_slabs, n_seq):
    @pl.when(pl.program_id(1) == 0)
    def _():
        z_ref[...] = jnp.zeros_like(z_ref)

    n2 = HEADS_PER_SLAB * CHUNK
    row = lax.broadcasted_iota(jnp.int32, (n2, n2), 0)
    col = lax.broadcasted_iota(jnp.int32, (n2, n2), 1)
    dts = (row & (CHUNK - 1)) - (col & (CHUNK - 1))
    blk16 = ((row & (CHUNK - 1)) >> 4) == ((col & (CHUNK - 1)) >> 4)
    eye = row == col
    rowi = lax.broadcasted_iota(jnp.int32, (CHUNK, LANES), 0)
    lane = lax.broadcasted_iota(jnp.int32, (CHUNK, LANES), 1)
    head0 = lane < HEAD_DIM

    def bd(x):
        return jnp.concatenate([jnp.where(head0, x, 0.0), jnp.where(head0, 0.0, x)], axis=0)

    def bf(x):
        return x.astype(BF16)

    chains = [(d, (q, p)) for q in range(n_seq) for d in range(2) for p in range(n_slabs)]
    sls = [(q, slice(p * LANES, (p + 1) * LANES)) for _, (q, p) in chains]
    strict = [dts > 0 if d == 0 else dts < 0 for d, _ in chains]
    incl = [dts >= 0 if d == 0 else dts <= 0 for d, _ in chains]
    r_in = [(rf_ref, rb_ref)[d] for d, _ in chains]
    v_in = [(vf_ref, vb_ref)[d] for d, _ in chains]
    kk_in = [(kkf_ref, kkb_ref)[d] for d, _ in chains]
    kd_in = [(kdf_ref, kdb_ref)[d] for d, _ in chains]
    be_in = [(bef_ref, beb_ref)[d] for d, _ in chains]
    lw_in = [(lwf_ref, lwb_ref)[d] for d, _ in chains]
    ident = jnp.where(eye, 1.0, 0.0).astype(F32)

    def each(f, *cols):
        return [f(*args) for args in zip(*cols)]

    lw = each(lambda ref, sl: ref[0, sl[0], :, sl[1]], lw_in, sls)
    def scan_rows(x, reverse):
        step = 1
        while step < CHUNK:
            if reverse:
                x = x + jnp.where(rowi < CHUNK - step, pltpu.roll(x, CHUNK - step, axis=0), 0.0)
            else:
                x = x + jnp.where(rowi >= step, pltpu.roll(x, step, axis=0), 0.0)
            step *= 2
        return x

    cum = [scan_rows(x, d == 1) for x, (d, _) in zip(lw, chains)]
    tot = each(lambda x: jnp.sum(x, axis=0, keepdims=True), lw)
    e_in = each(jnp.exp, cum)
    e_ex = each(lambda c_, l_: jnp.exp(c_ - l_), cum, lw)
    e_neg = each(lambda c_: jnp.exp(-c_), cum)
    e_rem = each(lambda t_, c_: jnp.exp(t_ - c_), tot, cum)
    gam = each(jnp.exp, tot)
    kk = each(lambda ref, sl: ref[sl[0], :, sl[1]].astype(F32), kk_in, sls)
    kd = each(lambda ref, sl: ref[0, sl[0], :, sl[1]].astype(F32), kd_in, sls)
    be = each(lambda ref, sl: ref[0, sl[0], :, sl[1]].astype(F32), be_in, sls)
    a_b = each(lambda x, e: bf(bd(x * e)), kk, e_ex)
    r_t = each(lambda ref, e, sl: bd(ref[sl[0], :, sl[1]].astype(F32) * e), r_in, e_in, sls)
    b_t = each(lambda x, e: bd(x * e), be, e_neg)
    k_t = each(lambda x, e: bd(x * e), kd, e_neg)
    b_h = each(lambda x, e: bd(x * e), be, e_rem)
    k_h = each(lambda x, e: bd(x * e), kd, e_rem)
    vv = each(lambda ref, sl: bf(bd(ref[sl[0], :, sl[1]].astype(F32))), v_in, sls)

    sc = each(lambda a, r, b, k: _mm_nt(jnp.concatenate([a, bf(r)], axis=0),
                                        bf(jnp.concatenate([b, k], axis=0))), a_b, r_t, b_t, k_t)
    lk = each(lambda m, s: bf(jnp.where(m, s[:n2, n2:], 0.0)), strict, sc)
    mb = each(lambda m, s: bf(jnp.where(m, s[n2:, :n2], 0.0)), incl, sc)
    mk = each(lambda m, s: bf(jnp.where(m, s[n2:, n2:], 0.0)), incl, sc)

    nn = each(lambda m, s: jnp.where(m, -s[:n2, :n2], 0.0), strict, sc)
    dg = each(lambda x: jnp.where(blk16, x, 0.0), nn)
    offb = each(lambda x, d_: bf(x - d_), nn, dg)
    def mm2(xs, ys):
        return each(_mm, xs, ys)

    d1b = each(bf, dg)
    d2b = each(bf, mm2(d1b, d1b))
    d4b = each(bf, mm2(d2b, d2b))
    d8b = each(bf, mm2(d4b, d4b))
    td = each(lambda d_: ident + d_, dg)
    for dpow in (d2b, d4b, d8b):
        td = each(lambda t_, m_: t_ + m_, td, mm2(each(bf, td), dpow))
    tdb = each(bf, td)
    e1 = mm2(tdb, offb)
    e1b = each(bf, e1)
    e2 = mm2(e1b, e1b)
    e3 = mm2(e1b, each(bf, e2))
    ttb = each(bf, mm2(each(lambda x1_, x2_, x3_: bf(ident + x1_ + x2_ + x3_), e1, e2, e3), tdb))

    lkv = mm2(lk, vv)
    wb = each(lambda t_, a, l_: bf(_mm(t_, jnp.concatenate([a, bf(l_)], axis=1))), ttb, a_b, lkv)
    mw = each(_mm, mb, wb)
    mkv = mm2(mk, vv)
    bw = each(lambda b, w: _mm(bf(b.T), w), b_h, wb)
    kv = mm2(each(lambda k: bf(k.T), k_h), vv)
    r_hat = each(lambda r, m: bf(r - m[:, :n2]), r_t, mw)
    y_loc = each(lambda m, w: m - w[:, n2:], mkv, mw)
    g = each(lambda g_, b: bf(jnp.where(eye, jnp.broadcast_to(g_, (n2, n2)), 0.0) - b[:, :n2]), gam, bw)
    h = each(lambda k, b: k - b[:, n2:], kv, bw)

    n_ch = len(chains)
    z = [z_ref[i] for i in range(n_ch)]
    z_hi = each(bf, z)
    z_lo = each(lambda z_, zh: bf(z_ - zh.astype(F32)), z, z_hi)
    y = each(lambda m_, yl: m_ + yl, mm2(r_hat, z_hi), y_loc)
    gz = each(lambda g_, zh, zl: _mm(g_, jnp.concatenate([zh, zl], axis=1)), g, z_hi, z_lo)
    z_new = each(lambda m_, h_: m_[:, :n2] + m_[:, n2:] + h_, gz, h)
    for i, (d, _) in enumerate(chains):
        z_ref[i] = z_new[i]
        (yf_ref, yb_ref)[d][sls[i][0], :, sls[i][1]] = (y[i][:CHUNK] + y[i][CHUNK:]).astype(BF16)


def _rwkv_scan(r, v, kk, kd, be, lw):
    b, l, dr = r.shape
    nc = l // CHUNK
    n_slabs = dr // LANES
    n_seq = RWKV_SEQS_PER_STEP if b % RWKV_SEQS_PER_STEP == 0 else 1
    fwd = pl.BlockSpec((n_seq, CHUNK, dr), lambda bi, c: (bi, c, 0))
    bwd = pl.BlockSpec((n_seq, CHUNK, dr), lambda bi, c: (bi, nc - 1 - c, 0))
    fwd_d = pl.BlockSpec((1, n_seq, CHUNK, dr), lambda bi, c: (0, bi, c, 0))
    bwd_d = pl.BlockSpec((1, n_seq, CHUNK, dr), lambda bi, c: (1, bi, nc - 1 - c, 0))
    return pl.pallas_call(
        functools.partial(_rwkv_kernel, n_slabs=n_slabs, n_seq=n_seq),
        grid=(b // n_seq, nc),
        in_specs=[fwd, fwd, fwd, fwd_d, fwd_d, fwd_d, bwd, bwd, bwd, bwd_d, bwd_d, bwd_d],
        out_specs=[fwd, bwd],
        out_shape=[jax.ShapeDtypeStruct((b, l, dr), BF16), jax.ShapeDtypeStruct((b, l, dr), BF16)],
        scratch_shapes=[pltpu.VMEM((2 * n_slabs * n_seq, HEADS_PER_SLAB * CHUNK, HEADS_PER_SLAB * HEAD_DIM),
                                   F32)],
        compiler_params=pltpu.CompilerParams(
            dimension_semantics=("parallel", "arbitrary"),
            vmem_limit_bytes=VMEM_LIMIT),
        name="rwkv_scan",
    )(r, v, kk, kd, be, lw, r, v, kk, kd, be, lw)


def _rwkv_prep_kernel(p_ref, prev_ref, next_ref, mu_ref, wdec_ref, w0_ref, wicl_ref, a0_ref, g2_ref,
                      kk_w_ref, ka_ref, rk_ref, hsum_ref,
                      r_ref, v_ref, kk_ref, kd_ref, be_ref, lw_ref, bonus_ref, gate_ref, *, tiles_per_seq):
    i = pl.program_id(0)
    p = p_ref[...].astype(F32)
    tm = p.shape[0]
    pos = i % tiles_per_seq
    prev_row = jnp.where(pos == 0, 0.0, prev_ref[...].astype(F32)[HALO_ROWS - 1:HALO_ROWS, :])
    next_row = jnp.where(pos == tiles_per_seq - 1, 0.0, next_ref[...].astype(F32)[0:1, :])
    rowi = lax.broadcasted_iota(jnp.int32, (tm, 1), 0)
    up = jnp.where(rowi == 0, prev_row, pltpu.roll(p, 1, axis=0))
    dn = jnp.where(rowi == tm - 1, next_row, pltpu.roll(p, tm - 1, axis=0))
    ps = p + mu_ref[...] * (0.5 * (up + dn) - p)

    c = np.cumsum((0,) + RWKV_SIZES)
    xr, xk, xv = ps[:, c[0]:c[1]], ps[:, c[1]:c[2]], ps[:, c[2]:c[3]]
    dw, da, dg = ps[:, c[3]:c[5]], ps[:, c[5]:c[7]], ps[:, c[7]:c[8]]
    wl = w0_ref[...] + _mm(jnp.tanh(dw).astype(BF16), wdec_ref[...])
    lw = jax.nn.sigmoid(wl) * (-np.exp(-0.5))
    a = jax.nn.sigmoid(a0_ref[...] + _mm(da.astype(BF16), wicl_ref[...]))
    kkr = xk * kk_w_ref[...]
    ss = _mm((kkr * kkr).astype(BF16), hsum_ref[...])
    kk = kkr * lax.rsqrt(jnp.maximum(ss, 1e-24))
    kd_sum = jnp.zeros_like(xk)
    for d in range(2):
        a_d = a[:, d * D_RWKV:(d + 1) * D_RWKV]
        kd = xk * (1.0 + (a_d - 1.0) * ka_ref[...])
        kd_sum = kd_sum + kd
        kd_ref[d] = kd.astype(BF16)
        be_ref[d] = (a_d * kk).astype(BF16)
        lw_ref[d] = lw[:, d * D_RWKV:(d + 1) * D_RWKV]
    r_ref[...] = xr.astype(BF16)
    v_ref[...] = xv.astype(BF16)
    kk_ref[...] = kk.astype(BF16)
    bonus_ref[...] = (_mm((xr * rk_ref[...] * kd_sum).astype(BF16), hsum_ref[...]) * xv).astype(BF16)
    gate_ref[...] = _mm(jax.nn.sigmoid(dg).astype(BF16), g2_ref[...]).astype(BF16)


def _rwkv_branch(p_rwkv, b, l, prm, tm):
    t = b * l
    r, v, kk, kd, be, lw, bonus, gate = _rwkv_prep(p_rwkv, l, prm, tm)
    seq = lambda a: a.reshape(b, l, D_RWKV)
    seq2 = lambda a: a.reshape(2, b, l, D_RWKV)
    wkv_f, wkv_b = _rwkv_scan(seq(r), seq(v), seq(kk), seq2(kd), seq2(be), seq2(lw))
    return wkv_f.reshape(t, D_RWKV), wkv_b.reshape(t, D_RWKV), bonus, gate


def _block_diag2(w):
    z = jnp.zeros_like(w[0])
    return jnp.concatenate([jnp.concatenate([w[0], z], axis=1), jnp.concatenate([z, w[1]], axis=1)], axis=0)


def _head_sum_matrix(scale):
    h = np.arange(D_RWKV) // HEAD_DIM
    return jnp.asarray((h[:, None] == h[None, :]) * scale, BF16)


def _rwkv_prep(p_rwkv, seq_len, prm, tm):
    t = p_rwkv.shape[0]
    row2 = lambda a: a.reshape(1, -1).astype(F32)
    consts = [row2(prm['mu_shift']), _block_diag2(prm['w2']).astype(BF16), row2(prm['w0']),
              _block_diag2(prm['a2']).astype(BF16), row2(prm['a0']), prm['g2'].astype(BF16),
              row2(prm['k_k']), row2(prm['k_a']), row2(prm['r_k']), _head_sum_matrix(1.0)]
    full = lambda a: pl.BlockSpec(a.shape, lambda i: (0, 0))
    row = pl.BlockSpec((tm, D_RWKV), lambda i: (i, 0))
    row_d = pl.BlockSpec((2, tm, D_RWKV), lambda i: (0, i, 0))
    halo = tm // HALO_ROWS
    sd = lambda dt: jax.ShapeDtypeStruct((t, D_RWKV), dt)
    sd2 = lambda dt: jax.ShapeDtypeStruct((2, t, D_RWKV), dt)
    return pl.pallas_call(
        functools.partial(_rwkv_prep_kernel, tiles_per_seq=seq_len // tm),
        grid=(t // tm,),
        in_specs=[pl.BlockSpec((tm, RWKV_IN), lambda i: (i, 0)),
                  pl.BlockSpec((HALO_ROWS, RWKV_IN), lambda i: (jnp.maximum(i * halo - 1, 0), 0)),
                  pl.BlockSpec((HALO_ROWS, RWKV_IN),
                               lambda i: (jnp.minimum((i + 1) * halo, t // HALO_ROWS - 1), 0))]
        + [full(a) for a in consts],
        out_specs=[row, row, row, row_d, row_d, row_d, row, row],
        out_shape=[sd(BF16), sd(BF16), sd(BF16), sd2(BF16), sd2(BF16), sd2(F32), sd(BF16), sd(BF16)],
        compiler_params=pltpu.CompilerParams(dimension_semantics=("parallel",),
                                             vmem_limit_bytes=VMEM_LIMIT),
        name="rwkv_prep",
    )(p_rwkv, p_rwkv, p_rwkv, *consts)


def _na_bias_table(rpb):
    n_h = rpb.shape[0]
    var = np.arange(NA_WIN_H)[:, None]
    i = np.arange(NA_WIN_H)[None, :]
    c = np.arange(GRID_W)[:, None]
    kc = np.arange(GRID_W)[None, :]
    cs = np.clip(c - NA_WIN_W // 2, 0, GRID_W - NA_WIN_W)
    valid = (kc >= cs) & (kc < cs + NA_WIN_W)
    row_sel = (np.arange(2 * NA_WIN_H - 1)[None, None, :] == (i - var + NA_WIN_H - 1)[:, :, None])
    col_sel = (np.arange(2 * NA_WIN_W - 1)[None, None, :] == (kc - c + NA_WIN_W - 1)[:, :, None])
    col_sel = col_sel & valid[:, :, None]
    tab = jnp.einsum('hab,via,ckb->vhcik', rpb.astype(F32), row_sel.astype(np.float32),
                     col_sel.astype(np.float32), precision=HI)
    tab = jnp.where(valid[None, None, :, None, :], tab, NEG_BIG)
    return tab.reshape(NA_WIN_H, n_h * GRID_W, NA_WIN_H * GRID_W)


def _na_kernel(q_ref, k_ref, v_ref, *refs, rows, n_slabs):
    bias_refs, o_ref = refs[:-1], refs[-1]
    band = NA_WIN_H * GRID_W
    scale = HEAD_DIM ** -0.5
    lane = lax.broadcasted_iota(jnp.int32, (GRID_W, LANES), 1)
    head0 = lane < HEAD_DIM
    chains = [(i, p) for i in range(NA_ROWS_PER_STEP) for p in range(n_slabs)]
    starts = []
    for i in range(NA_ROWS_PER_STEP):
        r = pl.program_id(1) * NA_ROWS_PER_STEP + i
        rs = jnp.clip(r - NA_WIN_H // 2, 0, rows - NA_WIN_H)
        starts.append(pl.multiple_of(rs * GRID_W, GRID_W))

    def lanes(p):
        return slice(p * LANES, (p + 1) * LANES)

    def split_heads(q2):
        zero = jnp.zeros_like(q2)
        return jnp.concatenate([jnp.where(head0, q2, zero), jnp.where(head0, zero, q2)], axis=0)

    lhs = [split_heads(q_ref[0, i * GRID_W:(i + 1) * GRID_W, lanes(p)]) for i, p in chains]
    s = [_mm_nt(x, k_ref[0, pl.ds(starts[i], band), lanes(p)]) for x, (i, p) in zip(lhs, chains)]
    s = [x * scale + bias_refs[i][0, p * 2 * GRID_W:(p + 1) * 2 * GRID_W, :] for x, (i, p) in zip(s, chains)]
    m = [jnp.max(x, axis=-1, keepdims=True) for x in s]
    e = [jnp.exp(x - mx) for x, mx in zip(s, m)]
    den = [jnp.sum(x, axis=-1, keepdims=True) for x in e]
    o = [_mm(x.astype(BF16), v_ref[0, pl.ds(starts[i], band), lanes(p)]) for x, (i, p) in zip(e, chains)]
    for x, d, (i, p) in zip(o, den, chains):
        x = x / d
        o_ref[0, i * GRID_W:(i + 1) * GRID_W, lanes(p)] = (
            jnp.where(head0, x[:GRID_W], x[GRID_W:]).astype(o_ref.dtype))


def _na_attention(q, k, v, bias_tab):
    b, l, dn = q.shape
    rows = l // GRID_W
    assert rows >= NA_WIN_H and rows % NA_ROWS_PER_STEP == 0
    n_slabs = dn // LANES
    qrows = NA_ROWS_PER_STEP * GRID_W

    def bias_spec(i):
        def bias_idx(bi, g):
            r = g * NA_ROWS_PER_STEP + i
            rs = jnp.clip(r - NA_WIN_H // 2, 0, rows - NA_WIN_H)
            return (r - rs, 0, 0)
        return pl.BlockSpec((1,) + bias_tab.shape[1:], bias_idx)

    return pl.pallas_call(
        functools.partial(_na_kernel, rows=rows, n_slabs=n_slabs),
        grid=(b, rows // NA_ROWS_PER_STEP),
        in_specs=[pl.BlockSpec((1, qrows, dn), lambda bi, g: (bi, g, 0)),
                  pl.BlockSpec((1, l, dn), lambda bi, g: (bi, 0, 0)),
                  pl.BlockSpec((1, l, dn), lambda bi, g: (bi, 0, 0))]
        + [bias_spec(i) for i in range(NA_ROWS_PER_STEP)],
        out_specs=pl.BlockSpec((1, qrows, dn), lambda bi, g: (bi, g, 0)),
        out_shape=jax.ShapeDtypeStruct((b, l, dn), BF16),
        compiler_params=pltpu.CompilerParams(dimension_semantics=("parallel", "arbitrary"),
                                             vmem_limit_bytes=VMEM_LIMIT),
        name="na_attention",
    )(q, k, v, *([bias_tab] * NA_ROWS_PER_STEP))


def _to_token_tiles(dst_ref, x):
    n = x.shape[0]
    for s in range(SLABS):
        lo = _bf16_bits(x[:, s * LANES:(s + 1) * LANES])
        hi = _bf16_bits(x[:, (s + SLABS) * LANES:(s + SLABS + 1) * LANES])
        dst_ref[pl.ds(s, n, stride=SLABS), :] = (lo >> 16) | hi


def _from_token_tiles(src_ref, n):
    words = [src_ref[pl.ds(s, n, stride=SLABS), :] for s in range(SLABS)]
    lo = [lax.bitcast_convert_type(w << 16, F32) for w in words]
    hi = [lax.bitcast_convert_type(w & U32(0xFFFF0000), F32) for w in words]
    return jnp.concatenate(lo + hi, axis=1)


def _bf16_bits(x):
    b = lax.bitcast_convert_type(x, U32)
    return (b + U32(0x7FFF) + ((b >> 16) & U32(1))) & U32(0xFFFF0000)


def _merge_kernel(wf_ref, wb_ref, bonus_ref, gate_ref, havg_ref, lg_ref, lb_ref,
                  yn_ref, ga_ref, gn_ref, xn_ref, wa_ref, wn_ref, wo_ref, g1_ref, b1_ref,
                  wrt_ref, x1_ref, x1t_ref, sct_ref):
    wkv = wf_ref[...].astype(F32) + wb_ref[...].astype(F32)
    xc = wkv - _mm(wkv.astype(BF16), havg_ref[...])
    var = _mm((xc * xc).astype(BF16), havg_ref[...])
    ya = ((xc * lax.rsqrt(var + GN_EPS) * lg_ref[...] + lb_ref[...] + bonus_ref[...].astype(F32))
          * gate_ref[...].astype(F32))
    up_a = _mm(ya.astype(BF16), wa_ref[...])
    up_n = _mm(yn_ref[...], wn_ref[...])
    merged = (jax.nn.sigmoid(ga_ref[...].astype(F32)) * up_a
              + jax.nn.sigmoid(gn_ref[...].astype(F32)) * up_n)
    mix = _mm(merged.astype(BF16), wo_ref[...])
    x1 = _ln(ALPHA * xn_ref[...] + mix, g1_ref[...], b1_ref[...])
    x1_ref[...] = x1
    _to_token_tiles(x1t_ref, x1)
    w_hi, w_lo = wrt_ref[0], wrt_ref[1]
    x_hi = x1.astype(BF16)
    x_lo = (x1 - x_hi.astype(F32)).astype(BF16)
    sct_ref[...] = jax.nn.sigmoid(_mm_nt(w_hi, x_hi) + _mm_nt(w_hi, x_lo) + _mm_nt(w_lo, x_hi))


def _merge(wkv_f, wkv_b, bonus, gate, lnx_g, lnx_b, yn, ga, gn, xn, w_up_a, w_up_n, w_out, ln1_g, ln1_b,
           w_router_t, tm):
    t = xn.shape[0]
    row = lambda n: pl.BlockSpec((tm, n), lambda i: (i, 0))
    full = lambda a: pl.BlockSpec(a.shape, lambda i: (0,) * a.ndim)
    havg = _head_sum_matrix(1.0 / HEAD_DIM)
    return pl.pallas_call(
        _merge_kernel,
        grid=(t // tm,),
        in_specs=[row(D_RWKV), row(D_RWKV), row(D_RWKV), row(D_RWKV), full(havg), full(lnx_g), full(lnx_b),
                  row(D_NA), row(D_MODEL), row(D_MODEL), row(D_MODEL),
                  full(w_up_a), full(w_up_n), full(w_out), full(ln1_g), full(ln1_b), full(w_router_t)],
        out_specs=[row(D_MODEL), pl.BlockSpec((tm * SLABS, LANES), lambda i: (i, 0)),
                   pl.BlockSpec((N_EXPERTS, tm), lambda i: (0, i))],
        out_shape=[jax.ShapeDtypeStruct((t, D_MODEL), F32),
                   jax.ShapeDtypeStruct((t * SLABS, LANES), U32),
                   jax.ShapeDtypeStruct((N_EXPERTS, t), F32)],
        compiler_params=pltpu.CompilerParams(dimension_semantics=("parallel",),
                                             vmem_limit_bytes=VMEM_LIMIT),
        name="merge_ln1_router",
    )(wkv_f, wkv_b, bonus, gate, havg, lnx_g, lnx_b, yn, ga, gn, xn, w_up_a, w_up_n, w_out, ln1_g, ln1_b,
      w_router_t)


def _first_argmax(vals, iota):
    m = jnp.max(vals, axis=0, keepdims=True)
    first = jnp.min(jnp.where(vals == m, iota, float(vals.shape[0])), axis=0, keepdims=True)
    return m, first


def _route_kernel(sct_ref, bias_ref, idx_ref, wts_ref, cnt_ref):
    s = sct_ref[...]
    tm = s.shape[1]
    sel = s + bias_ref[...]
    gsz = N_EXPERTS // N_GROUPS
    iota_g = lax.broadcasted_iota(jnp.int32, (gsz, tm), 0).astype(F32)
    iota_8 = lax.broadcasted_iota(jnp.int32, (N_GROUPS, tm), 0).astype(F32)
    iota_e = lax.broadcasted_iota(jnp.int32, (N_EXPERTS, tm), 0).astype(F32)

    gs = jnp.zeros((N_GROUPS, tm), F32)
    for g in range(N_GROUPS):
        blk = sel[g * gsz:(g + 1) * gsz, :]
        m1, i1 = _first_argmax(blk, iota_g)
        m2 = jnp.max(jnp.where(iota_g == i1, -jnp.inf, blk), axis=0, keepdims=True)
        gs = jnp.where(iota_8 == float(g), m1 + m2, gs)

    chosen = jnp.zeros((N_GROUPS, tm), F32)
    for _ in range(TOPK_GROUPS):
        _, gi = _first_argmax(gs, iota_8)
        hit = iota_8 == gi
        chosen = jnp.where(hit, 1.0, chosen)
        gs = jnp.where(hit, -jnp.inf, gs)
    mask_e = jnp.concatenate([jnp.broadcast_to(chosen[g:g + 1, :], (gsz, tm)) for g in range(N_GROUPS)],
                             axis=0)
    cand = jnp.where(mask_e > 0.0, sel, -jnp.inf)

    iota_k = lax.broadcasted_iota(jnp.int32, (TOP_K, tm), 0)
    idx = jnp.zeros((TOP_K, tm), F32)
    wts = jnp.zeros((TOP_K, tm), F32)
    member = jnp.zeros((N_EXPERTS, tm), F32)
    for j in range(TOP_K):
        _, ij = _first_argmax(cand, iota_e)
        hit = iota_e == ij
        wj = jnp.sum(jnp.where(hit, s, 0.0), axis=0, keepdims=True)
        cand = jnp.where(hit, -jnp.inf, cand)
        member = jnp.where(hit, 1.0, member)
        idx = jnp.where(iota_k == j, ij, idx)
        wts = jnp.where(iota_k == j, wj, wts)
    wts = wts / jnp.sum(wts, axis=0, keepdims=True) * ROUTED_SCALE
    idx_ref[...] = idx.astype(jnp.int32)
    wts_ref[...] = wts
    cnt_ref[0] = _mm_nt(jnp.ones((8, tm), BF16), member.astype(BF16))


def _route(sct, e_bias, tm):
    n_e, t = sct.shape
    nt = t // tm
    return pl.pallas_call(
        _route_kernel,
        grid=(nt,),
        in_specs=[pl.BlockSpec((n_e, tm), lambda i: (0, i)),
                  pl.BlockSpec((n_e, 1), lambda i: (0, 0))],
        out_specs=[pl.BlockSpec((TOP_K, tm), lambda i: (0, i)),
                   pl.BlockSpec((TOP_K, tm), lambda i: (0, i)),
                   pl.BlockSpec((1, 8, n_e), lambda i: (i, 0, 0))],
        out_shape=[jax.ShapeDtypeStruct((TOP_K, t), jnp.int32),
                   jax.ShapeDtypeStruct((TOP_K, t), F32),
                   jax.ShapeDtypeStruct((nt, 8, n_e), F32)],
        compiler_params=pltpu.CompilerParams(dimension_semantics=("parallel",),
                                             vmem_limit_bytes=VMEM_LIMIT),
        name="moe_route",
    )(sct, e_bias.reshape(n_e, 1).astype(F32))


def _dest_kernel(idx_ref, base_ref, dest_ref):
    idx = idx_ref[...].astype(F32)
    tm = idx.shape[1]
    iota_e = lax.broadcasted_iota(jnp.int32, (N_EXPERTS, tm), 0).astype(F32)
    hits = [iota_e == idx[j:j + 1, :] for j in range(TOP_K)]
    member = jnp.zeros((N_EXPERTS, tm), F32)
    for hit in hits:
        member = jnp.where(hit, 1.0, member)
    r = lax.broadcasted_iota(jnp.int32, (tm, tm), 0)
    c = lax.broadcasted_iota(jnp.int32, (tm, tm), 1)
    earlier = jnp.where(r < c, 1.0, 0.0).astype(BF16)
    rank = _mm(member.astype(BF16), earlier) + base_ref[0]
    iota_k = lax.broadcasted_iota(jnp.int32, (TOP_K, tm), 0)
    dest = jnp.zeros((TOP_K, tm), F32)
    for j, hit in enumerate(hits):
        dj = jnp.sum(jnp.where(hit, rank, 0.0), axis=0, keepdims=True)
        dest = jnp.where(iota_k == j, dj, dest)
    dest_ref[0] = dest.astype(jnp.int32)


def _dest_rows(idx, base, tm):
    t = idx.shape[1]
    nt = t // tm
    return pl.pallas_call(
        _dest_kernel,
        grid=(nt,),
        in_specs=[pl.BlockSpec((TOP_K, tm), lambda i: (0, i)),
                  pl.BlockSpec((1, N_EXPERTS, 1), lambda i: (i, 0, 0))],
        out_specs=pl.BlockSpec((1, TOP_K, tm), lambda i: (i, 0, 0)),
        out_shape=jax.ShapeDtypeStruct((nt, TOP_K, tm), jnp.int32),
        compiler_params=pltpu.CompilerParams(dimension_semantics=("parallel",),
                                             vmem_limit_bytes=VMEM_LIMIT),
        name="moe_dest",
    )(idx, base)


def _block_plan(tile_counts, n_tokens):
    n_blocks = (n_tokens * TOP_K + N_EXPERTS * (MOE_ROWS - 1) + MOE_ROWS - 1) // MOE_ROWS
    n_blocks = -(-n_blocks // MOE_GROUP) * MOE_GROUP
    counts = jnp.sum(tile_counts, axis=0)
    padded = (counts + MOE_ROWS - 1) // MOE_ROWS * MOE_ROWS
    extra = (-(jnp.sum(padded) // MOE_ROWS)) % MOE_GROUP
    padded = padded.at[N_EXPERTS - 1].add(extra * MOE_ROWS)
    pend = jnp.cumsum(padded)
    pstart = pend - padded
    tile_base = pstart[None, :] + jnp.cumsum(tile_counts, axis=0) - tile_counts
    block_start = jnp.arange(n_blocks, dtype=jnp.int32) * MOE_ROWS
    block_e = jnp.minimum(jnp.sum(pend[None, :] <= block_start[:, None], axis=1),
                          N_EXPERTS - 1).astype(jnp.int32)
    n_used = (pend[-1] // MOE_ROWS).astype(jnp.int32).reshape(1)
    return (tile_base.astype(F32)[:, :, None], counts.astype(jnp.int32), padded.astype(jnp.int32),
            pstart.astype(jnp.int32), block_e, n_used, n_blocks)


def _row_tile(ref, row):
    return ref.at[pl.ds(pl.multiple_of(row * SLABS, SLABS), SLABS)]


def _dispatch_kernel(cnt_ref, pad_ref, pst_ref, dest_hbm, x_ref, xs_hbm, dest_s, zrow, isem, csem, zsem,
                     *, tm):
    i = pl.program_id(0)
    n = pl.num_programs(0)
    cur = i % 2

    def idx_copy(tile, buf):
        return pltpu.make_async_copy(dest_hbm.at[tile], dest_s.at[buf], isem.at[buf])

    @pl.when(i == 0)
    def _():
        idx_copy(0, 0).start()
        zrow[...] = jnp.zeros_like(zrow)

        def per_expert(e, carry, wait):
            def per_row(r, c2):
                cp = pltpu.make_async_copy(zrow, _row_tile(xs_hbm, pst_ref[e] + r), zsem)
                if wait:
                    cp.wait()
                else:
                    cp.start()
                return c2
            return lax.fori_loop(cnt_ref[e], pad_ref[e], per_row, carry)

        lax.fori_loop(0, N_EXPERTS, functools.partial(per_expert, wait=False), 0)
        lax.fori_loop(0, N_EXPERTS, functools.partial(per_expert, wait=True), 0)

    idx_copy(i, cur).wait()

    @pl.when(i + 1 < n)
    def _():
        idx_copy(i + 1, 1 - cur).start()

    def per_token(t, carry):
        src = x_ref.at[pl.ds(pl.multiple_of(t * SLABS, SLABS), SLABS)]
        for j in range(TOP_K):
            pltpu.make_async_copy(src, _row_tile(xs_hbm, dest_s[cur, j, t]), csem).start()
        return carry

    lax.fori_loop(0, tm, per_token, 0)
    for j in range(TOP_K):
        pltpu.make_async_copy(x_ref, xs_hbm.at[pl.ds(0, tm * SLABS)], csem).wait()


def _dispatch(x1t, dest, counts, padded, pstart, n_blocks, tm):
    nt = dest.shape[0]
    any_spec = pl.BlockSpec(memory_space=pl.ANY)
    grid_spec = pltpu.PrefetchScalarGridSpec(
        num_scalar_prefetch=3,
        grid=(nt,),
        in_specs=[any_spec, pl.BlockSpec((tm * SLABS, LANES), lambda i, c, p, s: (i, 0))],
        out_specs=any_spec,
        scratch_shapes=[pltpu.SMEM((2, TOP_K, tm), jnp.int32),
                        pltpu.VMEM((SLABS, LANES), U32),
                        pltpu.SemaphoreType.DMA((2,)),
                        pltpu.SemaphoreType.DMA(()),
                        pltpu.SemaphoreType.DMA(())])
    return pl.pallas_call(
        functools.partial(_dispatch_kernel, tm=tm),
        grid_spec=grid_spec,
        out_shape=jax.ShapeDtypeStruct((n_blocks * MOE_ROWS * SLABS, LANES), U32),
        compiler_params=pltpu.CompilerParams(dimension_semantics=("arbitrary",),
                                             vmem_limit_bytes=VMEM_LIMIT),
        name="moe_dispatch",
    )(counts, padded, pstart, dest, x1t)


def _swiglu(xb, wg, wu, wd):
    hg = _mm(xb, wg)
    hu = _mm(xb, wu)
    return _mm((hg * jax.nn.sigmoid(hg) * hu).astype(BF16), wd)


def _experts_kernel(be_ref, nu_ref, xs_ref, *refs):
    w_refs, ys_ref = refs[:-1], refs[-1]
    s = pl.program_id(0)
    blk = MOE_ROWS * SLABS

    @pl.when(MOE_GROUP * s < nu_ref[0])
    def _():
        parts = [pl.ds(k * blk, blk) for k in range(MOE_GROUP)]
        wgu = [w_refs[2 * k] for k in range(MOE_GROUP)]
        wd = [w_refs[2 * k + 1] for k in range(MOE_GROUP)]
        xb, hh = {}, {}
        for step in range(MOE_GROUP + 2):
            k = step
            if k < MOE_GROUP:
                xb[k] = _from_token_tiles(xs_ref.at[parts[k]], MOE_ROWS).astype(BF16)
            k = step - 1
            if 0 <= k < MOE_GROUP:
                hgu = _mm(xb[k], wgu[k][0])
                hg, hu = hgu[:, :D_EXPERT], hgu[:, D_EXPERT:]
                hh[k] = (hg * jax.nn.sigmoid(hg) * hu).astype(BF16)
            k = step - 2
            if 0 <= k < MOE_GROUP:
                _to_token_tiles(ys_ref.at[parts[k]], _mm(hh[k], wd[k][0]))

    @pl.when(MOE_GROUP * s >= nu_ref[0])
    def _():
        ys_ref[...] = jnp.zeros_like(ys_ref)


def _experts(xs, block_e, n_used, w_gate_up_e, w_down_e):
    n_groups = block_e.shape[0] // MOE_GROUP
    blkg = MOE_GROUP * MOE_ROWS * SLABS
    wspec = lambda shp, k: pl.BlockSpec((1,) + shp, lambda s, be, nu: (be[MOE_GROUP * s + k], 0, 0))
    wspecs = [wspec(shp, k) for k in range(MOE_GROUP)
              for shp in ((D_MODEL, 2 * D_EXPERT), (D_EXPERT, D_MODEL))]
    grid_spec = pltpu.PrefetchScalarGridSpec(
        num_scalar_prefetch=2,
        grid=(n_groups,),
        in_specs=[pl.BlockSpec((blkg, LANES),
                               lambda s, be, nu: (jnp.minimum(s, nu[0] // MOE_GROUP - 1), 0))] + wspecs,
        out_specs=pl.BlockSpec((blkg, LANES), lambda s, be, nu: (s, 0)))
    return pl.pallas_call(
        _experts_kernel,
        grid_spec=grid_spec,
        out_shape=jax.ShapeDtypeStruct(xs.shape, xs.dtype),
        compiler_params=pltpu.CompilerParams(dimension_semantics=("arbitrary",),
                                             vmem_limit_bytes=VMEM_LIMIT),
        name="moe_experts",
    )(block_e, n_used, xs, *([w_gate_up_e, w_down_e] * MOE_GROUP))


def _final_kernel(dest_hbm, ys_hbm, x1_ref, wts_ref, wg_ref, wu_ref, wd_ref, g2_ref, b2_ref, o_ref,
                  dest_s, gbuf, isem, gsem, *, tm):
    i = pl.program_id(0)
    n = pl.num_programs(0)
    cur = i % 2
    nxt = 1 - cur

    def idx_copy(tile, buf):
        return pltpu.make_async_copy(dest_hbm.at[tile], dest_s.at[buf], isem.at[buf])

    def start_gather(buf):
        def per_token(t, carry):
            for j in range(TOP_K):
                dst = gbuf.at[buf, j, pl.ds(pl.multiple_of(t * SLABS, SLABS), SLABS)]
                pltpu.make_async_copy(_row_tile(ys_hbm, dest_s[buf, j, t]), dst, gsem.at[buf]).start()
            return carry
        lax.fori_loop(0, tm, per_token, 0)

    @pl.when(i == 0)
    def _():
        idx_copy(0, 0).start()
        idx_copy(0, 0).wait()
        start_gather(0)

        @pl.when(1 < n)
        def _():
            idx_copy(1, 1).start()

    @pl.when(i + 1 < n)
    def _():
        idx_copy(i + 1, nxt).wait()
        start_gather(nxt)

        @pl.when(i + 2 < n)
        def _():
            idx_copy(i + 2, cur).start()

    x1 = x1_ref[...]
    y = _swiglu(x1.astype(BF16), wg_ref[...], wu_ref[...], wd_ref[...])
    r = lax.broadcasted_iota(jnp.int32, (tm, tm), 0)
    c = lax.broadcasted_iota(jnp.int32, (tm, tm), 1)
    w_cols = _mm_nt(jnp.where(r == c, 1.0, 0.0).astype(F32), wts_ref[...], HI)
    for j in range(TOP_K):
        pltpu.make_async_copy(ys_hbm.at[pl.ds(0, tm * SLABS)], gbuf.at[cur, j], gsem.at[cur]).wait()
    for j in range(TOP_K):
        y = y + w_cols[:, j:j + 1] * _from_token_tiles(gbuf.at[cur, j], tm)
    o_ref[...] = _ln(ALPHA * x1 + y, g2_ref[...], b2_ref[...])


def _final(dest, ys, x1, wts, w_gate_s, w_up_s, w_down_s, ln2_g, ln2_b, tm):
    t = x1.shape[0]
    any_spec = pl.BlockSpec(memory_space=pl.ANY)
    row = pl.BlockSpec((tm, D_MODEL), lambda i: (i, 0))
    full = lambda a: pl.BlockSpec(a.shape, lambda i: (0, 0))
    return pl.pallas_call(
        functools.partial(_final_kernel, tm=tm),
        grid=(t // tm,),
        in_specs=[any_spec, any_spec, row, pl.BlockSpec((TOP_K, tm), lambda i: (0, i)),
                  full(w_gate_s), full(w_up_s), full(w_down_s), full(ln2_g), full(ln2_b)],
        out_specs=row,
        out_shape=jax.ShapeDtypeStruct((t, D_MODEL), F32),
        scratch_shapes=[pltpu.SMEM((2, TOP_K, tm), jnp.int32),
                        pltpu.VMEM((2, TOP_K, tm * SLABS, LANES), U32),
                        pltpu.SemaphoreType.DMA((2,)),
                        pltpu.SemaphoreType.DMA((2,))],
        compiler_params=pltpu.CompilerParams(dimension_semantics=("arbitrary",),
                                             vmem_limit_bytes=VMEM_LIMIT),
        name="shared_combine_ln2",
    )(dest, ys, x1, wts, w_gate_s, w_up_s, w_down_s, ln2_g, ln2_b)


def _trunk(x, prm):
    b, l, d = x.shape
    t = b * l
    tm = 512
    row2 = lambda a: a.reshape(1, -1).astype(F32)
    xn, p_rwkv, q, k, v, ga, gn = _ln_proj(x.reshape(t, d), row2(prm['ln_in_g']), row2(prm['ln_in_b']),
                                           prm['w_in_rwkv'], prm['w_in_na'], prm['w_in_gate'], tm)
    wkv_f, wkv_b, bonus, gate = _rwkv_branch(p_rwkv, b, l, prm, tm)
    y_n = _na_attention(q.reshape(b, l, D_NA), k.reshape(b, l, D_NA), v.reshape(b, l, D_NA),
                        prm['na_bias'])
    x1, x1t, sct = _merge(wkv_f, wkv_b, bonus, gate, row2(prm['lnx_g']), row2(prm['lnx_b']),
                          y_n.reshape(t, D_NA), ga, gn, xn, prm['w_up_a'], prm['w_up_n'], prm['w_out'],
                          row2(prm['ln1_g']), row2(prm['ln1_b']), prm['w_router_t'], tm)
    return _moe(x1, x1t, sct, prm).reshape(b, l, d)


def _moe(x1, x1t, sct, prm):
    t = x1.shape[0]
    tm = MOE_TILE
    idx, wts, tile_cnt = _route(sct, prm['e_bias'], tm)
    tile_base, counts, padded, pstart, block_e, n_used, n_blocks = _block_plan(
        tile_cnt[:, 0, :].astype(jnp.int32), t)
    dest = _dest_rows(idx, tile_base, tm)
    xs = _dispatch(x1t, dest, counts, padded, pstart, n_blocks, tm)
    ys = _experts(xs, block_e, n_used, prm['w_gate_up_e'], prm['w_down_e'])
    row2 = lambda a: a.reshape(1, -1).astype(F32)
    return _final(dest, ys, x1, wts, prm['w_gate_s'], prm['w_up_s'], prm['w_down_s'],
                  row2(prm['ln2_g']), row2(prm['ln2_b']), tm)


def _split_bf16(w):
    hi = w.astype(BF16)
    return jnp.stack([hi, (w - hi.astype(F32)).astype(BF16)])


def kernel(x_prompt, x_sample, ln_in_g, ln_in_b, w_in, mu_shift, w0, w2, a0, a2, g2, k_k, k_a, r_k,
           lnx_g, lnx_b, rpb, w_up_a, w_up_n, w_out, ln1_g, ln1_b, w_router, e_bias,
           w_gate_e, w_up_e, w_down_e, w_gate_s, w_up_s, w_down_s, ln2_g, ln2_b):
    assert w_in.shape[0] == DEPTH == 1
    w_in0 = w_in[0].astype(BF16)
    prm = dict(
        ln_in_g=ln_in_g, ln_in_b=ln_in_b,
        w_in_rwkv=w_in0[:, :RWKV_IN], w_in_na=w_in0[:, RWKV_IN:RWKV_IN + NA_IN],
        w_in_gate=w_in0[:, RWKV_IN + NA_IN:],
        mu_shift=mu_shift[0], w0=w0[0], w2=w2[0], a0=a0[0], a2=a2[0], g2=g2[0], k_k=k_k[0], k_a=k_a[0],
        r_k=r_k[0], lnx_g=lnx_g[0], lnx_b=lnx_b[0], na_bias=_na_bias_table(rpb[0]),
        w_up_a=w_up_a[0].astype(BF16), w_up_n=w_up_n[0].astype(BF16), w_out=w_out[0].astype(BF16),
        ln1_g=ln1_g[0], ln1_b=ln1_b[0], w_router_t=_split_bf16(w_router[0].T), e_bias=e_bias[0],
        w_gate_up_e=jnp.concatenate([w_gate_e[0].astype(BF16), w_up_e[0].astype(BF16)], axis=-1),
        w_down_e=w_down_e[0].astype(BF16),
        w_gate_s=w_gate_s[0].astype(BF16), w_up_s=w_up_s[0].astype(BF16),
        w_down_s=w_down_s[0].astype(BF16), ln2_g=ln2_g[0], ln2_b=ln2_b[0])
    return (_trunk(x_prompt, prm), _trunk(x_sample, prm))
```
